```python
import math
import jax, jax.numpy as jnp
from jax import lax
import numpy as np

D_MODEL = 4096
BATCH = 2
SEQ = 4096
DEPTH = 1
DEC_BATCH = 128
DEC_SEQ = 8
PAST_LEN = 8192
PAGE_SIZE = 128

POOL_WIDTH = D_MODEL // 2
POOL_WINDOWS = (2, 4, 8, 16)
POOL_GROUPS = len(POOL_WINDOWS)
POOL_GROUP_WIDTH = POOL_WIDTH // POOL_GROUPS
POOL_STATE = max(POOL_WINDOWS) - 1
HEAD_DIM = 64
N_HEADS = (D_MODEL // 2) // HEAD_DIM
N_KV_HEADS = N_HEADS // 8
GQA_GROUP = N_HEADS // N_KV_HEADS
ATTN_WIDTH = N_HEADS * HEAD_DIM
KV_WIDTH = N_KV_HEADS * HEAD_DIM
WINDOW = 128
BLOCK = WINDOW
SPLITS = (POOL_WIDTH, POOL_WIDTH, ATTN_WIDTH, KV_WIDTH, KV_WIDTH, ATTN_WIDTH, D_MODEL, D_MODEL)
N_IN = sum(SPLITS)
DEEPNORM_ALPHA = (2.0 * DEPTH) ** 0.25
DEEPNORM_BETA = (8.0 * DEPTH) ** -0.25
LN_EPS = 1e-5

kernel_name = "hybrid_pool_swa_gated_decoder_step"


def _alibi_slopes():
    h = jnp.arange(N_HEADS, dtype=jnp.float32)
    return (2.0 ** (-8.0 * (h + 1.0) / N_HEADS)).reshape(N_KV_HEADS, GQA_GROUP)


def _project_in(x, w_in):
    proj = jnp.einsum('btd,dn->btn', x, w_in)
    idx = np.cumsum(SPLITS)[:-1].tolist()
    return jnp.split(proj, idx, axis=-1)


def _pool_mixer(u_ext, start_pos, n_new, w_pool, pool_scale):
    b = u_ext.shape[0]
    uf = u_ext.astype(jnp.float32)
    csum = jnp.concatenate([jnp.zeros((b, 1, POOL_WIDTH), jnp.float32), jnp.cumsum(uf, axis=1)], axis=1)
    end = csum[:, POOL_STATE + 1:]
    u_new = uf[:, POOL_STATE:]
    pos = start_pos + jnp.arange(n_new)
    outs = []
    for g, w in enumerate(POOL_WINDOWS):
        sl = slice(g * POOL_GROUP_WIDTH, (g + 1) * POOL_GROUP_WIDTH)
        begin = csum[:, POOL_STATE + 1 - w: POOL_STATE + 1 - w + n_new, sl]
        count = jnp.minimum(w, pos + 1).astype(jnp.float32)[None, :, None]
        outs.append((end[..., sl] - begin) / count - u_new[..., sl])
    pooled = jnp.stack(outs, axis=2).astype(u_ext.dtype)
    mixed = jnp.einsum('btgc,gce->btge', pooled, w_pool)
    return mixed.reshape(b, n_new, POOL_WIDTH) * pool_scale


def _window_attention(q, k, v, dist, valid, slopes, sinks):
    logits = jnp.einsum('bntkgd,bnskd->bnkgts', q, k, preferred_element_type=jnp.float32) * (HEAD_DIM ** -0.5)
    logits = logits - slopes[None, None, :, :, None, None] * dist[None, :, None, None, :, :]
    logits = jnp.where(valid[None, :, None, None, :, :], logits, -jnp.inf)
    sink = sinks.astype(jnp.float32).reshape(N_KV_HEADS, GQA_GROUP)[None, None, :, :, None, None]
    m = jnp.maximum(logits.max(axis=-1, keepdims=True), sink)
    p = jnp.exp(logits - m)
    probs = (p / (p.sum(axis=-1, keepdims=True) + jnp.exp(sink - m))).astype(v.dtype)
    return jnp.einsum('bnkgts,bnskd->bntkgd', probs, v)


def _prompt_attention(q, k, v, slopes, sinks):
    b, t = q.shape[:2]
    nb = t // BLOCK
    qb = q.reshape(b, nb, BLOCK, N_KV_HEADS, GQA_GROUP, HEAD_DIM)

    def band(a):
        prev = jnp.pad(a, ((0, 0), (BLOCK, 0), (0, 0), (0, 0)))[:, :t]
        return jnp.concatenate([prev.reshape(b, nb, BLOCK, N_KV_HEADS, HEAD_DIM),
                                a.reshape(b, nb, BLOCK, N_KV_HEADS, HEAD_DIM)], axis=2)

    qi = jnp.arange(BLOCK)[:, None]
    kj = jnp.arange(2 * BLOCK)[None, :]
    dist = BLOCK + qi - kj
    key_pos = (jnp.arange(nb)[:, None, None] - 1) * BLOCK + kj[None]
    valid = (dist >= 0) & (dist < WINDOW) & (key_pos >= 0)
    out = _window_attention(qb, band(k), band(v), dist[None].astype(jnp.float32), valid, slopes, sinks)
    return out.reshape(b, t, ATTN_WIDTH)


def _sample_attention(q, k_all, v_all, slopes, sinks):
    b, t = q.shape[:2]
    s = k_all.shape[1]
    qi = jnp.arange(t)[:, None]
    kj = jnp.arange(s)[None, :]
    dist = WINDOW + qi - kj
    key_pos = PAST_LEN - WINDOW + kj
    valid = (dist >= 0) & (dist < WINDOW) & (key_pos >= 0)
    qb = q.reshape(b, 1, t, N_KV_HEADS, GQA_GROUP, HEAD_DIM)
    out = _window_attention(qb, k_all[:, None], v_all[:, None], dist[None].astype(jnp.float32), valid[None], slopes, sinks)
    return out.reshape(b, t, ATTN_WIDTH)


def _layer_norm(x, g, b):
    xf = x.astype(jnp.float32)
    mu = xf.mean(axis=-1, keepdims=True)
    var = jnp.square(xf - mu).mean(axis=-1, keepdims=True)
    return ((xf - mu) * lax.rsqrt(var + LN_EPS) * g + b).astype(x.dtype)


def _merge_and_norm(x, pool_out, z_a, attn_out, z_b, g_a, g_b, w_out_pool, w_out_attn, w_out, ln_gain, ln_bias):
    y_a = jnp.einsum('btc,cd->btd', jax.nn.silu(z_a) * pool_out, w_out_pool)
    y_b = jnp.einsum('btc,cd->btd', jax.nn.silu(z_b) * attn_out, w_out_attn)
    h = jax.nn.sigmoid(g_a) * y_a + jax.nn.sigmoid(g_b) * y_b
    out = jnp.einsum('btc,cd->btd', h, w_out)
    return _layer_norm(DEEPNORM_ALPHA * x + out, ln_gain, ln_bias)


def setup_inputs(seed: int = 0) -> dict:
    key = jax.random.key(seed)
    ks = jax.random.split(key, 16)
    f32 = jnp.float32
    v_start = sum(SPLITS[:4])
    col_scale = np.ones((N_IN,), np.float32)
    col_scale[v_start:v_start + KV_WIDTH] = DEEPNORM_BETA
    return {
        "x_prompt": jax.random.normal(ks[0], (BATCH, SEQ, D_MODEL), f32),
        "x_sample": jax.random.normal(ks[1], (DEC_BATCH, DEC_SEQ, D_MODEL), f32),
        "cache_k": jax.random.normal(ks[2], (DEPTH, DEC_BATCH, WINDOW, N_KV_HEADS, HEAD_DIM), f32),
        "cache_v": jax.random.normal(ks[3], (DEPTH, DEC_BATCH, WINDOW, N_KV_HEADS, HEAD_DIM), f32) * DEEPNORM_BETA,
        "state_pool": jax.random.normal(ks[4], (DEPTH, DEC_BATCH, POOL_STATE, POOL_WIDTH), f32),
        "w_in": jax.random.normal(ks[5], (DEPTH, D_MODEL, N_IN), f32) * (D_MODEL ** -0.5) * jnp.asarray(col_scale),
        "w_pool": jax.random.normal(ks[6], (DEPTH, POOL_GROUPS, POOL_GROUP_WIDTH, POOL_GROUP_WIDTH), f32) * (POOL_GROUP_WIDTH ** -0.5),
        "pool_scale": 1.0 + 0.02 * jax.random.normal(ks[7], (DEPTH, POOL_WIDTH), f32),
        "sinks": 0.5 * jax.random.normal(ks[8], (DEPTH, N_HEADS), f32),
        "w_out_pool": jax.random.normal(ks[9], (DEPTH, POOL_WIDTH, D_MODEL), f32) * (POOL_WIDTH ** -0.5) * DEEPNORM_BETA,
        "w_out_attn": jax.random.normal(ks[10], (DEPTH, ATTN_WIDTH, D_MODEL), f32) * (ATTN_WIDTH ** -0.5) * DEEPNORM_BETA,
        "w_out": jax.random.normal(ks[11], (DEPTH, D_MODEL, D_MODEL), f32) * (D_MODEL ** -0.5) * DEEPNORM_BETA,
        "ln_gain": 1.0 + 0.02 * jax.random.normal(ks[12], (DEPTH, D_MODEL), f32),
        "ln_bias": 0.02 * jax.random.normal(ks[13], (DEPTH, D_MODEL), f32),
    }


def reference(x_prompt, x_sample, cache_k, cache_v, state_pool, w_in, w_pool, pool_scale, sinks,
              w_out_pool, w_out_attn, w_out, ln_gain, ln_bias):
    slopes = _alibi_slopes()
    xp, xs = x_prompt, x_sample
    bp, tp = xp.shape[:2]
    bs, ts = xs.shape[:2]
    kp_l, vp_l, pp_l, ks_l, vs_l, ps_l = [], [], [], [], [], []
    for l in range(DEPTH):
        u, z_a, q, k, v, z_b, g_a, g_b = _project_in(xp, w_in[l])
        k = k.reshape(bp, tp, N_KV_HEADS, HEAD_DIM)
        v = v.reshape(bp, tp, N_KV_HEADS, HEAD_DIM)
        u_ext = jnp.concatenate([jnp.zeros((bp, POOL_STATE, POOL_WIDTH), u.dtype), u], axis=1)
        pool_out = _pool_mixer(u_ext, 0, tp, w_pool[l], pool_scale[l])
        attn_out = _prompt_attention(q, k, v, slopes, sinks[l])
        kp_l.append(k[:, -WINDOW:])
        vp_l.append(v[:, -WINDOW:])
        pp_l.append(u_ext[:, -POOL_STATE:])
        xp = _merge_and_norm(xp, pool_out, z_a, attn_out, z_b, g_a, g_b,
                             w_out_pool[l], w_out_attn[l], w_out[l], ln_gain[l], ln_bias[l])

        u, z_a, q, k, v, z_b, g_a, g_b = _project_in(xs, w_in[l])
        k = k.reshape(bs, ts, N_KV_HEADS, HEAD_DIM)
        v = v.reshape(bs, ts, N_KV_HEADS, HEAD_DIM)
        u_ext = jnp.concatenate([state_pool[l].astype(u.dtype), u], axis=1)
        pool_out = _pool_mixer(u_ext, PAST_LEN, ts, w_pool[l], pool_scale[l])
        k_all = jnp.concatenate([cache_k[l].astype(k.dtype), k], axis=1)
        v_all = jnp.concatenate([cache_v[l].astype(v.dtype), v], axis=1)
        attn_out = _sample_attention(q, k_all, v_all, slopes, sinks[l])
        ks_l.append(k_all[:, -WINDOW:])
        vs_l.append(v_all[:, -WINDOW:])
        ps_l.append(u_ext[:, -POOL_STATE:])
        xs = _merge_and_norm(xs, pool_out, z_a, attn_out, z_b, g_a, g_b,
                             w_out_pool[l], w_out_attn[l], w_out[l], ln_gain[l], ln_bias[l])
    return (xp, xs, jnp.stack(kp_l), jnp.stack(vp_l), jnp.stack(pp_l),
            jnp.stack(ks_l), jnp.stack(vs_l), jnp.stack(ps_l))
```

```python
import functools

import jax
import jax.numpy as jnp
from jax import lax
from jax.experimental import pallas as pl
from jax.experimental.pallas import tpu as pltpu

D_MODEL = 4096
POOL_WIDTH = 2048
POOL_WINDOWS = (2, 4, 8, 16)
POOL_GROUP_WIDTH = 512
POOL_STATE = 15
HEAD_DIM = 64
N_HEADS = 32
N_KV_HEADS = 4
GQA_GROUP = 8
ATTN_WIDTH = 2048
KV_WIDTH = 256
WINDOW = 128
PAST_LEN = 8192
N_IN = 16896
COL_U, COL_ZA, COL_Q, COL_K, COL_V, COL_ZB, COL_GA, COL_GB = 0, 2048, 4096, 6144, 6400, 6656, 8704, 12800
DEEPNORM_ALPHA = 2.0 ** 0.25
LN_EPS = 1e-5

V7X_VMEM_BYTES = 64 * 1024 * 1024
SUBLANES = 8
HALO_ROWS = 16

_F32 = jnp.float32
_BF16 = jnp.bfloat16


def _params(sem, vmem_mib):
    return pltpu.CompilerParams(dimension_semantics=sem, vmem_limit_bytes=vmem_mib * 1024 * 1024)


def _proj_kernel(x_ref, w_ref, o_ref):
    o_ref[...] = jnp.dot(x_ref[...], w_ref[...], preferred_element_type=_F32)


def _proj(x_bf16, w_bf16, tm=1024, tn=768):
    rows, k = x_bf16.shape
    n = w_bf16.shape[1]
    return pl.pallas_call(
        _proj_kernel,
        grid=(rows // tm, n // tn),
        in_specs=[pl.BlockSpec((tm, k), lambda i, j: (i, 0)),
                  pl.BlockSpec((k, tn), lambda i, j: (0, j))],
        out_specs=pl.BlockSpec((tm, tn), lambda i, j: (i, j)),
        out_shape=jax.ShapeDtypeStruct((rows, n), _F32),
        compiler_params=_params(("arbitrary", "arbitrary"), 48),
        name="proj",
    )(x_bf16, w_bf16)


def _window_sums(ext, n_doublings):
    s = ext
    for d in range(n_doublings):
        s = s + pltpu.roll(s, 1 << d, axis=0)
    return s


def _pool_groups(ext, u, z_a, pos, wp_ref, scale_ref, halo):
    outs = []
    for g, w in enumerate(POOL_WINDOWS):
        sl = slice(g * POOL_GROUP_WIDTH, (g + 1) * POOL_GROUP_WIDTH)
        s = _window_sums(ext[:, sl], g + 1)[halo:]
        count = jnp.minimum(w, pos + 1).astype(_F32)
        pooled = s / count - u[:, sl]
        mixed = jnp.dot(pooled.astype(_BF16), wp_ref[g], preferred_element_type=_F32)
        pool_out = mixed * scale_ref[:, sl]
        z = z_a[:, sl]
        outs.append(z * jax.nn.sigmoid(z) * pool_out)
    return outs


def _pool_prompt_kernel(u_ref, halo_ref, za_ref, wp_ref, scale_ref, a_ref, *, tiles_per_seq, tm):
    t = pl.program_id(0) % tiles_per_seq
    u = u_ref[...]
    halo = jnp.where(t == 0, 0.0, halo_ref[...])
    ext = jnp.concatenate([halo, u], axis=0)
    pos = t * tm + lax.broadcasted_iota(jnp.int32, (tm, 1), 0)
    outs = _pool_groups(ext, u, za_ref[...], pos, wp_ref, scale_ref, HALO_ROWS)
    for g, o in enumerate(outs):
        a_ref[:, g * POOL_GROUP_WIDTH:(g + 1) * POOL_GROUP_WIDTH] = o.astype(_BF16)


def _pool_prompt(proj, w_pool_bf16, pool_scale, seq, tm=256):
    rows = proj.shape[0]
    tiles_per_seq = seq // tm
    halo_blocks = tm // HALO_ROWS
    kern = functools.partial(_pool_prompt_kernel, tiles_per_seq=tiles_per_seq, tm=tm)
    return pl.pallas_call(
        kern,
        grid=(rows // tm,),
        in_specs=[
            pl.BlockSpec((tm, POOL_WIDTH), lambda i: (i, COL_U // POOL_WIDTH)),
            pl.BlockSpec((HALO_ROWS, POOL_WIDTH), lambda i: (jnp.maximum(i * halo_blocks - 1, 0), 0)),
            pl.BlockSpec((tm, POOL_WIDTH), lambda i: (i, COL_ZA // POOL_WIDTH)),
            pl.BlockSpec((4, POOL_GROUP_WIDTH, POOL_GROUP_WIDTH), lambda i: (0, 0, 0)),
            pl.BlockSpec((1, POOL_WIDTH), lambda i: (0, 0)),
        ],
        out_specs=pl.BlockSpec((tm, POOL_WIDTH), lambda i: (i, 0)),
        out_shape=jax.ShapeDtypeStruct((rows, POOL_WIDTH), _BF16),
        compiler_params=_params(("arbitrary",), 48),
        name="pool_prompt",
    )(proj, proj, proj, w_pool_bf16, pool_scale)


def _pool_sample_kernel(u_ref, za_ref, st_ref, wp_ref, scale_ref, a_ref, np_ref, *, bs, t_new):
    u3 = u_ref[...]
    st = st_ref[...]
    rows_per_seq = HALO_ROWS + t_new
    ext = jnp.concatenate([st, u3], axis=1).reshape(bs * rows_per_seq, POOL_WIDTH)
    pos3 = PAST_LEN + lax.broadcasted_iota(jnp.int32, (bs, t_new, 1), 1)
    pos = pos3.reshape(bs * t_new, 1)
    u = u3.reshape(bs * t_new, POOL_WIDTH)
    za = za_ref[...].reshape(bs * t_new, POOL_WIDTH)
    for g, w in enumerate(POOL_WINDOWS):
        sl = slice(g * POOL_GROUP_WIDTH, (g + 1) * POOL_GROUP_WIDTH)
        s = _window_sums(ext[:, sl], g + 1)
        s = s.reshape(bs, rows_per_seq, POOL_GROUP_WIDTH)[:, HALO_ROWS:, :].reshape(bs * t_new, POOL_GROUP_WIDTH)
        count = jnp.minimum(w, pos + 1).astype(_F32)
        pooled = s / count - u[:, sl]
        mixed = jnp.dot(pooled.astype(_BF16), wp_ref[g], preferred_element_type=_F32)
        pool_out = mixed * scale_ref[:, sl]
        z = za[:, sl]
        a_ref[:, sl] = (z * jax.nn.sigmoid(z) * pool_out).astype(_BF16)
    keep = POOL_STATE - t_new
    np_ref[:, 0:keep, :] = st[:, HALO_ROWS - keep:, :]
    np_ref[:, keep:, :] = u3


def _pool_sample(proj3, state16, w_pool_bf16, pool_scale, bs=16):
    nseq, t_new, _ = proj3.shape
    kern = functools.partial(_pool_sample_kernel, bs=bs, t_new=t_new)
    return pl.pallas_call(
        kern,
        grid=(nseq // bs,),
        in_specs=[
            pl.BlockSpec((bs, t_new, POOL_WIDTH), lambda i: (i, 0, COL_U // POOL_WIDTH)),
            pl.BlockSpec((bs, t_new, POOL_WIDTH), lambda i: (i, 0, COL_ZA // POOL_WIDTH)),
            pl.BlockSpec((bs, HALO_ROWS, POOL_WIDTH), lambda i: (i, 0, 0)),
            pl.BlockSpec((4, POOL_GROUP_WIDTH, POOL_GROUP_WIDTH), lambda i: (0, 0, 0)),
            pl.BlockSpec((1, POOL_WIDTH), lambda i: (0, 0)),
        ],
        out_specs=[
            pl.BlockSpec((bs * t_new, POOL_WIDTH), lambda i: (i, 0)),
            pl.BlockSpec((bs, POOL_STATE, POOL_WIDTH), lambda i: (i, 0, 0)),
        ],
        out_shape=[
            jax.ShapeDtypeStruct((nseq * t_new, POOL_WIDTH), _BF16),
            jax.ShapeDtypeStruct((nseq, POOL_STATE, POOL_WIDTH), _F32),
        ],
        compiler_params=_params(("arbitrary",), 48),
        name="pool_sample",
    )(proj3, proj3, state16, w_pool_bf16, pool_scale)


def _softmax_with_sink(l, sink):
    m = jnp.maximum(jnp.max(l, axis=-1, keepdims=True), sink)
    p = jnp.exp(l - m)
    den = jnp.sum(p, axis=-1, keepdims=True) + jnp.exp(sink - m)
    return p / den


def _attn_prompt_kernel(slopes_ref, sinks_ref, q_ref, kvc_ref, kvp_ref, zb0, zb1, zb2, zb3, o_ref):
    n = pl.program_id(1)
    blk = WINDOW
    ti = lax.broadcasted_iota(jnp.int32, (blk, 2 * blk), 0)
    si = lax.broadcasted_iota(jnp.int32, (blk, 2 * blk), 1)
    dist = blk + ti - si
    key_pos = (n - 1) * blk + si
    valid = (dist >= 0) & (dist < WINDOW) & (key_pos >= 0)
    distf = dist.astype(_F32)
    kv = jnp.concatenate([kvp_ref[...], kvc_ref[...]], axis=0).astype(_BF16)
    zbs = (zb0, zb1, zb2, zb3)
    for kh in range(N_KV_HEADS):
        k_kh = kv[:, kh * HEAD_DIM:(kh + 1) * HEAD_DIM]
        v_kh = kv[:, KV_WIDTH + kh * HEAD_DIM:KV_WIDTH + (kh + 1) * HEAD_DIM]
        q_kh = q_ref[:, kh * GQA_GROUP * HEAD_DIM:(kh + 1) * GQA_GROUP * HEAD_DIM].astype(_BF16)
        qs = jnp.concatenate([q_kh[:, g * HEAD_DIM:(g + 1) * HEAD_DIM] for g in range(GQA_GROUP)], axis=0)
        logits = lax.dot_general(qs, k_kh, (((1,), (1,)), ((), ())), preferred_element_type=_F32)
        probs = []
        for g in range(GQA_GROUP):
            h = kh * GQA_GROUP + g
            l = logits[g * blk:(g + 1) * blk] * (HEAD_DIM ** -0.5) - slopes_ref[h] * distf
            l = jnp.where(valid, l, -jnp.inf)
            probs.append(_softmax_with_sink(l, sinks_ref[h]).astype(_BF16))
        pr = jnp.concatenate(probs, axis=0)
        o = jnp.dot(pr, v_kh, preferred_element_type=_F32)
        o_kh = jnp.concatenate([o[g * blk:(g + 1) * blk] for g in range(GQA_GROUP)], axis=1)
        zb = zbs[kh][...]
        w = GQA_GROUP * HEAD_DIM
        o_ref[:, kh * w:(kh + 1) * w] = (zb * jax.nn.sigmoid(zb) * o_kh).astype(_BF16)


def _attn_prompt(proj, slopes, sinks, batch, seq):
    blk = WINDOW
    nb = seq // blk
    gw = GQA_GROUP * HEAD_DIM
    smem = pl.BlockSpec(memory_space=pltpu.SMEM)

    def zb_spec(kh):
        return pl.BlockSpec((blk, gw), lambda b, n: (b * nb + n, COL_ZB // gw + kh))

    return pl.pallas_call(
        _attn_prompt_kernel,
        grid=(batch, nb),
        in_specs=[
            smem, smem,
            pl.BlockSpec((blk, ATTN_WIDTH), lambda b, n: (b * nb + n, COL_Q // ATTN_WIDTH)),
            pl.BlockSpec((blk, 2 * KV_WIDTH), lambda b, n: (b * nb + n, COL_K // (2 * KV_WIDTH))),
            pl.BlockSpec((blk, 2 * KV_WIDTH), lambda b, n: (b * nb + jnp.maximum(n - 1, 0), COL_K // (2 * KV_WIDTH))),
            zb_spec(0), zb_spec(1), zb_spec(2), zb_spec(3),
        ],
        out_specs=pl.BlockSpec((blk, ATTN_WIDTH), lambda b, n: (b * nb + n, 0)),
        out_shape=jax.ShapeDtypeStruct((batch * seq, ATTN_WIDTH), _BF16),
        compiler_params=_params(("arbitrary", "arbitrary"), 48),
        name="attn_prompt",
    )(slopes, sinks, proj, proj, proj, proj, proj, proj, proj)


def _attn_sample_kernel(slopes_ref, sinks_ref, q_ref, kvn_ref, ck_ref, cv_ref, zb0, zb1, zb2, zb3,
                        o_ref, nk_ref, nv_ref, *, bs, t_new):
    s_all = WINDOW + t_new
    rows = GQA_GROUP * t_new
    ti = lax.broadcasted_iota(jnp.int32, (t_new, s_all), 0)
    si = lax.broadcasted_iota(jnp.int32, (t_new, s_all), 1)
    dist = WINDOW + ti - si
    key_pos = PAST_LEN - WINDOW + si
    valid = (dist >= 0) & (dist < WINDOW) & (key_pos >= 0)
    distf = dist.astype(_F32)
    zbs = (zb0, zb1, zb2, zb3)
    gw = GQA_GROUP * HEAD_DIM

    def body(b, carry):
        k_all = jnp.concatenate([ck_ref[b], kvn_ref[b][:, 0:KV_WIDTH]], axis=0)
        v_all = jnp.concatenate([cv_ref[b], kvn_ref[b][:, KV_WIDTH:2 * KV_WIDTH]], axis=0)
        nk_ref[b] = k_all[t_new:, :]
        nv_ref[b] = v_all[t_new:, :]
        kb = k_all.astype(_BF16)
        vb = v_all.astype(_BF16)
        qb = q_ref[b].astype(_BF16)
        for kh in range(N_KV_HEADS):
            k_kh = kb[:, kh * HEAD_DIM:(kh + 1) * HEAD_DIM]
            v_kh = vb[:, kh * HEAD_DIM:(kh + 1) * HEAD_DIM]
            qs = jnp.concatenate(
                [qb[:, (kh * GQA_GROUP + g) * HEAD_DIM:(kh * GQA_GROUP + g + 1) * HEAD_DIM]
                 for g in range(GQA_GROUP)], axis=0)
            logits = lax.dot_general(qs, k_kh, (((1,), (1,)), ((), ())), preferred_element_type=_F32)
            probs = []
            for g in range(GQA_GROUP):
                h = kh * GQA_GROUP + g
                l = logits[g * t_new:(g + 1) * t_new] * (HEAD_DIM ** -0.5) - slopes_ref[h] * distf
                l = jnp.where(valid, l, -jnp.inf)
                probs.append(_softmax_with_sink(l, sinks_ref[h]))
            pr = jnp.concatenate(probs, axis=0).astype(_BF16)
            o = jnp.dot(pr, v_kh, preferred_element_type=_F32)
            o_kh = jnp.concatenate([o[g * t_new:(g + 1) * t_new] for g in range(GQA_GROUP)], axis=1)
            zb = zbs[kh][b]
            o_ref[b, :, kh * gw:(kh + 1) * gw] = zb * jax.nn.sigmoid(zb) * o_kh
        return carry

    lax.fori_loop(0, bs, body, 0)


def _attn_sample(proj3, cache_k, cache_v, slopes, sinks, bs=16):
    nseq, t_new, _ = proj3.shape
    gw = GQA_GROUP * HEAD_DIM
    smem = pl.BlockSpec(memory_space=pltpu.SMEM)
    kern = functools.partial(_attn_sample_kernel, bs=bs, t_new=t_new)

    def zb_spec(kh):
        return pl.BlockSpec((bs, t_new, gw), lambda i: (i, 0, COL_ZB // gw + kh))

    cache_spec = pl.BlockSpec((bs, WINDOW, KV_WIDTH), lambda i: (i, 0, 0))
    return pl.pallas_call(
        kern,
        grid=(nseq // bs,),
        in_specs=[
            smem, smem,
            pl.BlockSpec((bs, t_new, ATTN_WIDTH), lambda i: (i, 0, COL_Q // ATTN_WIDTH)),
            pl.BlockSpec((bs, t_new, 2 * KV_WIDTH), lambda i: (i, 0, COL_K // (2 * KV_WIDTH))),
            cache_spec, cache_spec,
            zb_spec(0), zb_spec(1), zb_spec(2), zb_spec(3),
        ],
        out_specs=[
            pl.BlockSpec((bs, t_new, ATTN_WIDTH), lambda i: (i, 0, 0)),
            cache_spec, cache_spec,
        ],
        out_shape=[
            jax.ShapeDtypeStruct((nseq, t_new, ATTN_WIDTH), _F32),
            jax.ShapeDtypeStruct((nseq, WINDOW, KV_WIDTH), _F32),
            jax.ShapeDtypeStruct((nseq, WINDOW, KV_WIDTH), _F32),
        ],
        compiler_params=_params(("arbitrary",), 48),
        name="attn_sample",
    )(slopes, sinks, proj3, proj3, cache_k, cache_v, proj3, proj3, proj3, proj3)


def _merge_kernel(a_ref, b_ref, wp_ref, wa_ref, ga_ref, gb_ref, h_ref):
    ya = jnp.dot(a_ref[...].astype(_BF16), wp_ref[...], preferred_element_type=_F32)
    yb = jnp.dot(b_ref[...].astype(_BF16), wa_ref[...], preferred_element_type=_F32)
    h = jax.nn.sigmoid(ga_ref[...]) * ya + jax.nn.sigmoid(gb_ref[...]) * yb
    h_ref[...] = h.astype(_BF16)


def _merge(a, b, proj, w_out_pool_bf16, w_out_attn_bf16, tm, tn=512):
    rows = a.shape[0]
    return pl.pallas_call(
        _merge_kernel,
        grid=(rows // tm, D_MODEL // tn),
        in_specs=[
            pl.BlockSpec((tm, POOL_WIDTH), lambda i, j: (i, 0)),
            pl.BlockSpec((tm, ATTN_WIDTH), lambda i, j: (i, 0)),
            pl.BlockSpec((POOL_WIDTH, tn), lambda i, j: (0, j)),
            pl.BlockSpec((ATTN_WIDTH, tn), lambda i, j: (0, j)),
            pl.BlockSpec((tm, tn), lambda i, j: (i, COL_GA // tn + j)),
            pl.BlockSpec((tm, tn), lambda i, j: (i, COL_GB // tn + j)),
        ],
        out_specs=pl.BlockSpec((tm, tn), lambda i, j: (i, j)),
        out_shape=jax.ShapeDtypeStruct((rows, D_MODEL), _BF16),
        compiler_params=_params(("arbitrary", "arbitrary"), 48),
        name="merge",
    )(a, b, w_out_pool_bf16, w_out_attn_bf16, proj, proj)


def _out_kernel(h_ref, w_ref, x_ref, gain_ref, bias_ref, y_ref):
    y = DEEPNORM_ALPHA * x_ref[...] + jnp.dot(h_ref[...], w_ref[...], preferred_element_type=_F32)
    mu = jnp.mean(y, axis=-1, keepdims=True)
    d = y - mu
    var = jnp.mean(d * d, axis=-1, keepdims=True)
    y_ref[...] = d * lax.rsqrt(var + LN_EPS) * gain_ref[...] + bias_ref[...]


def _out(h, w_out_bf16, x, gain, bias, tm=256):
    rows = h.shape[0]
    return pl.pallas_call(
        _out_kernel,
        grid=(rows // tm,),
        in_specs=[
            pl.BlockSpec((tm, D_MODEL), lambda i: (i, 0)),
            pl.BlockSpec((D_MODEL, D_MODEL), lambda i: (0, 0), pipeline_mode=pl.Buffered(1)),
            pl.BlockSpec((tm, D_MODEL), lambda i: (i, 0)),
            pl.BlockSpec((1, D_MODEL), lambda i: (0, 0)),
            pl.BlockSpec((1, D_MODEL), lambda i: (0, 0)),
        ],
        out_specs=pl.BlockSpec((tm, D_MODEL), lambda i: (i, 0)),
        out_shape=jax.ShapeDtypeStruct((rows, D_MODEL), _F32),
        compiler_params=_params(("arbitrary",), 60),
        name="out",
    )(h, w_out_bf16, x, gain, bias)


def _alibi_slopes():
    h = jnp.arange(N_HEADS, dtype=_F32)
    return 2.0 ** (-8.0 * (h + 1.0) / N_HEADS)


def kernel(x_prompt, x_sample, cache_k, cache_v, state_pool, w_in, w_pool, pool_scale, sinks,
           w_out_pool, w_out_attn, w_out, ln_gain, ln_bias):
    depth = w_in.shape[0]
    assert depth == 1
    bp, tp, _ = x_prompt.shape
    bs, ts, _ = x_sample.shape
    slopes = _alibi_slopes()

    l = 0
    w_in_b = w_in[l].astype(_BF16)
    w_pool_b = w_pool[l].astype(_BF16)
    w_op_b = w_out_pool[l].astype(_BF16)
    w_oa_b = w_out_attn[l].astype(_BF16)
    w_o_b = w_out[l].astype(_BF16)
    scale = pool_scale[l].reshape(1, POOL_WIDTH)
    gain = ln_gain[l].reshape(1, D_MODEL)
    bias = ln_bias[l].reshape(1, D_MODEL)
    sink = sinks[l]

    xp2 = x_prompt.reshape(bp * tp, D_MODEL)
    proj_p = _proj(xp2.astype(_BF16), w_in_b)
    a_p = _pool_prompt(proj_p, w_pool_b, scale, tp)
    b_p = _attn_prompt(proj_p, slopes, sink, bp, tp)
    h_p = _merge(a_p, b_p, proj_p, w_op_b, w_oa_b, tm=1024)
    y_p = _out(h_p, w_o_b, xp2, gain, bias).reshape(bp, tp, D_MODEL)
    proj_p3 = proj_p.reshape(bp, tp, N_IN)
    new_k_p = proj_p3[:, tp - WINDOW:, COL_K:COL_K + KV_WIDTH].reshape(1, bp, WINDOW, N_KV_HEADS, HEAD_DIM)
    new_v_p = proj_p3[:, tp - WINDOW:, COL_V:COL_V + KV_WIDTH].reshape(1, bp, WINDOW, N_KV_HEADS, HEAD_DIM)
    new_pool_p = proj_p3[:, tp - POOL_STATE:, COL_U:COL_U + POOL_WIDTH][None]

    xs2 = x_sample.reshape(bs * ts, D_MODEL)
    proj_s = _proj(xs2.astype(_BF16), w_in_b)
    proj_s3 = proj_s.reshape(bs, ts, N_IN)
    state16 = jnp.pad(state_pool[l], ((0, 0), (HALO_ROWS - POOL_STATE, 0), (0, 0)))
    a_s, new_pool_s = _pool_sample(proj_s3, state16, w_pool_b, scale)
    ck = cache_k[l].reshape(bs, WINDOW, KV_WIDTH)
    cv = cache_v[l].reshape(bs, WINDOW, KV_WIDTH)
    b_s3, new_k_s, new_v_s = _attn_sample(proj_s3, ck, cv, slopes, sink)
    b_s = b_s3.reshape(bs * ts, ATTN_WIDTH)
    h_s = _merge(a_s, b_s, proj_s, w_op_b, w_oa_b, tm=1024)
    y_s = _out(h_s, w_o_b, xs2, gain, bias).reshape(bs, ts, D_MODEL)
    new_k_s = new_k_s.reshape(1, bs, WINDOW, N_KV_HEADS, HEAD_DIM)
    new_v_s = new_v_s.reshape(1, bs, WINDOW, N_KV_HEADS, HEAD_DIM)

    return (y_p, y_s, new_k_p, new_v_p, new_pool_p, new_k_s, new_v_s, new_pool_s[None])
```

```python
import functools

import jax
import jax.numpy as jnp
from jax import lax
from jax.experimental import pallas as pl
from jax.experimental.pallas import tpu as pltpu

D_MODEL = 4096
POOL_WIDTH = 2048
POOL_WINDOWS = (2, 4, 8, 16)
POOL_GROUP_WIDTH = 512
POOL_STATE = 15
HEAD_DIM = 64
N_HEADS = 32
N_KV_HEADS = 4
GQA_GROUP = 8
ATTN_WIDTH = 2048
KV_WIDTH = 256
WINDOW = 128
PAST_LEN = 8192
N_IN = 16896
COL_U, COL_ZA, COL_Q, COL_K, COL_V, COL_ZB, COL_GA, COL_GB = 0, 2048, 4096, 6144, 6400, 6656, 8704, 12800
DEEPNORM_ALPHA = 2.0 ** 0.25
LN_EPS = 1e-5

V7X_VMEM_BYTES = 64 * 1024 * 1024
SUBLANES = 8
HALO_ROWS = 16

_F32 = jnp.float32
_BF16 = jnp.bfloat16


def _params(sem, vmem_mib):
    return pltpu.CompilerParams(dimension_semantics=sem, vmem_limit_bytes=vmem_mib * 1024 * 1024)


def _proj_kernel(x_ref, w_ref, o_ref):
    o_ref[...] = jnp.dot(x_ref[...], w_ref[...], preferred_element_type=_F32)


def _proj(x_bf16, w_bf16, tm=1024, tn=768):
    rows, k = x_bf16.shape
    n = w_bf16.shape[1]
    return pl.pallas_call(
        _proj_kernel,
        grid=(rows // tm, n // tn),
        in_specs=[pl.BlockSpec((tm, k), lambda i, j: (i, 0)),
                  pl.BlockSpec((k, tn), lambda i, j: (0, j))],
        out_specs=pl.BlockSpec((tm, tn), lambda i, j: (i, j)),
        out_shape=jax.ShapeDtypeStruct((rows, n), _F32),
        compiler_params=_params(("arbitrary", "arbitrary"), 48),
        name="proj",
    )(x_bf16, w_bf16)


def _window_sums(ext, n_doublings):
    s = ext
    for d in range(n_doublings):
        s = s + pltpu.roll(s, 1 << d, axis=0)
    return s


def _pool_groups(ext, u, z_a, pos, wp_ref, scale_ref, halo):
    outs = []
    for g, w in enumerate(POOL_WINDOWS):
        sl = slice(g * POOL_GROUP_WIDTH, (g + 1) * POOL_GROUP_WIDTH)
        s = _window_sums(ext[:, sl], g + 1)[halo:]
        count = jnp.minimum(w, pos + 1).astype(_F32)
        pooled = s / count - u[:, sl]
        mixed = jnp.dot(pooled.astype(_BF16), wp_ref[g], preferred_element_type=_F32)
        pool_out = mixed * scale_ref[:, sl]
        z = z_a[:, sl]
        outs.append(z * jax.nn.sigmoid(z) * pool_out)
    return outs


def _pool_prompt_kernel(u_ref, halo_ref, za_ref, wp_ref, scale_ref, a_ref, *, tiles_per_seq, tm):
    t = pl.program_id(0) % tiles_per_seq
    u = u_ref[...]
    halo = jnp.where(t == 0, 0.0, halo_ref[...])
    ext = jnp.concatenate([halo, u], axis=0)
    pos = t * tm + lax.broadcasted_iota(jnp.int32, (tm, 1), 0)
    outs = _pool_groups(ext, u, za_ref[...], pos, wp_ref, scale_ref, HALO_ROWS)
    for g, o in enumerate(outs):
        a_ref[:, g * POOL_GROUP_WIDTH:(g + 1) * POOL_GROUP_WIDTH] = o.astype(_BF16)


def _pool_prompt(proj, w_pool_bf16, pool_scale, seq, tm=256):
    rows = proj.shape[0]
    tiles_per_seq = seq // tm
    halo_blocks = tm // HALO_ROWS
    kern = functools.partial(_pool_prompt_kernel, tiles_per_seq=tiles_per_seq, tm=tm)
    return pl.pallas_call(
        kern,
        grid=(rows // tm,),
        in_specs=[
            pl.BlockSpec((tm, POOL_WIDTH), lambda i: (i, COL_U // POOL_WIDTH)),
            pl.BlockSpec((HALO_ROWS, POOL_WIDTH), lambda i: (jnp.maximum(i * halo_blocks - 1, 0), 0)),
            pl.BlockSpec((tm, POOL_WIDTH), lambda i: (i, COL_ZA // POOL_WIDTH)),
            pl.BlockSpec((4, POOL_GROUP_WIDTH, POOL_GROUP_WIDTH), lambda i: (0, 0, 0)),
            pl.BlockSpec((1, POOL_WIDTH), lambda i: (0, 0)),
        ],
        out_specs=pl.BlockSpec((tm, POOL_WIDTH), lambda i: (i, 0)),
        out_shape=jax.ShapeDtypeStruct((rows, POOL_WIDTH), _BF16),
        compiler_params=_params(("arbitrary",), 48),
        name="pool_prompt",
    )(proj, proj, proj, w_pool_bf16, pool_scale)


def _pool_sample_kernel(u_ref, za_ref, st_ref, wp_ref, scale_ref, a_ref, np_ref, *, bs, t_new):
    u3 = u_ref[...]
    st = st_ref[...]
    rows_per_seq = HALO_ROWS + t_new
    ext = jnp.concatenate([st, u3], axis=1).reshape(bs * rows_per_seq, POOL_WIDTH)
    pos3 = PAST_LEN + lax.broadcasted_iota(jnp.int32, (bs, t_new, 1), 1)
    pos = pos3.reshape(bs * t_new, 1)
    u = u3.reshape(bs * t_new, POOL_WIDTH)
    za = za_ref[...].reshape(bs * t_new, POOL_WIDTH)
    for g, w in enumerate(POOL_WINDOWS):
        sl = slice(g * POOL_GROUP_WIDTH, (g + 1) * POOL_GROUP_WIDTH)
        s = _window_sums(ext[:, sl], g + 1)
        s = s.reshape(bs, rows_per_seq, POOL_GROUP_WIDTH)[:, HALO_ROWS:, :].reshape(bs * t_new, POOL_GROUP_WIDTH)
        count = jnp.minimum(w, pos + 1).astype(_F32)
        pooled = s / count - u[:, sl]
        mixed = jnp.dot(pooled.astype(_BF16), wp_ref[g], preferred_element_type=_F32)
        pool_out = mixed * scale_ref[:, sl]
        z = za[:, sl]
        a_ref[:, sl] = (z * jax.nn.sigmoid(z) * pool_out).astype(_BF16)
    keep = POOL_STATE - t_new
    np_ref[:, 0:keep, :] = st[:, HALO_ROWS - keep:, :]
    np_ref[:, keep:, :] = u3


def _pool_sample(proj3, state16, w_pool_bf16, pool_scale, bs=16):
    nseq, t_new, _ = proj3.shape
    kern = functools.partial(_pool_sample_kernel, bs=bs, t_new=t_new)
    return pl.pallas_call(
        kern,
        grid=(nseq // bs,),
        in_specs=[
            pl.BlockSpec((bs, t_new, POOL_WIDTH), lambda i: (i, 0, COL_U // POOL_WIDTH)),
            pl.BlockSpec((bs, t_new, POOL_WIDTH), lambda i: (i, 0, COL_ZA // POOL_WIDTH)),
            pl.BlockSpec((bs, HALO_ROWS, POOL_WIDTH), lambda i: (i, 0, 0)),
            pl.BlockSpec((4, POOL_GROUP_WIDTH, POOL_GROUP_WIDTH), lambda i: (0, 0, 0)),
            pl.BlockSpec((1, POOL_WIDTH), lambda i: (0, 0)),
        ],
        out_specs=[
            pl.BlockSpec((bs * t_new, POOL_WIDTH), lambda i: (i, 0)),
            pl.BlockSpec((bs, POOL_STATE, POOL_WIDTH), lambda i: (i, 0, 0)),
        ],
        out_shape=[
            jax.ShapeDtypeStruct((nseq * t_new, POOL_WIDTH), _BF16),
            jax.ShapeDtypeStruct((nseq, POOL_STATE, POOL_WIDTH), _F32),
        ],
        compiler_params=_params(("arbitrary",), 48),
        name="pool_sample",
    )(proj3, proj3, state16, w_pool_bf16, pool_scale)


def _softmax_with_sink(l, sink):
    m = jnp.maximum(jnp.max(l, axis=-1, keepdims=True), sink)
    p = jnp.exp(l - m)
    den = jnp.sum(p, axis=-1, keepdims=True) + jnp.exp(sink - m)
    return p * (1.0 / den)


def _penalty(slope, dist, valid):
    return jnp.where(valid, slope * dist.astype(_F32), jnp.inf)


def _attn_prompt_kernel(slopes_ref, sinks_ref, q_ref, kvc_ref, kvp_ref, zb0, zb1, zb2, zb3, o_ref, pen_ref):
    blk = WINDOW
    gw = GQA_GROUP * HEAD_DIM
    n = pl.program_id(1)

    @pl.when((pl.program_id(0) == 0) & (n == 0))
    def _fill_penalty_tables():
        ti = lax.broadcasted_iota(jnp.int32, (blk, 2 * blk), 0)
        si = lax.broadcasted_iota(jnp.int32, (blk, 2 * blk), 1)
        dist = blk + ti - si
        for n_tbl in (0, 1):
            key_pos = (n_tbl - 1) * blk + si
            valid = (dist >= 0) & (dist < WINDOW) & (key_pos >= 0)
            for h in range(N_HEADS):
                pen_ref[n_tbl, h * blk:(h + 1) * blk, :] = _penalty(slopes_ref[h], dist, valid)

    tbl = jnp.minimum(n, 1)
    kv = jnp.concatenate([kvp_ref[...], kvc_ref[...]], axis=0).astype(_BF16)
    zbs = (zb0, zb1, zb2, zb3)
    for kh in range(N_KV_HEADS):
        k_kh = kv[:, kh * HEAD_DIM:(kh + 1) * HEAD_DIM]
        v_kh = kv[:, KV_WIDTH + kh * HEAD_DIM:KV_WIDTH + (kh + 1) * HEAD_DIM]
        q_kh = (q_ref[:, kh * gw:(kh + 1) * gw] * (HEAD_DIM ** -0.5)).astype(_BF16)
        qs = jnp.concatenate([q_kh[:, g * HEAD_DIM:(g + 1) * HEAD_DIM] for g in range(GQA_GROUP)], axis=0)
        logits = lax.dot_general(qs, k_kh, (((1,), (1,)), ((), ())), preferred_element_type=_F32)
        probs = []
        for g in range(GQA_GROUP):
            h = kh * GQA_GROUP + g
            l = logits[g * blk:(g + 1) * blk] - pen_ref[tbl, pl.ds(h * blk, blk), :]
            probs.append(_softmax_with_sink(l, sinks_ref[h]).astype(_BF16))
        pr = jnp.concatenate(probs, axis=0)
        o = jnp.dot(pr, v_kh, preferred_element_type=_F32)
        o_kh = jnp.concatenate([o[g * blk:(g + 1) * blk] for g in range(GQA_GROUP)], axis=1)
        zb = zbs[kh][...]
        o_ref[:, kh * gw:(kh + 1) * gw] = (zb * jax.nn.sigmoid(zb) * o_kh).astype(_BF16)


def _attn_prompt(proj, slopes, sinks, batch, seq):
    blk = WINDOW
    nb = seq // blk
    gw = GQA_GROUP * HEAD_DIM
    smem = pl.BlockSpec(memory_space=pltpu.SMEM)

    def zb_spec(kh):
        return pl.BlockSpec((blk, gw), lambda b, n: (b * nb + n, COL_ZB // gw + kh))

    return pl.pallas_call(
        _attn_prompt_kernel,
        grid=(batch, nb),
        in_specs=[
            smem, smem,
            pl.BlockSpec((blk, ATTN_WIDTH), lambda b, n: (b * nb + n, COL_Q // ATTN_WIDTH)),
            pl.BlockSpec((blk, 2 * KV_WIDTH), lambda b, n: (b * nb + n, COL_K // (2 * KV_WIDTH))),
            pl.BlockSpec((blk, 2 * KV_WIDTH), lambda b, n: (b * nb + jnp.maximum(n - 1, 0), COL_K // (2 * KV_WIDTH))),
            zb_spec(0), zb_spec(1), zb_spec(2), zb_spec(3),
        ],
        out_specs=pl.BlockSpec((blk, ATTN_WIDTH), lambda b, n: (b * nb + n, 0)),
        out_shape=jax.ShapeDtypeStruct((batch * seq, ATTN_WIDTH), _BF16),
        scratch_shapes=[pltpu.VMEM((2, N_HEADS * blk, 2 * blk), _F32)],
        compiler_params=_params(("arbitrary", "arbitrary"), 48),
        name="attn_prompt",
    )(slopes, sinks, proj, proj, proj, proj, proj, proj, proj)


def _attn_sample_kernel(slopes_ref, sinks_ref, q_ref, kvn_ref, ck_ref, cv_ref, zb0, zb1, zb2, zb3,
                        o_ref, nk_ref, nv_ref, pen_ref, *, bs, t_new):
    s_all = WINDOW + t_new
    gw = GQA_GROUP * HEAD_DIM

    @pl.when(pl.program_id(0) == 0)
    def _fill_penalty_table():
        ti = lax.broadcasted_iota(jnp.int32, (t_new, s_all), 0)
        si = lax.broadcasted_iota(jnp.int32, (t_new, s_all), 1)
        dist = WINDOW + ti - si
        key_pos = PAST_LEN - WINDOW + si
        valid = (dist >= 0) & (dist < WINDOW) & (key_pos >= 0)
        for h in range(N_HEADS):
            pen_ref[h] = _penalty(slopes_ref[h], dist, valid)

    kvn = kvn_ref[...]
    k_all = jnp.concatenate([ck_ref[...], kvn[:, :, 0:KV_WIDTH]], axis=1)
    v_all = jnp.concatenate([cv_ref[...], kvn[:, :, KV_WIDTH:2 * KV_WIDTH]], axis=1)
    nk_ref[...] = k_all[:, t_new:, :]
    nv_ref[...] = v_all[:, t_new:, :]
    kb = k_all.astype(_BF16)
    vb = v_all.astype(_BF16)
    zbs = (zb0, zb1, zb2, zb3)
    for kh in range(N_KV_HEADS):
        q_kh = (q_ref[:, :, kh * gw:(kh + 1) * gw] * (HEAD_DIM ** -0.5)).astype(_BF16)
        qs = jnp.concatenate([q_kh[:, :, g * HEAD_DIM:(g + 1) * HEAD_DIM] for g in range(GQA_GROUP)], axis=1)
        hd = slice(kh * HEAD_DIM, (kh + 1) * HEAD_DIM)
        logits = jnp.stack(
            [lax.dot_general(qs[b], kb[b, :, hd], (((1,), (1,)), ((), ())), preferred_element_type=_F32)
             for b in range(bs)], axis=0)
        probs = []
        for g in range(GQA_GROUP):
            h = kh * GQA_GROUP + g
            l = logits[:, g * t_new:(g + 1) * t_new, :] - pen_ref[h]
            probs.append(_softmax_with_sink(l, sinks_ref[h]).astype(_BF16))
        pr = jnp.concatenate(probs, axis=1)
        o = jnp.stack([jnp.dot(pr[b], vb[b, :, hd], preferred_element_type=_F32) for b in range(bs)], axis=0)
        o_kh = jnp.concatenate([o[:, g * t_new:(g + 1) * t_new, :] for g in range(GQA_GROUP)], axis=2)
        zb = zbs[kh][...]
        res = (zb * jax.nn.sigmoid(zb) * o_kh).reshape(bs * t_new, gw)
        o_ref[:, kh * gw:(kh + 1) * gw] = res.astype(_BF16)


def _attn_sample(proj3, cache_k, cache_v, slopes, sinks, bs=16):
    nseq, t_new, _ = proj3.shape
    gw = GQA_GROUP * HEAD_DIM
    smem = pl.BlockSpec(memory_space=pltpu.SMEM)
    kern = functools.partial(_attn_sample_kernel, bs=bs, t_new=t_new)

    def zb_spec(kh):
        return pl.BlockSpec((bs, t_new, gw), lambda i: (i, 0, COL_ZB // gw + kh))

    cache_spec = pl.BlockSpec((bs, WINDOW, KV_WIDTH), lambda i: (i, 0, 0))
    return pl.pallas_call(
        kern,
        grid=(nseq // bs,),
        in_specs=[
            smem, smem,
            pl.BlockSpec((bs, t_new, ATTN_WIDTH), lambda i: (i, 0, COL_Q // ATTN_WIDTH)),
            pl.BlockSpec((bs, t_new, 2 * KV_WIDTH), lambda i: (i, 0, COL_K // (2 * KV_WIDTH))),
            cache_spec, cache_spec,
            zb_spec(0), zb_spec(1), zb_spec(2), zb_spec(3),
        ],
        out_specs=[
            pl.BlockSpec((bs * t_new, ATTN_WIDTH), lambda i: (i, 0)),
            cache_spec, cache_spec,
        ],
        out_shape=[
            jax.ShapeDtypeStruct((nseq * t_new, ATTN_WIDTH), _BF16),
            jax.ShapeDtypeStruct((nseq, WINDOW, KV_WIDTH), _F32),
            jax.ShapeDtypeStruct((nseq, WINDOW, KV_WIDTH), _F32),
        ],
        scratch_shapes=[pltpu.VMEM((N_HEADS, t_new, WINDOW + t_new), _F32)],
        compiler_params=_params(("arbitrary",), 48),
        name="attn_sample",
    )(slopes, sinks, proj3, proj3, cache_k, cache_v, proj3, proj3, proj3, proj3)


def _merge_kernel(a_ref, b_ref, wp_ref, wa_ref, ga_ref, gb_ref, h_ref, *, chunk):
    for r in range(0, a_ref.shape[0], chunk):
        rows = pl.ds(r, chunk)
        ya = jnp.dot(a_ref[rows, :], wp_ref[...], preferred_element_type=_F32)
        yb = jnp.dot(b_ref[rows, :], wa_ref[...], preferred_element_type=_F32)
        h = jax.nn.sigmoid(ga_ref[rows, :]) * ya + jax.nn.sigmoid(gb_ref[rows, :]) * yb
        h_ref[rows, :] = h.astype(_BF16)


def _merge(a, b, proj, w_out_pool_bf16, w_out_attn_bf16, tm, tn=512, chunk=256):
    rows = a.shape[0]
    return pl.pallas_call(
        functools.partial(_merge_kernel, chunk=chunk),
        grid=(rows // tm, D_MODEL // tn),
        in_specs=[
            pl.BlockSpec((tm, POOL_WIDTH), lambda i, j: (i, 0)),
            pl.BlockSpec((tm, ATTN_WIDTH), lambda i, j: (i, 0)),
            pl.BlockSpec((POOL_WIDTH, tn), lambda i, j: (0, j)),
            pl.BlockSpec((ATTN_WIDTH, tn), lambda i, j: (0, j)),
            pl.BlockSpec((tm, tn), lambda i, j: (i, COL_GA // tn + j)),
            pl.BlockSpec((tm, tn), lambda i, j: (i, COL_GB // tn + j)),
        ],
        out_specs=pl.BlockSpec((tm, tn), lambda i, j: (i, j)),
        out_shape=jax.ShapeDtypeStruct((rows, D_MODEL), _BF16),
        compiler_params=_params(("arbitrary", "arbitrary"), 48),
        name="merge",
    )(a, b, w_out_pool_bf16, w_out_attn_bf16, proj, proj)


def _out_kernel(h_ref, w_ref, x_ref, gain_ref, bias_ref, y_ref):
    y = DEEPNORM_ALPHA * x_ref[...] + jnp.dot(h_ref[...], w_ref[...], preferred_element_type=_F32)
    mu = jnp.mean(y, axis=-1, keepdims=True)
    d = y - mu
    var = jnp.mean(d * d, axis=-1, keepdims=True)
    y_ref[...] = d * lax.rsqrt(var + LN_EPS) * gain_ref[...] + bias_ref[...]


def _out(h, w_out_bf16, x, gain, bias, tm=256):
    rows = h.shape[0]
    return pl.pallas_call(
        _out_kernel,
        grid=(rows // tm,),
        in_specs=[
            pl.BlockSpec((tm, D_MODEL), lambda i: (i, 0)),
            pl.BlockSpec((D_MODEL, D_MODEL), lambda i: (0, 0), pipeline_mode=pl.Buffered(1)),
            pl.BlockSpec((tm, D_MODEL), lambda i: (i, 0)),
            pl.BlockSpec((1, D_MODEL), lambda i: (0, 0)),
            pl.BlockSpec((1, D_MODEL), lambda i: (0, 0)),
        ],
        out_specs=pl.BlockSpec((tm, D_MODEL), lambda i: (i, 0)),
        out_shape=jax.ShapeDtypeStruct((rows, D_MODEL), _F32),
        compiler_params=_params(("arbitrary",), 60),
        name="out",
    )(h, w_out_bf16, x, gain, bias)


def _alibi_slopes():
    h = jnp.arange(N_HEADS, dtype=_F32)
    return 2.0 ** (-8.0 * (h + 1.0) / N_HEADS)


def kernel(x_prompt, x_sample, cache_k, cache_v, state_pool, w_in, w_pool, pool_scale, sinks,
           w_out_pool, w_out_attn, w_out, ln_gain, ln_bias):
    depth = w_in.shape[0]
    assert depth == 1
    bp, tp, _ = x_prompt.shape
    bs, ts, _ = x_sample.shape
    slopes = _alibi_slopes()

    l = 0
    w_in_b = w_in[l].astype(_BF16)
    w_pool_b = w_pool[l].astype(_BF16)
    w_op_b = w_out_pool[l].astype(_BF16)
    w_oa_b = w_out_attn[l].astype(_BF16)
    w_o_b = w_out[l].astype(_BF16)
    scale = pool_scale[l].reshape(1, POOL_WIDTH)
    gain = ln_gain[l].reshape(1, D_MODEL)
    bias = ln_bias[l].reshape(1, D_MODEL)
    sink = sinks[l]

    xp2 = x_prompt.reshape(bp * tp, D_MODEL)
    proj_p = _proj(xp2.astype(_BF16), w_in_b)
    a_p = _pool_prompt(proj_p, w_pool_b, scale, tp)
    b_p = _attn_prompt(proj_p, slopes, sink, bp, tp)
    h_p = _merge(a_p, b_p, proj_p, w_op_b, w_oa_b, tm=1024)
    y_p = _out(h_p, w_o_b, xp2, gain, bias).reshape(bp, tp, D_MODEL)
    proj_p3 = proj_p.reshape(bp, tp, N_IN)
    new_k_p = proj_p3[:, tp - WINDOW:, COL_K:COL_K + KV_WIDTH].reshape(1, bp, WINDOW, N_KV_HEADS, HEAD_DIM)
    new_v_p = proj_p3[:, tp - WINDOW:, COL_V:COL_V + KV_WIDTH].reshape(1, bp, WINDOW, N_KV_HEADS, HEAD_DIM)
    new_pool_p = proj_p3[:, tp - POOL_STATE:, COL_U:COL_U + POOL_WIDTH][None]

    xs2 = x_sample.reshape(bs * ts, D_MODEL)
    proj_s = _proj(xs2.astype(_BF16), w_in_b)
    proj_s3 = proj_s.reshape(bs, ts, N_IN)
    state16 = jnp.pad(state_pool[l], ((0, 0), (HALO_ROWS - POOL_STATE, 0), (0, 0)))
    a_s, new_pool_s = _pool_sample(proj_s3, state16, w_pool_b, scale)
    ck = cache_k[l].reshape(bs, WINDOW, KV_WIDTH)
    cv = cache_v[l].reshape(bs, WINDOW, KV_WIDTH)
    b_s, new_k_s, new_v_s = _attn_sample(proj_s3, ck, cv, slopes, sink)
    h_s = _merge(a_s, b_s, proj_s, w_op_b, w_oa_b, tm=1024)
    y_s = _out(h_s, w_o_b, xs2, gain, bias).reshape(bs, ts, D_MODEL)
    new_k_s = new_k_s.reshape(1, bs, WINDOW, N_KV_HEADS, HEAD_DIM)
    new_v_s = new_v_s.reshape(1, bs, WINDOW, N_KV_HEADS, HEAD_DIM)

    return (y_p, y_s, new_k_p, new_v_p, new_pool_p, new_k_s, new_v_s, new_pool_s[None])
```

```python
import functools

import jax
import jax.numpy as jnp
from jax import lax
from jax.experimental import pallas as pl
from jax.experimental.pallas import tpu as pltpu

D_MODEL = 4096
POOL_WIDTH = 2048
POOL_WINDOWS = (2, 4, 8, 16)
POOL_GROUP_WIDTH = 512
POOL_STATE = 15
HEAD_DIM = 64
N_HEADS = 32
N_KV_HEADS = 4
GQA_GROUP = 8
ATTN_WIDTH = 2048
KV_WIDTH = 256
WINDOW = 128
PAST_LEN = 8192
N_IN = 16896
COL_U, COL_ZA, COL_Q, COL_K, COL_V, COL_ZB, COL_GA, COL_GB = 0, 2048, 4096, 6144, 6400, 6656, 8704, 12800
DEEPNORM_ALPHA = 2.0 ** 0.25
LN_EPS = 1e-5

V7X_VMEM_BYTES = 64 * 1024 * 1024
SUBLANES = 8
HALO_ROWS = 16

_F32 = jnp.float32
_BF16 = jnp.bfloat16


def _params(sem, vmem_mib):
    return pltpu.CompilerParams(dimension_semantics=sem, vmem_limit_bytes=vmem_mib * 1024 * 1024)


def _proj_kernel(x_ref, w_ref, o_ref, wb_ref):
    @pl.when(pl.program_id(1) == 0)
    def _cast_weight_tile():
        wb_ref[...] = w_ref[...].astype(_BF16)

    o_ref[...] = jnp.dot(x_ref[...], wb_ref[...], preferred_element_type=_F32)


def _proj(x_bf16, w_f32, tm=1024, tn=768):
    rows, k = x_bf16.shape
    n = w_f32.shape[1]
    return pl.pallas_call(
        _proj_kernel,
        grid=(n // tn, rows // tm),
        in_specs=[pl.BlockSpec((tm, k), lambda j, i: (i, 0)),
                  pl.BlockSpec((k, tn), lambda j, i: (0, j))],
        out_specs=pl.BlockSpec((tm, tn), lambda j, i: (i, j)),
        out_shape=jax.ShapeDtypeStruct((rows, n), _F32),
        scratch_shapes=[pltpu.VMEM((k, tn), _BF16)],
        compiler_params=_params(("arbitrary", "arbitrary"), 60),
        name="proj",
    )(x_bf16, w_f32)


def _window_sums(ext, n_doublings):
    s = ext
    for d in range(n_doublings):
        s = s + pltpu.roll(s, 1 << d, axis=0)
    return s


def _pool_groups(ext, u, z_a, pos, wp_ref, scale_ref, halo):
    outs = []
    for g, w in enumerate(POOL_WINDOWS):
        sl = slice(g * POOL_GROUP_WIDTH, (g + 1) * POOL_GROUP_WIDTH)
        s = _window_sums(ext[:, sl], g + 1)[halo:]
        count = jnp.minimum(w, pos + 1).astype(_F32)
        pooled = s / count - u[:, sl]
        mixed = jnp.dot(pooled.astype(_BF16), wp_ref[g], preferred_element_type=_F32)
        pool_out = mixed * scale_ref[:, sl]
        z = z_a[:, sl]
        outs.append(z * jax.nn.sigmoid(z) * pool_out)
    return outs


def _pool_prompt_kernel(u_ref, halo_ref, za_ref, wp_ref, scale_ref, a_ref, *, tiles_per_seq, tm):
    t = pl.program_id(0) % tiles_per_seq
    u = u_ref[...]
    halo = jnp.where(t == 0, 0.0, halo_ref[...])
    ext = jnp.concatenate([halo, u], axis=0)
    pos = t * tm + lax.broadcasted_iota(jnp.int32, (tm, 1), 0)
    outs = _pool_groups(ext, u, za_ref[...], pos, wp_ref, scale_ref, HALO_ROWS)
    for g, o in enumerate(outs):
        a_ref[:, g * POOL_GROUP_WIDTH:(g + 1) * POOL_GROUP_WIDTH] = o.astype(_BF16)


def _pool_prompt(proj, w_pool_bf16, pool_scale, batch, seq, tm=256):
    rows = proj.shape[0]
    tiles_per_seq = seq // tm
    halo_blocks = tm // HALO_ROWS
    kern = functools.partial(_pool_prompt_kernel, tiles_per_seq=tiles_per_seq, tm=tm)
    return pl.pallas_call(
        kern,
        grid=(batch * seq // tm,),
        in_specs=[
            pl.BlockSpec((tm, POOL_WIDTH), lambda i: (i, COL_U // POOL_WIDTH)),
            pl.BlockSpec((HALO_ROWS, POOL_WIDTH), lambda i: (jnp.maximum(i * halo_blocks - 1, 0), 0)),
            pl.BlockSpec((tm, POOL_WIDTH), lambda i: (i, COL_ZA // POOL_WIDTH)),
            pl.BlockSpec((4, POOL_GROUP_WIDTH, POOL_GROUP_WIDTH), lambda i: (0, 0, 0)),
            pl.BlockSpec((1, POOL_WIDTH), lambda i: (0, 0)),
        ],
        out_specs=pl.BlockSpec((tm, POOL_WIDTH), lambda i: (i, 0)),
        out_shape=jax.ShapeDtypeStruct((rows, POOL_WIDTH), _BF16),
        compiler_params=_params(("arbitrary",), 48),
        name="pool_prompt",
    )(proj, proj, proj, w_pool_bf16, pool_scale)


def _pool_sample_kernel(u_ref, za_ref, st_ref, wp_ref, scale_ref, a_in_ref, a_ref, np_ref, *, bs, t_new):
    del a_in_ref
    u3 = u_ref[...]
    st = st_ref[...]
    rows_per_seq = HALO_ROWS + t_new
    ext = jnp.concatenate([st, u3], axis=1).reshape(bs * rows_per_seq, POOL_WIDTH)
    pos3 = PAST_LEN + lax.broadcasted_iota(jnp.int32, (bs, t_new, 1), 1)
    pos = pos3.reshape(bs * t_new, 1)
    u = u3.reshape(bs * t_new, POOL_WIDTH)
    za = za_ref[...].reshape(bs * t_new, POOL_WIDTH)
    for g, w in enumerate(POOL_WINDOWS):
        sl = slice(g * POOL_GROUP_WIDTH, (g + 1) * POOL_GROUP_WIDTH)
        s = _window_sums(ext[:, sl], g + 1)
        s = s.reshape(bs, rows_per_seq, POOL_GROUP_WIDTH)[:, HALO_ROWS:, :].reshape(bs * t_new, POOL_GROUP_WIDTH)
        count = jnp.minimum(w, pos + 1).astype(_F32)
        pooled = s / count - u[:, sl]
        mixed = jnp.dot(pooled.astype(_BF16), wp_ref[g], preferred_element_type=_F32)
        pool_out = mixed * scale_ref[:, sl]
        z = za[:, sl]
        a_ref[:, sl] = (z * jax.nn.sigmoid(z) * pool_out).astype(_BF16)
    keep = POOL_STATE - t_new
    np_ref[:, 0:keep, :] = st[:, HALO_ROWS - keep:, :]
    np_ref[:, keep:, :] = u3


def _pool_sample(proj3, state16, w_pool_bf16, pool_scale, a_all, seq0, bs=16):
    nseq = state16.shape[0]
    t_new = proj3.shape[1]
    blk0 = seq0 // bs
    kern = functools.partial(_pool_sample_kernel, bs=bs, t_new=t_new)
    return pl.pallas_call(
        kern,
        grid=(nseq // bs,),
        in_specs=[
            pl.BlockSpec((bs, t_new, POOL_WIDTH), lambda i: (blk0 + i, 0, COL_U // POOL_WIDTH)),
            pl.BlockSpec((bs, t_new, POOL_WIDTH), lambda i: (blk0 + i, 0, COL_ZA // POOL_WIDTH)),
            pl.BlockSpec((bs, HALO_ROWS, POOL_WIDTH), lambda i: (i, 0, 0)),
            pl.BlockSpec((4, POOL_GROUP_WIDTH, POOL_GROUP_WIDTH), lambda i: (0, 0, 0)),
            pl.BlockSpec((1, POOL_WIDTH), lambda i: (0, 0)),
            pl.BlockSpec(memory_space=pl.ANY),
        ],
        out_specs=[
            pl.BlockSpec((bs * t_new, POOL_WIDTH), lambda i: (blk0 + i, 0)),
            pl.BlockSpec((bs, POOL_STATE, POOL_WIDTH), lambda i: (i, 0, 0)),
        ],
        out_shape=[
            jax.ShapeDtypeStruct(a_all.shape, _BF16),
            jax.ShapeDtypeStruct((nseq, POOL_STATE, POOL_WIDTH), _F32),
        ],
        input_output_aliases={5: 0},
        compiler_params=_params(("arbitrary",), 48),
        name="pool_sample",
    )(proj3, proj3, state16, w_pool_bf16, pool_scale, a_all)


def _softmax_with_sink(l, sink):
    m = jnp.maximum(jnp.max(l, axis=-1, keepdims=True), sink)
    p = jnp.exp(l - m)
    den = jnp.sum(p, axis=-1, keepdims=True) + jnp.exp(sink - m)
    return p * (1.0 / den)


def _penalty(slope, dist, valid):
    return jnp.where(valid, slope * dist.astype(_F32), jnp.inf)


def _attn_prompt_kernel(slopes_ref, sinks_ref, q_ref, kvc_ref, kvp_ref, zb0, zb1, zb2, zb3, o_ref, pen_ref):
    blk = WINDOW
    gw = GQA_GROUP * HEAD_DIM
    n = pl.program_id(1)

    @pl.when((pl.program_id(0) == 0) & (n == 0))
    def _fill_penalty_tables():
        ti = lax.broadcasted_iota(jnp.int32, (blk, 2 * blk), 0)
        si = lax.broadcasted_iota(jnp.int32, (blk, 2 * blk), 1)
        dist = blk + ti - si
        for n_tbl in (0, 1):
            key_pos = (n_tbl - 1) * blk + si
            valid = (dist >= 0) & (dist < WINDOW) & (key_pos >= 0)
            for h in range(N_HEADS):
                pen_ref[n_tbl, h * blk:(h + 1) * blk, :] = _penalty(slopes_ref[h], dist, valid)

    tbl = jnp.minimum(n, 1)
    kv = jnp.concatenate([kvp_ref[...], kvc_ref[...]], axis=0).astype(_BF16)
    zbs = (zb0, zb1, zb2, zb3)
    for kh in range(N_KV_HEADS):
        k_kh = kv[:, kh * HEAD_DIM:(kh + 1) * HEAD_DIM]
        v_kh = kv[:, KV_WIDTH + kh * HEAD_DIM:KV_WIDTH + (kh + 1) * HEAD_DIM]
        q_kh = (q_ref[:, kh * gw:(kh + 1) * gw] * (HEAD_DIM ** -0.5)).astype(_BF16)
        qs = jnp.concatenate([q_kh[:, g * HEAD_DIM:(g + 1) * HEAD_DIM] for g in range(GQA_GROUP)], axis=0)
        logits = lax.dot_general(qs, k_kh, (((1,), (1,)), ((), ())), preferred_element_type=_F32)
        probs = []
        for g in range(GQA_GROUP):
            h = kh * GQA_GROUP + g
            l = logits[g * blk:(g + 1) * blk] - pen_ref[tbl, pl.ds(h * blk, blk), :]
            probs.append(_softmax_with_sink(l, sinks_ref[h]).astype(_BF16))
        pr = jnp.concatenate(probs, axis=0)
        o = jnp.dot(pr, v_kh, preferred_element_type=_F32)
        o_kh = jnp.concatenate([o[g * blk:(g + 1) * blk] for g in range(GQA_GROUP)], axis=1)
        zb = zbs[kh][...]
        o_ref[:, kh * gw:(kh + 1) * gw] = (zb * jax.nn.sigmoid(zb) * o_kh).astype(_BF16)


def _attn_prompt(proj, slopes, sinks, batch, seq):
    blk = WINDOW
    nb = seq // blk
    gw = GQA_GROUP * HEAD_DIM
    smem = pl.BlockSpec(memory_space=pltpu.SMEM)

    def zb_spec(kh):
        return pl.BlockSpec((blk, gw), lambda b, n: (b * nb + n, COL_ZB // gw + kh))

    return pl.pallas_call(
        _attn_prompt_kernel,
        grid=(batch, nb),
        in_specs=[
            smem, smem,
            pl.BlockSpec((blk, ATTN_WIDTH), lambda b, n: (b * nb + n, COL_Q // ATTN_WIDTH)),
            pl.BlockSpec((blk, 2 * KV_WIDTH), lambda b, n: (b * nb + n, COL_K // (2 * KV_WIDTH))),
            pl.BlockSpec((blk, 2 * KV_WIDTH), lambda b, n: (b * nb + jnp.maximum(n - 1, 0), COL_K // (2 * KV_WIDTH))),
            zb_spec(0), zb_spec(1), zb_spec(2), zb_spec(3),
        ],
        out_specs=pl.BlockSpec((blk, ATTN_WIDTH), lambda b, n: (b * nb + n, 0)),
        out_shape=jax.ShapeDtypeStruct((proj.shape[0], ATTN_WIDTH), _BF16),
        scratch_shapes=[pltpu.VMEM((2, N_HEADS * blk, 2 * blk), _F32)],
        compiler_params=_params(("arbitrary", "arbitrary"), 48),
        name="attn_prompt",
    )(slopes, sinks, proj, proj, proj, proj, proj, proj, proj)


def _attn_sample_kernel(slopes_ref, sinks_ref, q_ref, kvn_ref, ck_ref, cv_ref, zb0, zb1, zb2, zb3, b_in_ref,
                        o_ref, nk_ref, nv_ref, pen_ref, *, bs, t_new):
    del b_in_ref
    s_all = WINDOW + t_new
    gw = GQA_GROUP * HEAD_DIM

    @pl.when(pl.program_id(0) == 0)
    def _fill_penalty_table():
        ti = lax.broadcasted_iota(jnp.int32, (t_new, s_all), 0)
        si = lax.broadcasted_iota(jnp.int32, (t_new, s_all), 1)
        dist = WINDOW + ti - si
        key_pos = PAST_LEN - WINDOW + si
        valid = (dist >= 0) & (dist < WINDOW) & (key_pos >= 0)
        for h in range(N_HEADS):
            pen_ref[h] = _penalty(slopes_ref[h], dist, valid)

    kvn = kvn_ref[...]
    k_all = jnp.concatenate([ck_ref[...], kvn[:, :, 0:KV_WIDTH]], axis=1)
    v_all = jnp.concatenate([cv_ref[...], kvn[:, :, KV_WIDTH:2 * KV_WIDTH]], axis=1)
    nk_ref[...] = k_all[:, t_new:, :]
    nv_ref[...] = v_all[:, t_new:, :]
    kb = k_all.astype(_BF16)
    vb = v_all.astype(_BF16)
    zbs = (zb0, zb1, zb2, zb3)
    for kh in range(N_KV_HEADS):
        q_kh = (q_ref[:, :, kh * gw:(kh + 1) * gw] * (HEAD_DIM ** -0.5)).astype(_BF16)
        qs = jnp.concatenate([q_kh[:, :, g * HEAD_DIM:(g + 1) * HEAD_DIM] for g in range(GQA_GROUP)], axis=1)
        hd = slice(kh * HEAD_DIM, (kh + 1) * HEAD_DIM)
        logits = jnp.stack(
            [lax.dot_general(qs[b], kb[b, :, hd], (((1,), (1,)), ((), ())), preferred_element_type=_F32)
             for b in range(bs)], axis=0)
        probs = []
        for g in range(GQA_GROUP):
            h = kh * GQA_GROUP + g
            l = logits[:, g * t_new:(g + 1) * t_new, :] - pen_ref[h]
            probs.append(_softmax_with_sink(l, sinks_ref[h]).astype(_BF16))
        pr = jnp.concatenate(probs, axis=1)
        o = jnp.stack([jnp.dot(pr[b], vb[b, :, hd], preferred_element_type=_F32) for b in range(bs)], axis=0)
        o_kh = jnp.concatenate([o[:, g * t_new:(g + 1) * t_new, :] for g in range(GQA_GROUP)], axis=2)
        zb = zbs[kh][...]
        res = (zb * jax.nn.sigmoid(zb) * o_kh).reshape(bs * t_new, gw)
        o_ref[:, kh * gw:(kh + 1) * gw] = res.astype(_BF16)


def _attn_sample(proj3, cache_k, cache_v, slopes, sinks, b_all, seq0, bs=16):
    nseq = cache_k.shape[0]
    t_new = proj3.shape[1]
    blk0 = seq0 // bs
    gw = GQA_GROUP * HEAD_DIM
    smem = pl.BlockSpec(memory_space=pltpu.SMEM)
    kern = functools.partial(_attn_sample_kernel, bs=bs, t_new=t_new)

    def zb_spec(kh):
        return pl.BlockSpec((bs, t_new, gw), lambda i: (blk0 + i, 0, COL_ZB // gw + kh))

    cache_spec = pl.BlockSpec((bs, WINDOW, KV_WIDTH), lambda i: (i, 0, 0))
    return pl.pallas_call(
        kern,
        grid=(nseq // bs,),
        in_specs=[
            smem, smem,
            pl.BlockSpec((bs, t_new, ATTN_WIDTH), lambda i: (blk0 + i, 0, COL_Q // ATTN_WIDTH)),
            pl.BlockSpec((bs, t_new, 2 * KV_WIDTH), lambda i: (blk0 + i, 0, COL_K // (2 * KV_WIDTH))),
            cache_spec, cache_spec,
            zb_spec(0), zb_spec(1), zb_spec(2), zb_spec(3),
            pl.BlockSpec(memory_space=pl.ANY),
        ],
        out_specs=[
            pl.BlockSpec((bs * t_new, ATTN_WIDTH), lambda i: (blk0 + i, 0)),
            cache_spec, cache_spec,
        ],
        out_shape=[
            jax.ShapeDtypeStruct(b_all.shape, _BF16),
            jax.ShapeDtypeStruct((nseq, WINDOW, KV_WIDTH), _F32),
            jax.ShapeDtypeStruct((nseq, WINDOW, KV_WIDTH), _F32),
        ],
        input_output_aliases={10: 0},
        scratch_shapes=[pltpu.VMEM((N_HEADS, t_new, WINDOW + t_new), _F32)],
        compiler_params=_params(("arbitrary",), 48),
        name="attn_sample",
    )(slopes, sinks, proj3, proj3, cache_k, cache_v, proj3, proj3, proj3, proj3, b_all)


def _merge_kernel(a_ref, b_ref, wp_ref, wa_ref, ga_ref, gb_ref, h_ref, wpb_ref, wab_ref, *, chunk):
    @pl.when(pl.program_id(1) == 0)
    def _cast_weight_tiles():
        wpb_ref[...] = wp_ref[...].astype(_BF16)
        wab_ref[...] = wa_ref[...].astype(_BF16)

    for r in range(0, a_ref.shape[0], chunk):
        rows = pl.ds(r, chunk)
        ya = jnp.dot(a_ref[rows, :], wpb_ref[...], preferred_element_type=_F32)
        yb = jnp.dot(b_ref[rows, :], wab_ref[...], preferred_element_type=_F32)
        h = jax.nn.sigmoid(ga_ref[rows, :]) * ya + jax.nn.sigmoid(gb_ref[rows, :]) * yb
        h_ref[rows, :] = h.astype(_BF16)


def _merge(a, b, proj, w_out_pool, w_out_attn, tm=1024, tn=512, chunk=256):
    rows = a.shape[0]
    return pl.pallas_call(
        functools.partial(_merge_kernel, chunk=chunk),
        grid=(D_MODEL // tn, rows // tm),
        in_specs=[
            pl.BlockSpec((tm, POOL_WIDTH), lambda j, i: (i, 0)),
            pl.BlockSpec((tm, ATTN_WIDTH), lambda j, i: (i, 0)),
            pl.BlockSpec((POOL_WIDTH, tn), lambda j, i: (0, j)),
            pl.BlockSpec((ATTN_WIDTH, tn), lambda j, i: (0, j)),
            pl.BlockSpec((tm, tn), lambda j, i: (i, COL_GA // tn + j)),
            pl.BlockSpec((tm, tn), lambda j, i: (i, COL_GB // tn + j)),
        ],
        out_specs=pl.BlockSpec((tm, tn), lambda j, i: (i, j)),
        out_shape=jax.ShapeDtypeStruct((rows, D_MODEL), _BF16),
        scratch_shapes=[pltpu.VMEM((POOL_WIDTH, tn), _BF16), pltpu.VMEM((ATTN_WIDTH, tn), _BF16)],
        compiler_params=_params(("arbitrary", "arbitrary"), 56),
        name="merge",
    )(a, b, w_out_pool, w_out_attn, proj, proj)


def _out_kernel(h_ref, w_ref, x_ref, gain_ref, bias_ref, y_ref):
    y = DEEPNORM_ALPHA * x_ref[...] + jnp.dot(h_ref[...], w_ref[...], preferred_element_type=_F32)
    mu = jnp.mean(y, axis=-1, keepdims=True)
    d = y - mu
    var = jnp.mean(d * d, axis=-1, keepdims=True)
    y_ref[...] = d * lax.rsqrt(var + LN_EPS) * gain_ref[...] + bias_ref[...]


def _out(h_all, row0, w_out_bf16, x, gain, bias, tm=256):
    rows = x.shape[0]
    blk0 = row0 // tm
    return pl.pallas_call(
        _out_kernel,
        grid=(rows // tm,),
        in_specs=[
            pl.BlockSpec((tm, D_MODEL), lambda i: (blk0 + i, 0)),
            pl.BlockSpec((D_MODEL, D_MODEL), lambda i: (0, 0), pipeline_mode=pl.Buffered(1)),
            pl.BlockSpec((tm, D_MODEL), lambda i: (i, 0)),
            pl.BlockSpec((1, D_MODEL), lambda i: (0, 0)),
            pl.BlockSpec((1, D_MODEL), lambda i: (0, 0)),
        ],
        out_specs=pl.BlockSpec((tm, D_MODEL), lambda i: (i, 0)),
        out_shape=jax.ShapeDtypeStruct((rows, D_MODEL), _F32),
        compiler_params=_params(("arbitrary",), 60),
        name="out",
    )(h_all, w_out_bf16, x, gain, bias)


def _alibi_slopes():
    h = jnp.arange(N_HEADS, dtype=_F32)
    return 2.0 ** (-8.0 * (h + 1.0) / N_HEADS)


def kernel(x_prompt, x_sample, cache_k, cache_v, state_pool, w_in, w_pool, pool_scale, sinks,
           w_out_pool, w_out_attn, w_out, ln_gain, ln_bias):
    depth = w_in.shape[0]
    assert depth == 1
    bp, tp, _ = x_prompt.shape
    bs, ts, _ = x_sample.shape
    slopes = _alibi_slopes()

    l = 0
    w_pool_b = w_pool[l].astype(_BF16)
    w_o_b = w_out[l].astype(_BF16)
    scale = pool_scale[l].reshape(1, POOL_WIDTH)
    gain = ln_gain[l].reshape(1, D_MODEL)
    bias = ln_bias[l].reshape(1, D_MODEL)
    sink = sinks[l]

    xp2 = x_prompt.reshape(bp * tp, D_MODEL)
    xs2 = x_sample.reshape(bs * ts, D_MODEL)
    rows_p = bp * tp
    x_all = jnp.concatenate([xp2.astype(_BF16), xs2.astype(_BF16)], axis=0)
    proj = _proj(x_all, w_in[l])
    proj3 = proj.reshape(proj.shape[0] // ts, ts, N_IN)

    a_all = _pool_prompt(proj, w_pool_b, scale, bp, tp)
    state16 = jnp.pad(state_pool[l], ((0, 0), (HALO_ROWS - POOL_STATE, 0), (0, 0)))
    a_all, new_pool_s = _pool_sample(proj3, state16, w_pool_b, scale, a_all, rows_p // ts)

    b_all = _attn_prompt(proj, slopes, sink, bp, tp)
    ck = cache_k[l].reshape(bs, WINDOW, KV_WIDTH)
    cv = cache_v[l].reshape(bs, WINDOW, KV_WIDTH)
    b_all, new_k_s, new_v_s = _attn_sample(proj3, ck, cv, slopes, sink, b_all, rows_p // ts)

    h_all = _merge(a_all, b_all, proj, w_out_pool[l], w_out_attn[l])
    y_p = _out(h_all, 0, w_o_b, xp2, gain, bias).reshape(bp, tp, D_MODEL)
    y_s = _out(h_all, rows_p, w_o_b, xs2, gain, bias).reshape(bs, ts, D_MODEL)

    proj_p3 = proj[:rows_p].reshape(bp, tp, N_IN)
    new_k_p = proj_p3[:, tp - WINDOW:, COL_K:COL_K + KV_WIDTH].reshape(1, bp, WINDOW, N_KV_HEADS, HEAD_DIM)
    new_v_p = proj_p3[:, tp - WINDOW:, COL_V:COL_V + KV_WIDTH].reshape(1, bp, WINDOW, N_KV_HEADS, HEAD_DIM)
    new_pool_p = proj_p3[:, tp - POOL_STATE:, COL_U:COL_U + POOL_WIDTH][None]
    new_k_s = new_k_s.reshape(1, bs, WINDOW, N_KV_HEADS, HEAD_DIM)
    new_v_s = new_v_s.reshape(1, bs, WINDOW, N_KV_HEADS, HEAD_DIM)

    return (y_p, y_s, new_k_p, new_v_p, new_pool_p, new_k_s, new_v_s, new_pool_s[None])
```

```python
import functools

import jax
import jax.numpy as jnp
from jax import lax
from jax.experimental import pallas as pl
from jax.experimental.pallas import tpu as pltpu

D_MODEL = 4096
POOL_WIDTH = 2048
POOL_WINDOWS = (2, 4, 8, 16)
POOL_GROUP_WIDTH = 512
POOL_STATE = 15
HEAD_DIM = 64
N_HEADS = 32
N_KV_HEADS = 4
GQA_GROUP = 8
ATTN_WIDTH = 2048
KV_WIDTH = 256
WINDOW = 128
PAST_LEN = 8192
N_IN = 16896
COL_U, COL_ZA, COL_Q, COL_K, COL_V, COL_ZB, COL_GA, COL_GB = 0, 2048, 4096, 6144, 6400, 6656, 8704, 12800
DEEPNORM_ALPHA = 2.0 ** 0.25
LN_EPS = 1e-5

V7X_VMEM_BYTES = 64 * 1024 * 1024
SUBLANES = 8
HALO_ROWS = 16

_F32 = jnp.float32
_BF16 = jnp.bfloat16


def _params(sem, vmem_mib):
    return pltpu.CompilerParams(dimension_semantics=sem, vmem_limit_bytes=vmem_mib * 1024 * 1024)


def _proj_kernel(x_ref, w_ref, o_ref, wb_ref):
    @pl.when(pl.program_id(1) == 0)
    def _cast_weight_tile():
        wb_ref[...] = w_ref[...].astype(_BF16)

    o_ref[...] = jnp.dot(x_ref[...], wb_ref[...], preferred_element_type=_F32)


def _proj(x_bf16, w_f32, tm=1024, tn=768):
    rows, k = x_bf16.shape
    n = w_f32.shape[1]
    return pl.pallas_call(
        _proj_kernel,
        grid=(n // tn, rows // tm),
        in_specs=[pl.BlockSpec((tm, k), lambda j, i: (i, 0)),
                  pl.BlockSpec((k, tn), lambda j, i: (0, j))],
        out_specs=pl.BlockSpec((tm, tn), lambda j, i: (i, j)),
        out_shape=jax.ShapeDtypeStruct((rows, n), _F32),
        scratch_shapes=[pltpu.VMEM((k, tn), _BF16)],
        compiler_params=_params(("arbitrary", "arbitrary"), 60),
        name="proj",
    )(x_bf16, w_f32)


def _window_sums(ext, n_doublings):
    s = ext
    for d in range(n_doublings):
        s = s + pltpu.roll(s, 1 << d, axis=0)
    return s


def _pool_groups(ext, u, z_a, pos, wp_ref, scale_ref, halo):
    outs = []
    for g, w in enumerate(POOL_WINDOWS):
        sl = slice(g * POOL_GROUP_WIDTH, (g + 1) * POOL_GROUP_WIDTH)
        s = _window_sums(ext[:, sl], g + 1)[halo:]
        count = jnp.minimum(w, pos + 1).astype(_F32)
        pooled = s / count - u[:, sl]
        mixed = jnp.dot(pooled.astype(_BF16), wp_ref[g], preferred_element_type=_F32)
        pool_out = mixed * scale_ref[:, sl]
        z = z_a[:, sl]
        outs.append(z * jax.nn.sigmoid(z) * pool_out)
    return outs


def _pool_prompt_kernel(u_ref, halo_ref, za_ref, wp_ref, scale_ref, a_ref, *, tiles_per_seq, tm):
    t = pl.program_id(0) % tiles_per_seq
    u = u_ref[...]
    halo = jnp.where(t == 0, 0.0, halo_ref[...])
    ext = jnp.concatenate([halo, u], axis=0)
    pos = t * tm + lax.broadcasted_iota(jnp.int32, (tm, 1), 0)
    outs = _pool_groups(ext, u, za_ref[...], pos, wp_ref, scale_ref, HALO_ROWS)
    for g, o in enumerate(outs):
        a_ref[:, g * POOL_GROUP_WIDTH:(g + 1) * POOL_GROUP_WIDTH] = o.astype(_BF16)


def _pool_prompt(proj, w_pool_bf16, pool_scale, batch, seq, tm=256):
    rows = proj.shape[0]
    tiles_per_seq = seq // tm
    halo_blocks = tm // HALO_ROWS
    kern = functools.partial(_pool_prompt_kernel, tiles_per_seq=tiles_per_seq, tm=tm)
    return pl.pallas_call(
        kern,
        grid=(batch * seq // tm,),
        in_specs=[
            pl.BlockSpec((tm, POOL_WIDTH), lambda i: (i, COL_U // POOL_WIDTH)),
            pl.BlockSpec((HALO_ROWS, POOL_WIDTH), lambda i: (jnp.maximum(i * halo_blocks - 1, 0), 0)),
            pl.BlockSpec((tm, POOL_WIDTH), lambda i: (i, COL_ZA // POOL_WIDTH)),
            pl.BlockSpec((4, POOL_GROUP_WIDTH, POOL_GROUP_WIDTH), lambda i: (0, 0, 0)),
            pl.BlockSpec((1, POOL_WIDTH), lambda i: (0, 0)),
        ],
        out_specs=pl.BlockSpec((tm, POOL_WIDTH), lambda i: (i, 0)),
        out_shape=jax.ShapeDtypeStruct((rows, POOL_WIDTH), _BF16),
        compiler_params=_params(("arbitrary",), 48),
        name="pool_prompt",
    )(proj, proj, proj, w_pool_bf16, pool_scale)


def _pool_sample_kernel(u_ref, za_ref, st_ref, wp_ref, scale_ref, a_in_ref, a_ref, np_ref, *, bs, t_new):
    del a_in_ref
    u3 = u_ref[...]
    st = st_ref[...]
    rows_per_seq = HALO_ROWS + t_new
    ext = jnp.concatenate([st, u3], axis=1).reshape(bs * rows_per_seq, POOL_WIDTH)
    pos3 = PAST_LEN + lax.broadcasted_iota(jnp.int32, (bs, t_new, 1), 1)
    pos = pos3.reshape(bs * t_new, 1)
    u = u3.reshape(bs * t_new, POOL_WIDTH)
    za = za_ref[...].reshape(bs * t_new, POOL_WIDTH)
    for g, w in enumerate(POOL_WINDOWS):
        sl = slice(g * POOL_GROUP_WIDTH, (g + 1) * POOL_GROUP_WIDTH)
        s = _window_sums(ext[:, sl], g + 1)
        s = s.reshape(bs, rows_per_seq, POOL_GROUP_WIDTH)[:, HALO_ROWS:, :].reshape(bs * t_new, POOL_GROUP_WIDTH)
        count = jnp.minimum(w, pos + 1).astype(_F32)
        pooled = s / count - u[:, sl]
        mixed = jnp.dot(pooled.astype(_BF16), wp_ref[g], preferred_element_type=_F32)
        pool_out = mixed * scale_ref[:, sl]
        z = za[:, sl]
        a_ref[:, sl] = (z * jax.nn.sigmoid(z) * pool_out).astype(_BF16)
    keep = POOL_STATE - t_new
    np_ref[:, 0:keep, :] = st[:, HALO_ROWS - keep:, :]
    np_ref[:, keep:, :] = u3


def _pool_sample(proj3, state16, w_pool_bf16, pool_scale, a_all, seq0, bs=16):
    nseq = state16.shape[0]
    t_new = proj3.shape[1]
    blk0 = seq0 // bs
    kern = functools.partial(_pool_sample_kernel, bs=bs, t_new=t_new)
    return pl.pallas_call(
        kern,
        grid=(nseq // bs,),
        in_specs=[
            pl.BlockSpec((bs, t_new, POOL_WIDTH), lambda i: (blk0 + i, 0, COL_U // POOL_WIDTH)),
            pl.BlockSpec((bs, t_new, POOL_WIDTH), lambda i: (blk0 + i, 0, COL_ZA // POOL_WIDTH)),
            pl.BlockSpec((bs, HALO_ROWS, POOL_WIDTH), lambda i: (i, 0, 0)),
            pl.BlockSpec((4, POOL_GROUP_WIDTH, POOL_GROUP_WIDTH), lambda i: (0, 0, 0)),
            pl.BlockSpec((1, POOL_WIDTH), lambda i: (0, 0)),
            pl.BlockSpec(memory_space=pl.ANY),
        ],
        out_specs=[
            pl.BlockSpec((bs * t_new, POOL_WIDTH), lambda i: (blk0 + i, 0)),
            pl.BlockSpec((bs, POOL_STATE, POOL_WIDTH), lambda i: (i, 0, 0)),
        ],
        out_shape=[
            jax.ShapeDtypeStruct(a_all.shape, _BF16),
            jax.ShapeDtypeStruct((nseq, POOL_STATE, POOL_WIDTH), _F32),
        ],
        input_output_aliases={5: 0},
        compiler_params=_params(("arbitrary",), 48),
        name="pool_sample",
    )(proj3, proj3, state16, w_pool_bf16, pool_scale, a_all)


def _softmax_with_sink(l, sink):
    m = jnp.maximum(jnp.max(l, axis=-1, keepdims=True), sink)
    p = jnp.exp(l - m)
    den = jnp.sum(p, axis=-1, keepdims=True) + jnp.exp(sink - m)
    return p * (1.0 / den)


def _penalty(slope, dist, valid):
    return jnp.where(valid, slope * dist.astype(_F32), jnp.inf)


def _attn_prompt_kernel(slopes_ref, sinks_ref, q_ref, kvc_ref, kvp_ref, zb0, zb1, zb2, zb3, o_ref,
                        pen_ref, sink_ref):
    blk = WINDOW
    gw = GQA_GROUP * HEAD_DIM
    nq = GQA_GROUP * blk
    n = pl.program_id(1)

    @pl.when((pl.program_id(0) == 0) & (n == 0))
    def _fill_tables():
        si = lax.broadcasted_iota(jnp.int32, (2 * blk, blk), 0)
        ti = lax.broadcasted_iota(jnp.int32, (2 * blk, blk), 1)
        dist = blk + ti - si
        for n_tbl in (0, 1):
            key_pos = (n_tbl - 1) * blk + si
            valid = (dist >= 0) & (dist < WINDOW) & (key_pos >= 0)
            for h in range(N_HEADS):
                kh, g = divmod(h, GQA_GROUP)
                pen_ref[n_tbl, kh, :, g * blk:(g + 1) * blk] = _penalty(slopes_ref[h], dist, valid)
        for h in range(N_HEADS):
            kh, g = divmod(h, GQA_GROUP)
            sink_ref[kh, :, g * blk:(g + 1) * blk] = jnp.full((1, blk), sinks_ref[h], _F32)

    tbl = jnp.minimum(n, 1)
    kv = jnp.concatenate([kvp_ref[...], kvc_ref[...]], axis=0)
    k_all = kv[:, 0:KV_WIDTH].astype(_BF16)
    v_t = kv[:, KV_WIDTH:2 * KV_WIDTH].T.astype(_BF16)
    zbs = (zb0, zb1, zb2, zb3)
    o_t = []
    for kh in range(N_KV_HEADS):
        hd = slice(kh * HEAD_DIM, (kh + 1) * HEAD_DIM)
        q_kh = (q_ref[:, kh * gw:(kh + 1) * gw] * (HEAD_DIM ** -0.5)).astype(_BF16)
        qs = jnp.concatenate([q_kh[:, g * HEAD_DIM:(g + 1) * HEAD_DIM] for g in range(GQA_GROUP)], axis=0)
        logits_t = lax.dot_general(k_all[:, hd], qs, (((1,), (1,)), ((), ())),
                                   preferred_element_type=_F32)
        l = logits_t - pen_ref[tbl, kh]
        sink = sink_ref[kh]
        m = jnp.maximum(jnp.max(l, axis=0, keepdims=True), sink)
        p = jnp.exp(l - m)
        den = jnp.sum(p, axis=0, keepdims=True) + jnp.exp(sink - m)
        pr_t = (p * (1.0 / den)).astype(_BF16)
        o_t.append(jnp.dot(v_t[hd, :], pr_t, preferred_element_type=_F32))
    for pair in range(N_KV_HEADS // 2):
        o_pair = jnp.concatenate([o_t[2 * pair], o_t[2 * pair + 1]], axis=0)
        tiles = [o_pair[:, g * blk:(g + 1) * blk].T for g in range(GQA_GROUP)]
        for j in range(2):
            kh = 2 * pair + j
            o_kh = jnp.concatenate([t[:, j * HEAD_DIM:(j + 1) * HEAD_DIM] for t in tiles], axis=1)
            zb = zbs[kh][...]
            o_ref[:, kh * gw:(kh + 1) * gw] = (zb * jax.nn.sigmoid(zb) * o_kh).astype(_BF16)


def _attn_prompt(proj, slopes, sinks, batch, seq):
    blk = WINDOW
    nb = seq // blk
    gw = GQA_GROUP * HEAD_DIM
    smem = pl.BlockSpec(memory_space=pltpu.SMEM)

    def zb_spec(kh):
        return pl.BlockSpec((blk, gw), lambda b, n: (b * nb + n, COL_ZB // gw + kh))

    return pl.pallas_call(
        _attn_prompt_kernel,
        grid=(batch, nb),
        in_specs=[
            smem, smem,
            pl.BlockSpec((blk, ATTN_WIDTH), lambda b, n: (b * nb + n, COL_Q // ATTN_WIDTH)),
            pl.BlockSpec((blk, 2 * KV_WIDTH), lambda b, n: (b * nb + n, COL_K // (2 * KV_WIDTH))),
            pl.BlockSpec((blk, 2 * KV_WIDTH), lambda b, n: (b * nb + jnp.maximum(n - 1, 0), COL_K // (2 * KV_WIDTH))),
            zb_spec(0), zb_spec(1), zb_spec(2), zb_spec(3),
        ],
        out_specs=pl.BlockSpec((blk, ATTN_WIDTH), lambda b, n: (b * nb + n, 0)),
        out_shape=jax.ShapeDtypeStruct((proj.shape[0], ATTN_WIDTH), _BF16),
        scratch_shapes=[pltpu.VMEM((2, N_KV_HEADS, 2 * blk, GQA_GROUP * blk), _F32),
                        pltpu.VMEM((N_KV_HEADS, 1, GQA_GROUP * blk), _F32)],
        compiler_params=_params(("arbitrary", "arbitrary"), 48),
        name="attn_prompt",
    )(slopes, sinks, proj, proj, proj, proj, proj, proj, proj)


def _attn_sample_kernel(slopes_ref, sinks_ref, q_ref, kvn_ref, ck_ref, cv_ref, zb0, zb1, zb2, zb3, b_in_ref,
                        o_ref, nk_ref, nv_ref, pen_ref, *, bs, t_new):
    del b_in_ref
    s_all = WINDOW + t_new
    gw = GQA_GROUP * HEAD_DIM

    @pl.when(pl.program_id(0) == 0)
    def _fill_penalty_table():
        ti = lax.broadcasted_iota(jnp.int32, (t_new, s_all), 0)
        si = lax.broadcasted_iota(jnp.int32, (t_new, s_all), 1)
        dist = WINDOW + ti - si
        key_pos = PAST_LEN - WINDOW + si
        valid = (dist >= 0) & (dist < WINDOW) & (key_pos >= 0)
        for h in range(N_HEADS):
            pen_ref[h] = _penalty(slopes_ref[h], dist, valid)

    kvn = kvn_ref[...]
    k_all = jnp.concatenate([ck_ref[...], kvn[:, :, 0:KV_WIDTH]], axis=1)
    v_all = jnp.concatenate([cv_ref[...], kvn[:, :, KV_WIDTH:2 * KV_WIDTH]], axis=1)
    nk_ref[...] = k_all[:, t_new:, :]
    nv_ref[...] = v_all[:, t_new:, :]
    kb = k_all.astype(_BF16)
    vb = v_all.astype(_BF16)
    zbs = (zb0, zb1, zb2, zb3)
    for kh in range(N_KV_HEADS):
        q_kh = (q_ref[:, :, kh * gw:(kh + 1) * gw] * (HEAD_DIM ** -0.5)).astype(_BF16)
        qs = jnp.concatenate([q_kh[:, :, g * HEAD_DIM:(g + 1) * HEAD_DIM] for g in range(GQA_GROUP)], axis=1)
        hd = slice(kh * HEAD_DIM, (kh + 1) * HEAD_DIM)
        logits = jnp.stack(
            [lax.dot_general(qs[b], kb[b, :, hd], (((1,), (1,)), ((), ())), preferred_element_type=_F32)
             for b in range(bs)], axis=0)
        probs = []
        for g in range(GQA_GROUP):
            h = kh * GQA_GROUP + g
            l = logits[:, g * t_new:(g + 1) * t_new, :] - pen_ref[h]
            probs.append(_softmax_with_sink(l, sinks_ref[h]).astype(_BF16))
        pr = jnp.concatenate(probs, axis=1)
        o = jnp.stack([jnp.dot(pr[b], vb[b, :, hd], preferred_element_type=_F32) for b in range(bs)], axis=0)
        o_kh = jnp.concatenate([o[:, g * t_new:(g + 1) * t_new, :] for g in range(GQA_GROUP)], axis=2)
        zb = zbs[kh][...]
        res = (zb * jax.nn.sigmoid(zb) * o_kh).reshape(bs * t_new, gw)
        o_ref[:, kh * gw:(kh + 1) * gw] = res.astype(_BF16)


def _attn_sample(proj3, cache_k, cache_v, slopes, sinks, b_all, seq0, bs=16):
    nseq = cache_k.shape[0]
    t_new = proj3.shape[1]
    blk0 = seq0 // bs
    gw = GQA_GROUP * HEAD_DIM
    smem = pl.BlockSpec(memory_space=pltpu.SMEM)
    kern = functools.partial(_attn_sample_kernel, bs=bs, t_new=t_new)

    def zb_spec(kh):
        return pl.BlockSpec((bs, t_new, gw), lambda i: (blk0 + i, 0, COL_ZB // gw + kh))

    cache_spec = pl.BlockSpec((bs, WINDOW, KV_WIDTH), lambda i: (i, 0, 0))
    return pl.pallas_call(
        kern,
        grid=(nseq // bs,),
        in_specs=[
            smem, smem,
            pl.BlockSpec((bs, t_new, ATTN_WIDTH), lambda i: (blk0 + i, 0, COL_Q // ATTN_WIDTH)),
            pl.BlockSpec((bs, t_new, 2 * KV_WIDTH), lambda i: (blk0 + i, 0, COL_K // (2 * KV_WIDTH))),
            cache_spec, cache_spec,
            zb_spec(0), zb_spec(1), zb_spec(2), zb_spec(3),
            pl.BlockSpec(memory_space=pl.ANY),
        ],
        out_specs=[
            pl.BlockSpec((bs * t_new, ATTN_WIDTH), lambda i: (blk0 + i, 0)),
            cache_spec, cache_spec,
        ],
        out_shape=[
            jax.ShapeDtypeStruct(b_all.shape, _BF16),
            jax.ShapeDtypeStruct((nseq, WINDOW, KV_WIDTH), _F32),
            jax.ShapeDtypeStruct((nseq, WINDOW, KV_WIDTH), _F32),
        ],
        input_output_aliases={10: 0},
        scratch_shapes=[pltpu.VMEM((N_HEADS, t_new, WINDOW + t_new), _F32)],
        compiler_params=_params(("arbitrary",), 48),
        name="attn_sample",
    )(slopes, sinks, proj3, proj3, cache_k, cache_v, proj3, proj3, proj3, proj3, b_all)


def _merge_kernel(a_ref, b_ref, wp_ref, wa_ref, ga_ref, gb_ref, h_ref, wpb_ref, wab_ref, *, chunk):
    @pl.when(pl.program_id(1) == 0)
    def _cast_weight_tiles():
        wpb_ref[...] = wp_ref[...].astype(_BF16)
        wab_ref[...] = wa_ref[...].astype(_BF16)

    for r in range(0, a_ref.shape[0], chunk):
        rows = pl.ds(r, chunk)
        ya = jnp.dot(a_ref[rows, :], wpb_ref[...], preferred_element_type=_F32)
        yb = jnp.dot(b_ref[rows, :], wab_ref[...], preferred_element_type=_F32)
        h = jax.nn.sigmoid(ga_ref[rows, :]) * ya + jax.nn.sigmoid(gb_ref[rows, :]) * yb
        h_ref[rows, :] = h.astype(_BF16)


def _merge(a, b, proj, w_out_pool, w_out_attn, tm=1024, tn=512, chunk=256):
    rows = a.shape[0]
    return pl.pallas_call(
        functools.partial(_merge_kernel, chunk=chunk),
        grid=(D_MODEL // tn, rows // tm),
        in_specs=[
            pl.BlockSpec((tm, POOL_WIDTH), lambda j, i: (i, 0)),
            pl.BlockSpec((tm, ATTN_WIDTH), lambda j, i: (i, 0)),
            pl.BlockSpec((POOL_WIDTH, tn), lambda j, i: (0, j)),
            pl.BlockSpec((ATTN_WIDTH, tn), lambda j, i: (0, j)),
            pl.BlockSpec((tm, tn), lambda j, i: (i, COL_GA // tn + j)),
            pl.BlockSpec((tm, tn), lambda j, i: (i, COL_GB // tn + j)),
        ],
        out_specs=pl.BlockSpec((tm, tn), lambda j, i: (i, j)),
        out_shape=jax.ShapeDtypeStruct((rows, D_MODEL), _BF16),
        scratch_shapes=[pltpu.VMEM((POOL_WIDTH, tn), _BF16), pltpu.VMEM((ATTN_WIDTH, tn), _BF16)],
        compiler_params=_params(("arbitrary", "arbitrary"), 56),
        name="merge",
    )(a, b, w_out_pool, w_out_attn, proj, proj)


def _out_kernel(h_ref, w_ref, x_ref, gain_ref, bias_ref, y_ref):
    y_ref[...] = jnp.dot(h_ref[...], w_ref[...], preferred_element_type=_F32)
    gain = gain_ref[...]
    bias = bias_ref[...]
    for g in range(h_ref.shape[0] // SUBLANES):
        rows = pl.ds(g * SUBLANES, SUBLANES)
        y = DEEPNORM_ALPHA * x_ref[rows, :] + y_ref[rows, :]
        mu = jnp.mean(y, axis=-1, keepdims=True)
        d = y - mu
        var = jnp.mean(d * d, axis=-1, keepdims=True)
        y_ref[rows, :] = d * lax.rsqrt(var + LN_EPS) * gain + bias


def _out(h_all, row0, w_out_bf16, x, gain, bias, tm=256):
    rows = x.shape[0]
    blk0 = row0 // tm
    return pl.pallas_call(
        _out_kernel,
        grid=(rows // tm,),
        in_specs=[
            pl.BlockSpec((tm, D_MODEL), lambda i: (blk0 + i, 0)),
            pl.BlockSpec((D_MODEL, D_MODEL), lambda i: (0, 0), pipeline_mode=pl.Buffered(1)),
            pl.BlockSpec((tm, D_MODEL), lambda i: (i, 0)),
            pl.BlockSpec((1, D_MODEL), lambda i: (0, 0)),
            pl.BlockSpec((1, D_MODEL), lambda i: (0, 0)),
        ],
        out_specs=pl.BlockSpec((tm, D_MODEL), lambda i: (i, 0)),
        out_shape=jax.ShapeDtypeStruct((rows, D_MODEL), _F32),
        compiler_params=_params(("arbitrary",), 60),
        name="out",
    )(h_all, w_out_bf16, x, gain, bias)


def _alibi_slopes():
    h = jnp.arange(N_HEADS, dtype=_F32)
    return 2.0 ** (-8.0 * (h + 1.0) / N_HEADS)


def kernel(x_prompt, x_sample, cache_k, cache_v, state_pool, w_in, w_pool, pool_scale, sinks,
           w_out_pool, w_out_attn, w_out, ln_gain, ln_bias):
    depth = w_in.shape[0]
    assert depth == 1
    bp, tp, _ = x_prompt.shape
    bs, ts, _ = x_sample.shape
    slopes = _alibi_slopes()

    l = 0
    w_pool_b = w_pool[l].astype(_BF16)
    w_o_b = w_out[l].astype(_BF16)
    scale = pool_scale[l].reshape(1, POOL_WIDTH)
    gain = ln_gain[l].reshape(1, D_MODEL)
    bias = ln_bias[l].reshape(1, D_MODEL)
    sink = sinks[l]

    xp2 = x_prompt.reshape(bp * tp, D_MODEL)
    xs2 = x_sample.reshape(bs * ts, D_MODEL)
    rows_p = bp * tp
    x_all = jnp.concatenate([xp2, xs2], axis=0).astype(_BF16)
    proj = _proj(x_all, w_in[l])
    proj3 = proj.reshape(proj.shape[0] // ts, ts, N_IN)

    a_all = _pool_prompt(proj, w_pool_b, scale, bp, tp)
    state16 = jnp.pad(state_pool[l], ((0, 0), (HALO_ROWS - POOL_STATE, 0), (0, 0)))
    a_all, new_pool_s = _pool_sample(proj3, state16, w_pool_b, scale, a_all, rows_p // ts)

    b_all = _attn_prompt(proj, slopes, sink, bp, tp)
    ck = cache_k[l].reshape(bs, WINDOW, KV_WIDTH)
    cv = cache_v[l].reshape(bs, WINDOW, KV_WIDTH)
    b_all, new_k_s, new_v_s = _attn_sample(proj3, ck, cv, slopes, sink, b_all, rows_p // ts)

    h_all = _merge(a_all, b_all, proj, w_out_pool[l], w_out_attn[l])
    y_p = _out(h_all, 0, w_o_b, xp2, gain, bias).reshape(bp, tp, D_MODEL)
    y_s = _out(h_all, rows_p, w_o_b, xs2, gain, bias).reshape(bs, ts, D_MODEL)

    def tail(n_rows, col0, width):
        return jnp.stack([lax.slice(proj, ((b + 1) * tp - n_rows, col0), ((b + 1) * tp, col0 + width))
                          for b in range(bp)])

    new_k_p = tail(WINDOW, COL_K, KV_WIDTH).reshape(1, bp, WINDOW, N_KV_HEADS, HEAD_DIM)
    new_v_p = tail(WINDOW, COL_V, KV_WIDTH).reshape(1, bp, WINDOW, N_KV_HEADS, HEAD_DIM)
    new_pool_p = tail(POOL_STATE, COL_U, POOL_WIDTH)[None]
    new_k_s = new_k_s.reshape(1, bs, WINDOW, N_KV_HEADS, HEAD_DIM)
    new_v_s = new_v_s.reshape(1, bs, WINDOW, N_KV_HEADS, HEAD_DIM)

    return (y_p, y_s, new_k_p, new_v_p, new_pool_p, new_k_s, new_v_s, new_pool_s[None])
```

```python
import functools

import jax
import jax.numpy as jnp
from jax import lax
from jax.experimental import pallas as pl
from jax.experimental.pallas import tpu as pltpu

D_MODEL = 4096
POOL_WIDTH = 2048
POOL_WINDOWS = (2, 4, 8, 16)
POOL_GROUP_WIDTH = 512
POOL_STATE = 15
HEAD_DIM = 64
N_HEADS = 32
N_KV_HEADS = 4
GQA_GROUP = 8
ATTN_WIDTH = 2048
KV_WIDTH = 256
WINDOW = 128
PAST_LEN = 8192
N_IN = 16896
COL_U, COL_ZA, COL_Q, COL_K, COL_V, COL_ZB, COL_GA, COL_GB = 0, 2048, 4096, 6144, 6400, 6656, 8704, 12800
DEEPNORM_ALPHA = 2.0 ** 0.25
LN_EPS = 1e-5

V7X_VMEM_BYTES = 64 * 1024 * 1024
SUBLANES = 8
HALO_ROWS = 16
PROJ_TN = 768

_F32 = jnp.float32
_BF16 = jnp.bfloat16


def _params(sem, vmem_mib):
    return pltpu.CompilerParams(dimension_semantics=sem, vmem_limit_bytes=vmem_mib * 1024 * 1024)


def _proj_kernel(x_ref, w_ref, o_ref, wb_ref):
    @pl.when(pl.program_id(1) == 0)
    def _cast_weight_tile():
        wb_ref[...] = w_ref[...].astype(_BF16)

    o_ref[...] = jnp.dot(x_ref[...], wb_ref[...], preferred_element_type=_F32)


def _proj_rest_kernel(x_ref, w_ref, proj_in_ref, o_ref, wb_ref):
    del proj_in_ref
    _proj_kernel(x_ref, w_ref, o_ref, wb_ref)


def _proj_rest(x_bf16, w_f32, proj, tm=1024, tn=PROJ_TN):
    rows, k = x_bf16.shape
    n = w_f32.shape[1]
    return pl.pallas_call(
        _proj_rest_kernel,
        grid=(n // tn - 1, rows // tm),
        in_specs=[pl.BlockSpec((tm, k), lambda j, i: (i, 0)),
                  pl.BlockSpec((k, tn), lambda j, i: (0, j + 1)),
                  pl.BlockSpec(memory_space=pl.ANY)],
        out_specs=pl.BlockSpec((tm, tn), lambda j, i: (i, j + 1)),
        out_shape=jax.ShapeDtypeStruct((rows, n), _F32),
        input_output_aliases={2: 0},
        scratch_shapes=[pltpu.VMEM((k, tn), _BF16)],
        compiler_params=_params(("arbitrary", "arbitrary"), 60),
        name="proj",
    )(x_bf16, w_f32, proj)


def _proj_cast_kernel(x_ref, w_ref, *refs):
    xb_ref, o_ref, wb_ref = refs[-3:]

    @pl.when(pl.program_id(0) == 0)
    def _cast_weight_tile():
        wb_ref[...] = w_ref[...].astype(_BF16)

    xb = x_ref[...].astype(_BF16)
    xb_ref[...] = xb
    o_ref[...] = jnp.dot(xb, wb_ref[...], preferred_element_type=_F32)


def _proj_cast(x_f32, w_f32, row0, total_rows, x_all=None, proj=None, tm=512, tn=PROJ_TN):
    rows, k = x_f32.shape
    n = w_f32.shape[1]
    blk0 = row0 // tm
    aliased = [] if x_all is None else [x_all, proj]
    return pl.pallas_call(
        _proj_cast_kernel,
        grid=(rows // tm,),
        in_specs=[pl.BlockSpec((tm, k), lambda i: (i, 0)),
                  pl.BlockSpec((k, tn), lambda i: (0, 0), pipeline_mode=pl.Buffered(1))]
                 + [pl.BlockSpec(memory_space=pl.ANY)] * len(aliased),
        out_specs=[pl.BlockSpec((tm, k), lambda i: (blk0 + i, 0)),
                   pl.BlockSpec((tm, tn), lambda i: (blk0 + i, 0))],
        out_shape=[jax.ShapeDtypeStruct((total_rows, k), _BF16),
                   jax.ShapeDtypeStruct((total_rows, n), _F32)],
        input_output_aliases={2: 0, 3: 1} if aliased else {},
        scratch_shapes=[pltpu.VMEM((k, tn), _BF16)],
        compiler_params=_params(("arbitrary",), 56),
        name="proj_cast",
    )(x_f32, w_f32, *aliased)


def _window_sums(ext, n_doublings):
    s = ext
    for d in range(n_doublings):
        s = s + pltpu.roll(s, 1 << d, axis=0)
    return s


def _pool_groups(ext, u, z_a, pos, wp_ref, scale_ref, halo):
    outs = []
    for g, w in enumerate(POOL_WINDOWS):
        sl = slice(g * POOL_GROUP_WIDTH, (g + 1) * POOL_GROUP_WIDTH)
        s = _window_sums(ext[:, sl], g + 1)[halo:]
        count = jnp.minimum(w, pos + 1).astype(_F32)
        pooled = s / count - u[:, sl]
        mixed = jnp.dot(pooled.astype(_BF16), wp_ref[g], preferred_element_type=_F32)
        pool_out = mixed * scale_ref[:, sl]
        z = z_a[:, sl]
        outs.append(z * jax.nn.sigmoid(z) * pool_out)
    return outs


def _pool_prompt_kernel(u_ref, halo_ref, za_ref, wp_ref, scale_ref, a_ref, *, tiles_per_seq, tm):
    t = pl.program_id(0) % tiles_per_seq
    u = u_ref[...]
    halo = jnp.where(t == 0, 0.0, halo_ref[...])
    ext = jnp.concatenate([halo, u], axis=0)
    pos = t * tm + lax.broadcasted_iota(jnp.int32, (tm, 1), 0)
    outs = _pool_groups(ext, u, za_ref[...], pos, wp_ref, scale_ref, HALO_ROWS)
    for g, o in enumerate(outs):
        a_ref[:, g * POOL_GROUP_WIDTH:(g + 1) * POOL_GROUP_WIDTH] = o.astype(_BF16)


def _pool_prompt(proj, w_pool_bf16, pool_scale, batch, seq, tm=256):
    rows = proj.shape[0]
    tiles_per_seq = seq // tm
    halo_blocks = tm // HALO_ROWS
    kern = functools.partial(_pool_prompt_kernel, tiles_per_seq=tiles_per_seq, tm=tm)
    return pl.pallas_call(
        kern,
        grid=(batch * seq // tm,),
        in_specs=[
            pl.BlockSpec((tm, POOL_WIDTH), lambda i: (i, COL_U // POOL_WIDTH)),
            pl.BlockSpec((HALO_ROWS, POOL_WIDTH), lambda i: (jnp.maximum(i * halo_blocks - 1, 0), 0)),
            pl.BlockSpec((tm, POOL_WIDTH), lambda i: (i, COL_ZA // POOL_WIDTH)),
            pl.BlockSpec((4, POOL_GROUP_WIDTH, POOL_GROUP_WIDTH), lambda i: (0, 0, 0)),
            pl.BlockSpec((1, POOL_WIDTH), lambda i: (0, 0)),
        ],
        out_specs=pl.BlockSpec((tm, POOL_WIDTH), lambda i: (i, 0)),
        out_shape=jax.ShapeDtypeStruct((rows, POOL_WIDTH), _BF16),
        compiler_params=_params(("arbitrary",), 48),
        name="pool_prompt",
    )(proj, proj, proj, w_pool_bf16, pool_scale)


def _pool_sample_kernel(u_ref, za_ref, st_ref, wp_ref, scale_ref, a_in_ref, a_ref, np_ref, *, bs, t_new):
    del a_in_ref
    u3 = u_ref[...]
    st = st_ref[...]
    rows_per_seq = HALO_ROWS + t_new
    ext = jnp.concatenate([st, u3], axis=1).reshape(bs * rows_per_seq, POOL_WIDTH)
    pos3 = PAST_LEN + lax.broadcasted_iota(jnp.int32, (bs, t_new, 1), 1)
    pos = pos3.reshape(bs * t_new, 1)
    u = u3.reshape(bs * t_new, POOL_WIDTH)
    za = za_ref[...].reshape(bs * t_new, POOL_WIDTH)
    for g, w in enumerate(POOL_WINDOWS):
        sl = slice(g * POOL_GROUP_WIDTH, (g + 1) * POOL_GROUP_WIDTH)
        s = _window_sums(ext[:, sl], g + 1)
        s = s.reshape(bs, rows_per_seq, POOL_GROUP_WIDTH)[:, HALO_ROWS:, :].reshape(bs * t_new, POOL_GROUP_WIDTH)
        count = jnp.minimum(w, pos + 1).astype(_F32)
        pooled = s / count - u[:, sl]
        mixed = jnp.dot(pooled.astype(_BF16), wp_ref[g], preferred_element_type=_F32)
        pool_out = mixed * scale_ref[:, sl]
        z = za[:, sl]
        a_ref[:, sl] = (z * jax.nn.sigmoid(z) * pool_out).astype(_BF16)
    keep = POOL_STATE - t_new
    np_ref[:, 0:keep, :] = st[:, HALO_ROWS - keep:, :]
    np_ref[:, keep:, :] = u3


def _pool_sample(proj3, state16, w_pool_bf16, pool_scale, a_all, seq0, bs=16):
    nseq = state16.shape[0]
    t_new = proj3.shape[1]
    blk0 = seq0 // bs
    kern = functools.partial(_pool_sample_kernel, bs=bs, t_new=t_new)
    return pl.pallas_call(
        kern,
        grid=(nseq // bs,),
        in_specs=[
            pl.BlockSpec((bs, t_new, POOL_WIDTH), lambda i: (blk0 + i, 0, COL_U // POOL_WIDTH)),
            pl.BlockSpec((bs, t_new, POOL_WIDTH), lambda i: (blk0 + i, 0, COL_ZA // POOL_WIDTH)),
            pl.BlockSpec((bs, HALO_ROWS, POOL_WIDTH), lambda i: (i, 0, 0)),
            pl.BlockSpec((4, POOL_GROUP_WIDTH, POOL_GROUP_WIDTH), lambda i: (0, 0, 0)),
            pl.BlockSpec((1, POOL_WIDTH), lambda i: (0, 0)),
            pl.BlockSpec(memory_space=pl.ANY),
        ],
        out_specs=[
            pl.BlockSpec((bs * t_new, POOL_WIDTH), lambda i: (blk0 + i, 0)),
            pl.BlockSpec((bs, POOL_STATE, POOL_WIDTH), lambda i: (i, 0, 0)),
        ],
        out_shape=[
            jax.ShapeDtypeStruct(a_all.shape, _BF16),
            jax.ShapeDtypeStruct((nseq, POOL_STATE, POOL_WIDTH), _F32),
        ],
        input_output_aliases={5: 0},
        compiler_params=_params(("arbitrary",), 48),
        name="pool_sample",
    )(proj3, proj3, state16, w_pool_bf16, pool_scale, a_all)


def _softmax_with_sink(l, sink):
    m = jnp.maximum(jnp.max(l, axis=-1, keepdims=True), sink)
    p = jnp.exp(l - m)
    den = jnp.sum(p, axis=-1, keepdims=True) + jnp.exp(sink - m)
    return p * (1.0 / den)


def _penalty(slope, dist, valid):
    return jnp.where(valid, slope * dist.astype(_F32), jnp.inf)


def _attn_prompt_kernel(slopes_ref, sinks_ref, q_ref, kvc_ref, kvp_ref, zb0, zb1, zb2, zb3, o_ref,
                        pen_ref, sink_ref):
    blk = WINDOW
    gw = GQA_GROUP * HEAD_DIM
    nq = GQA_GROUP * blk
    n = pl.program_id(1)

    @pl.when((pl.program_id(0) == 0) & (n == 0))
    def _fill_tables():
        si = lax.broadcasted_iota(jnp.int32, (2 * blk, blk), 0)
        ti = lax.broadcasted_iota(jnp.int32, (2 * blk, blk), 1)
        dist = blk + ti - si
        for n_tbl in (0, 1):
            key_pos = (n_tbl - 1) * blk + si
            valid = (dist >= 0) & (dist < WINDOW) & (key_pos >= 0)
            for h in range(N_HEADS):
                kh, g = divmod(h, GQA_GROUP)
                pen_ref[n_tbl, kh, :, g * blk:(g + 1) * blk] = _penalty(slopes_ref[h], dist, valid)
        for h in range(N_HEADS):
            kh, g = divmod(h, GQA_GROUP)
            sink_ref[kh, :, g * blk:(g + 1) * blk] = jnp.full((1, blk), sinks_ref[h], _F32)

    tbl = jnp.minimum(n, 1)
    kv = jnp.concatenate([kvp_ref[...], kvc_ref[...]], axis=0)
    k_all = kv[:, 0:KV_WIDTH].astype(_BF16)
    v_t = kv[:, KV_WIDTH:2 * KV_WIDTH].T.astype(_BF16)
    zbs = (zb0, zb1, zb2, zb3)
    o_t = []
    for kh in range(N_KV_HEADS):
        hd = slice(kh * HEAD_DIM, (kh + 1) * HEAD_DIM)
        q_kh = (q_ref[:, kh * gw:(kh + 1) * gw] * (HEAD_DIM ** -0.5)).astype(_BF16)
        qs = jnp.concatenate([q_kh[:, g * HEAD_DIM:(g + 1) * HEAD_DIM] for g in range(GQA_GROUP)], axis=0)
        logits_t = lax.dot_general(k_all[:, hd], qs, (((1,), (1,)), ((), ())),
                                   preferred_element_type=_F32)
        l = logits_t - pen_ref[tbl, kh]
        sink = sink_ref[kh]
        m = jnp.maximum(jnp.max(l, axis=0, keepdims=True), sink)
        p = jnp.exp(l - m)
        den = jnp.sum(p, axis=0, keepdims=True) + jnp.exp(sink - m)
        pr_t = (p * (1.0 / den)).astype(_BF16)
        o_t.append(jnp.dot(v_t[hd, :], pr_t, preferred_element_type=_F32))
    for pair in range(N_KV_HEADS // 2):
        o_pair = jnp.concatenate([o_t[2 * pair], o_t[2 * pair + 1]], axis=0)
        tiles = [o_pair[:, g * blk:(g + 1) * blk].T for g in range(GQA_GROUP)]
        for j in range(2):
            kh = 2 * pair + j
            o_kh = jnp.concatenate([t[:, j * HEAD_DIM:(j + 1) * HEAD_DIM] for t in tiles], axis=1)
            zb = zbs[kh][...]
            o_ref[:, kh * gw:(kh + 1) * gw] = (zb * jax.nn.sigmoid(zb) * o_kh).astype(_BF16)


def _attn_prompt(proj, slopes, sinks, batch, seq):
    blk = WINDOW
    nb = seq // blk
    gw = GQA_GROUP * HEAD_DIM
    smem = pl.BlockSpec(memory_space=pltpu.SMEM)

    def zb_spec(kh):
        return pl.BlockSpec((blk, gw), lambda b, n: (b * nb + n, COL_ZB // gw + kh))

    return pl.pallas_call(
        _attn_prompt_kernel,
        grid=(batch, nb),
        in_specs=[
            smem, smem,
            pl.BlockSpec((blk, ATTN_WIDTH), lambda b, n: (b * nb + n, COL_Q // ATTN_WIDTH)),
            pl.BlockSpec((blk, 2 * KV_WIDTH), lambda b, n: (b * nb + n, COL_K // (2 * KV_WIDTH))),
            pl.BlockSpec((blk, 2 * KV_WIDTH), lambda b, n: (b * nb + jnp.maximum(n - 1, 0), COL_K // (2 * KV_WIDTH))),
            zb_spec(0), zb_spec(1), zb_spec(2), zb_spec(3),
        ],
        out_specs=pl.BlockSpec((blk, ATTN_WIDTH), lambda b, n: (b * nb + n, 0)),
        out_shape=jax.ShapeDtypeStruct((proj.shape[0], ATTN_WIDTH), _BF16),
        scratch_shapes=[pltpu.VMEM((2, N_KV_HEADS, 2 * blk, GQA_GROUP * blk), _F32),
                        pltpu.VMEM((N_KV_HEADS, 1, GQA_GROUP * blk), _F32)],
        compiler_params=_params(("arbitrary", "arbitrary"), 48),
        name="attn_prompt",
    )(slopes, sinks, proj, proj, proj, proj, proj, proj, proj)


def _attn_sample_kernel(slopes_ref, sinks_ref, q_ref, kvn_ref, ck_ref, cv_ref, zb0, zb1, zb2, zb3, b_in_ref,
                        o_ref, nk_ref, nv_ref, pen_ref, *, bs, t_new):
    del b_in_ref
    s_all = WINDOW + t_new
    gw = GQA_GROUP * HEAD_DIM

    @pl.when(pl.program_id(0) == 0)
    def _fill_penalty_table():
        ti = lax.broadcasted_iota(jnp.int32, (t_new, s_all), 0)
        si = lax.broadcasted_iota(jnp.int32, (t_new, s_all), 1)
        dist = WINDOW + ti - si
        key_pos = PAST_LEN - WINDOW + si
        valid = (dist >= 0) & (dist < WINDOW) & (key_pos >= 0)
        for h in range(N_HEADS):
            pen_ref[h] = _penalty(slopes_ref[h], dist, valid)

    kvn = kvn_ref[...]
    k_all = jnp.concatenate([ck_ref[...], kvn[:, :, 0:KV_WIDTH]], axis=1)
    v_all = jnp.concatenate([cv_ref[...], kvn[:, :, KV_WIDTH:2 * KV_WIDTH]], axis=1)
    nk_ref[...] = k_all[:, t_new:, :]
    nv_ref[...] = v_all[:, t_new:, :]
    kb = k_all.astype(_BF16)
    vb = v_all.astype(_BF16)
    zbs = (zb0, zb1, zb2, zb3)
    for kh in range(N_KV_HEADS):
        q_kh = (q_ref[:, :, kh * gw:(kh + 1) * gw] * (HEAD_DIM ** -0.5)).astype(_BF16)
        qs = jnp.concatenate([q_kh[:, :, g * HEAD_DIM:(g + 1) * HEAD_DIM] for g in range(GQA_GROUP)], axis=1)
        hd = slice(kh * HEAD_DIM, (kh + 1) * HEAD_DIM)
        logits = jnp.stack(
            [lax.dot_general(qs[b], kb[b, :, hd], (((1,), (1,)), ((), ())), preferred_element_type=_F32)
             for b in range(bs)], axis=0)
        probs = []
        for g in range(GQA_GROUP):
            h = kh * GQA_GROUP + g
            l = logits[:, g * t_new:(g + 1) * t_new, :] - pen_ref[h]
            probs.append(_softmax_with_sink(l, sinks_ref[h]).astype(_BF16))
        pr = jnp.concatenate(probs, axis=1)
        o = jnp.stack([jnp.dot(pr[b], vb[b, :, hd], preferred_element_type=_F32) for b in range(bs)], axis=0)
        o_kh = jnp.concatenate([o[:, g * t_new:(g + 1) * t_new, :] for g in range(GQA_GROUP)], axis=2)
        zb = zbs[kh][...]
        res = (zb * jax.nn.sigmoid(zb) * o_kh).reshape(bs * t_new, gw)
        o_ref[:, kh * gw:(kh + 1) * gw] = res.astype(_BF16)


def _attn_sample(proj3, cache_k, cache_v, slopes, sinks, b_all, seq0, bs=16):
    nseq = cache_k.shape[0]
    t_new = proj3.shape[1]
    blk0 = seq0 // bs
    gw = GQA_GROUP * HEAD_DIM
    smem = pl.BlockSpec(memory_space=pltpu.SMEM)
    kern = functools.partial(_attn_sample_kernel, bs=bs, t_new=t_new)

    def zb_spec(kh):
        return pl.BlockSpec((bs, t_new, gw), lambda i: (blk0 + i, 0, COL_ZB // gw + kh))

    cache_spec = pl.BlockSpec((bs, WINDOW, KV_WIDTH), lambda i: (i, 0, 0))
    return pl.pallas_call(
        kern,
        grid=(nseq // bs,),
        in_specs=[
            smem, smem,
            pl.BlockSpec((bs, t_new, ATTN_WIDTH), lambda i: (blk0 + i, 0, COL_Q // ATTN_WIDTH)),
            pl.BlockSpec((bs, t_new, 2 * KV_WIDTH), lambda i: (blk0 + i, 0, COL_K // (2 * KV_WIDTH))),
            cache_spec, cache_spec,
            zb_spec(0), zb_spec(1), zb_spec(2), zb_spec(3),
            pl.BlockSpec(memory_space=pl.ANY),
        ],
        out_specs=[
            pl.BlockSpec((bs * t_new, ATTN_WIDTH), lambda i: (blk0 + i, 0)),
            cache_spec, cache_spec,
        ],
        out_shape=[
            jax.ShapeDtypeStruct(b_all.shape, _BF16),
            jax.ShapeDtypeStruct((nseq, WINDOW, KV_WIDTH), _F32),
            jax.ShapeDtypeStruct((nseq, WINDOW, KV_WIDTH), _F32),
        ],
        input_output_aliases={10: 0},
        scratch_shapes=[pltpu.VMEM((N_HEADS, t_new, WINDOW + t_new), _F32)],
        compiler_params=_params(("arbitrary",), 48),
        name="attn_sample",
    )(slopes, sinks, proj3, proj3, cache_k, cache_v, proj3, proj3, proj3, proj3, b_all)


def _merge_kernel(a_ref, b_ref, wp_ref, wa_ref, ga_ref, gb_ref, h_ref, wpb_ref, wab_ref, *, chunk):
    @pl.when(pl.program_id(1) == 0)
    def _cast_weight_tiles():
        wpb_ref[...] = wp_ref[...].astype(_BF16)
        wab_ref[...] = wa_ref[...].astype(_BF16)

    for r in range(0, a_ref.shape[0], chunk):
        rows = pl.ds(r, chunk)
        ya = jnp.dot(a_ref[rows, :], wpb_ref[...], preferred_element_type=_F32)
        yb = jnp.dot(b_ref[rows, :], wab_ref[...], preferred_element_type=_F32)
        h = jax.nn.sigmoid(ga_ref[rows, :]) * ya + jax.nn.sigmoid(gb_ref[rows, :]) * yb
        h_ref[rows, :] = h.astype(_BF16)


def _merge(a, b, proj, w_out_pool, w_out_attn, tm=1024, tn=512, chunk=256):
    rows = a.shape[0]
    return pl.pallas_call(
        functools.partial(_merge_kernel, chunk=chunk),
        grid=(D_MODEL // tn, rows // tm),
        in_specs=[
            pl.BlockSpec((tm, POOL_WIDTH), lambda j, i: (i, 0)),
            pl.BlockSpec((tm, ATTN_WIDTH), lambda j, i: (i, 0)),
            pl.BlockSpec((POOL_WIDTH, tn), lambda j, i: (0, j)),
            pl.BlockSpec((ATTN_WIDTH, tn), lambda j, i: (0, j)),
            pl.BlockSpec((tm, tn), lambda j, i: (i, COL_GA // tn + j)),
            pl.BlockSpec((tm, tn), lambda j, i: (i, COL_GB // tn + j)),
        ],
        out_specs=pl.BlockSpec((tm, tn), lambda j, i: (i, j)),
        out_shape=jax.ShapeDtypeStruct((rows, D_MODEL), _BF16),
        scratch_shapes=[pltpu.VMEM((POOL_WIDTH, tn), _BF16), pltpu.VMEM((ATTN_WIDTH, tn), _BF16)],
        compiler_params=_params(("arbitrary", "arbitrary"), 56),
        name="merge",
    )(a, b, w_out_pool, w_out_attn, proj, proj)


def _out_kernel(h_ref, w_ref, x_ref, gain_ref, bias_ref, y_ref, *, sub):
    gain = gain_ref[...]
    bias = bias_ref[...]
    for r0 in range(0, h_ref.shape[0], sub):
        y_ref[pl.ds(r0, sub), :] = jnp.dot(h_ref[pl.ds(r0, sub), :], w_ref[...], preferred_element_type=_F32)
        for g in range(sub // SUBLANES):
            rows = pl.ds(r0 + g * SUBLANES, SUBLANES)
            y = DEEPNORM_ALPHA * x_ref[rows, :] + y_ref[rows, :]
            mu = jnp.mean(y, axis=-1, keepdims=True)
            d = y - mu
            var = jnp.mean(d * d, axis=-1, keepdims=True)
            y_ref[rows, :] = d * lax.rsqrt(var + LN_EPS) * gain + bias


def _out(h_all, row0, w_out_bf16, x, gain, bias, tm=256, sub=128):
    rows = x.shape[0]
    blk0 = row0 // tm
    return pl.pallas_call(
        functools.partial(_out_kernel, sub=sub),
        grid=(rows // tm,),
        in_specs=[
            pl.BlockSpec((tm, D_MODEL), lambda i: (blk0 + i, 0)),
            pl.BlockSpec((D_MODEL, D_MODEL), lambda i: (0, 0), pipeline_mode=pl.Buffered(1)),
            pl.BlockSpec((tm, D_MODEL), lambda i: (i, 0)),
            pl.BlockSpec((1, D_MODEL), lambda i: (0, 0)),
            pl.BlockSpec((1, D_MODEL), lambda i: (0, 0)),
        ],
        out_specs=pl.BlockSpec((tm, D_MODEL), lambda i: (i, 0)),
        out_shape=jax.ShapeDtypeStruct((rows, D_MODEL), _F32),
        compiler_params=_params(("arbitrary",), 60),
        name="out",
    )(h_all, w_out_bf16, x, gain, bias)


def _alibi_slopes():
    h = jnp.arange(N_HEADS, dtype=_F32)
    return 2.0 ** (-8.0 * (h + 1.0) / N_HEADS)


def kernel(x_prompt, x_sample, cache_k, cache_v, state_pool, w_in, w_pool, pool_scale, sinks,
           w_out_pool, w_out_attn, w_out, ln_gain, ln_bias):
    depth = w_in.shape[0]
    assert depth == 1
    bp, tp, _ = x_prompt.shape
    bs, ts, _ = x_sample.shape
    slopes = _alibi_slopes()

    l = 0
    w_pool_b = w_pool[l].astype(_BF16)
    w_o_b = w_out[l].astype(_BF16)
    scale = pool_scale[l].reshape(1, POOL_WIDTH)
    gain = ln_gain[l].reshape(1, D_MODEL)
    bias = ln_bias[l].reshape(1, D_MODEL)
    sink = sinks[l]

    xp2 = x_prompt.reshape(bp * tp, D_MODEL)
    xs2 = x_sample.reshape(bs * ts, D_MODEL)
    rows_p = bp * tp
    rows_all = rows_p + bs * ts
    x_all, proj = _proj_cast(xp2, w_in[l], 0, rows_all)
    x_all, proj = _proj_cast(xs2, w_in[l], rows_p, rows_all, x_all, proj)
    proj = _proj_rest(x_all, w_in[l], proj)
    proj3 = proj.reshape(proj.shape[0] // ts, ts, N_IN)

    a_all = _pool_prompt(proj, w_pool_b, scale, bp, tp)
    state16 = jnp.pad(state_pool[l], ((0, 0), (HALO_ROWS - POOL_STATE, 0), (0, 0)))
    a_all, new_pool_s = _pool_sample(proj3, state16, w_pool_b, scale, a_all, rows_p // ts)

    b_all = _attn_prompt(proj, slopes, sink, bp, tp)
    ck = cache_k[l].reshape(bs, WINDOW, KV_WIDTH)
    cv = cache_v[l].reshape(bs, WINDOW, KV_WIDTH)
    b_all, new_k_s, new_v_s = _attn_sample(proj3, ck, cv, slopes, sink, b_all, rows_p // ts)

    h_all = _merge(a_all, b_all, proj, w_out_pool[l], w_out_attn[l])
    y_p = _out(h_all, 0, w_o_b, xp2, gain, bias).reshape(bp, tp, D_MODEL)
    y_s = _out(h_all, rows_p, w_o_b, xs2, gain, bias).reshape(bs, ts, D_MODEL)

    def tail(n_rows, col0, width):
        return jnp.stack([lax.slice(proj, ((b + 1) * tp - n_rows, col0), ((b + 1) * tp, col0 + width))
                          for b in range(bp)])

    new_k_p = tail(WINDOW, COL_K, KV_WIDTH).reshape(1, bp, WINDOW, N_KV_HEADS, HEAD_DIM)
    new_v_p = tail(WINDOW, COL_V, KV_WIDTH).reshape(1, bp, WINDOW, N_KV_HEADS, HEAD_DIM)
    new_pool_p = tail(POOL_STATE, COL_U, POOL_WIDTH)[None]
    new_k_s = new_k_s.reshape(1, bs, WINDOW, N_KV_HEADS, HEAD_DIM)
    new_v_s = new_v_s.reshape(1, bs, WINDOW, N_KV_HEADS, HEAD_DIM)

    return (y_p, y_s, new_k_p, new_v_p, new_pool_p, new_k_s, new_v_s, new_pool_s[None])
```

```python
import functools

import jax
import jax.numpy as jnp
from jax import lax
from jax.experimental import pallas as pl
from jax.experimental.pallas import tpu as pltpu

D_MODEL = 4096
POOL_WIDTH = 2048
POOL_WINDOWS = (2, 4, 8, 16)
POOL_GROUP_WIDTH = 512
POOL_STATE = 15
HEAD_DIM = 64
N_HEADS = 32
N_KV_HEADS = 4
GQA_GROUP = 8
ATTN_WIDTH = 2048
KV_WIDTH = 256
WINDOW = 128
PAST_LEN = 8192
N_IN = 16896
COL_U, COL_ZA, COL_Q, COL_K, COL_V, COL_ZB, COL_GA, COL_GB = 0, 2048, 4096, 6144, 6400, 6656, 8704, 12800
DEEPNORM_ALPHA = 2.0 ** 0.25
LN_EPS = 1e-5

V7X_VMEM_BYTES = 64 * 1024 * 1024
SUBLANES = 8
HALO_ROWS = 16
MXU_WIDTH = 256
PROJ_TN = 3 * MXU_WIDTH
PROJ_MAIN_TN = 7 * MXU_WIDTH
MERGE_TN = 4 * MXU_WIDTH
STREAM_STEPS = 16

_F32 = jnp.float32
_BF16 = jnp.bfloat16


def _params(sem, vmem_mib):
    return pltpu.CompilerParams(dimension_semantics=sem, vmem_limit_bytes=vmem_mib * 1024 * 1024)


def _stream_cast_steps(n_tiles, steps_per_pass):
    last = n_tiles - 1

    def row_blk(j, i):
        return jnp.where(j > 0, i, 0)

    def col_blk(j, i):
        return jnp.maximum(j - 1, 0)

    def chunk_blk(j, i):
        return jnp.where(j <= last, i, steps_per_pass - 1)

    def stage_blk(j, i):
        return jnp.minimum(j, last)

    return row_blk, col_blk, chunk_blk, stage_blk


def _proj_main_kernel(x_ref, wc_ref, proj_in_ref, o_ref, wb0_ref, wb1_ref):
    del proj_in_ref
    j = pl.program_id(0)
    chunk = wc_ref.shape[0]
    rows = pl.ds(pl.multiple_of(pl.program_id(1) * chunk, chunk), chunk)

    def step(w_next_ref, w_cur_ref):
        w_next_ref[rows, :] = wc_ref[...].astype(_BF16)
        if w_cur_ref is not None:
            o_ref[...] = jnp.dot(x_ref[...], w_cur_ref[...], preferred_element_type=_F32)

    @pl.when(j == 0)
    def _stage_only():
        step(wb0_ref, None)

    @pl.when((j > 0) & (j % 2 == 0))
    def _even():
        step(wb0_ref, wb1_ref)

    @pl.when(j % 2 == 1)
    def _odd():
        step(wb1_ref, wb0_ref)


def _proj_main(x_bf16, w_f32, proj, steps=STREAM_STEPS, tn=PROJ_MAIN_TN):
    rows, k = x_bf16.shape
    n = w_f32.shape[1]
    n_tiles = (n - PROJ_TN) // tn
    assert n_tiles * tn + PROJ_TN == n and rows % steps == 0 and k % steps == 0
    tm = rows // steps
    chunk = k // steps
    row_blk, col_blk, chunk_blk, stage_blk = _stream_cast_steps(n_tiles, steps)
    return pl.pallas_call(
        _proj_main_kernel,
        grid=(n_tiles + 1, steps),
        in_specs=[pl.BlockSpec((tm, k), lambda j, i: (row_blk(j, i), 0)),
                  pl.BlockSpec((chunk, tn), lambda j, i: (chunk_blk(j, i), stage_blk(j, i))),
                  pl.BlockSpec(memory_space=pl.ANY)],
        out_specs=pl.BlockSpec((tm, tn), lambda j, i: (row_blk(j, i), col_blk(j, i))),
        out_shape=jax.ShapeDtypeStruct((rows, n), _F32),
        input_output_aliases={2: 0},
        scratch_shapes=[pltpu.VMEM((k, tn), _BF16), pltpu.VMEM((k, tn), _BF16)],
        compiler_params=_params(("arbitrary", "arbitrary"), 58),
        name="proj",
    )(x_bf16, w_f32, proj)


def _proj_cast_kernel(x_ref, w_ref, *refs):
    xb_ref, o_ref, wb_ref = refs[-3:]

    @pl.when(pl.program_id(0) == 0)
    def _cast_weight_tile():
        wb_ref[...] = w_ref[...].astype(_BF16)

    xb = x_ref[...].astype(_BF16)
    xb_ref[...] = xb
    o_ref[...] = jnp.dot(xb, wb_ref[...], preferred_element_type=_F32)


def _proj_cast(x_f32, w_f32, row0, total_rows, x_all=None, proj=None, tm=512, tn=PROJ_TN):
    rows, k = x_f32.shape
    n = w_f32.shape[1]
    blk0 = row0 // tm
    col_blk = n // tn - 1
    aliased = [] if x_all is None else [x_all, proj]
    return pl.pallas_call(
        _proj_cast_kernel,
        grid=(rows // tm,),
        in_specs=[pl.BlockSpec((tm, k), lambda i: (i, 0)),
                  pl.BlockSpec((k, tn), lambda i: (0, col_blk), pipeline_mode=pl.Buffered(1))]
                 + [pl.BlockSpec(memory_space=pl.ANY)] * len(aliased),
        out_specs=[pl.BlockSpec((tm, k), lambda i: (blk0 + i, 0)),
                   pl.BlockSpec((tm, tn), lambda i: (blk0 + i, col_blk))],
        out_shape=[jax.ShapeDtypeStruct((total_rows, k), _BF16),
                   jax.ShapeDtypeStruct((total_rows, n), _F32)],
        input_output_aliases={2: 0, 3: 1} if aliased else {},
        scratch_shapes=[pltpu.VMEM((k, tn), _BF16)],
        compiler_params=_params(("arbitrary",), 56),
        name="proj_cast",
    )(x_f32, w_f32, *aliased)


def _window_sums(ext, n_doublings):
    s = ext
    for d in range(n_doublings):
        s = s + pltpu.roll(s, 1 << d, axis=0)
    return s


def _pool_groups(ext, u, z_a, pos, wp_ref, scale_ref, halo):
    outs = []
    for g, w in enumerate(POOL_WINDOWS):
        sl = slice(g * POOL_GROUP_WIDTH, (g + 1) * POOL_GROUP_WIDTH)
        s = _window_sums(ext[:, sl], g + 1)[halo:]
        count = jnp.minimum(w, pos + 1).astype(_F32)
        pooled = s / count - u[:, sl]
        mixed = jnp.dot(pooled.astype(_BF16), wp_ref[g], preferred_element_type=_F32)
        pool_out = mixed * scale_ref[:, sl]
        z = z_a[:, sl]
        outs.append(z * jax.nn.sigmoid(z) * pool_out)
    return outs


def _pool_prompt_kernel(u_ref, halo_ref, za_ref, wp_ref, scale_ref, a_ref, *, tiles_per_seq, tm):
    t = pl.program_id(0) % tiles_per_seq
    u = u_ref[...]
    halo = jnp.where(t == 0, 0.0, halo_ref[...])
    ext = jnp.concatenate([halo, u], axis=0)
    pos = t * tm + lax.broadcasted_iota(jnp.int32, (tm, 1), 0)
    outs = _pool_groups(ext, u, za_ref[...], pos, wp_ref, scale_ref, HALO_ROWS)
    for g, o in enumerate(outs):
        a_ref[:, g * POOL_GROUP_WIDTH:(g + 1) * POOL_GROUP_WIDTH] = o.astype(_BF16)


def _pool_prompt(proj, w_pool_bf16, pool_scale, batch, seq, tm=256):
    rows = proj.shape[0]
    tiles_per_seq = seq // tm
    halo_blocks = tm // HALO_ROWS
    kern = functools.partial(_pool_prompt_kernel, tiles_per_seq=tiles_per_seq, tm=tm)
    return pl.pallas_call(
        kern,
        grid=(batch * seq // tm,),
        in_specs=[
            pl.BlockSpec((tm, POOL_WIDTH), lambda i: (i, COL_U // POOL_WIDTH)),
            pl.BlockSpec((HALO_ROWS, POOL_WIDTH), lambda i: (jnp.maximum(i * halo_blocks - 1, 0), 0)),
            pl.BlockSpec((tm, POOL_WIDTH), lambda i: (i, COL_ZA // POOL_WIDTH)),
            pl.BlockSpec((4, POOL_GROUP_WIDTH, POOL_GROUP_WIDTH), lambda i: (0, 0, 0)),
            pl.BlockSpec((1, POOL_WIDTH), lambda i: (0, 0)),
        ],
        out_specs=pl.BlockSpec((tm, POOL_WIDTH), lambda i: (i, 0)),
        out_shape=jax.ShapeDtypeStruct((rows, POOL_WIDTH), _BF16),
        compiler_params=_params(("arbitrary",), 48),
        name="pool_prompt",
    )(proj, proj, proj, w_pool_bf16, pool_scale)


def _pool_sample_kernel(u_ref, za_ref, st_ref, wp_ref, scale_ref, a_in_ref, a_ref, np_ref, *, bs, t_new):
    del a_in_ref
    u3 = u_ref[...]
    st = st_ref[...]
    rows_per_seq = HALO_ROWS + t_new
    ext = jnp.concatenate([st, u3], axis=1).reshape(bs * rows_per_seq, POOL_WIDTH)
    pos3 = PAST_LEN + lax.broadcasted_iota(jnp.int32, (bs, t_new, 1), 1)
    pos = pos3.reshape(bs * t_new, 1)
    u = u3.reshape(bs * t_new, POOL_WIDTH)
    za = za_ref[...].reshape(bs * t_new, POOL_WIDTH)
    for g, w in enumerate(POOL_WINDOWS):
        sl = slice(g * POOL_GROUP_WIDTH, (g + 1) * POOL_GROUP_WIDTH)
        s = _window_sums(ext[:, sl], g + 1)
        s = s.reshape(bs, rows_per_seq, POOL_GROUP_WIDTH)[:, HALO_ROWS:, :].reshape(bs * t_new, POOL_GROUP_WIDTH)
        count = jnp.minimum(w, pos + 1).astype(_F32)
        pooled = s / count - u[:, sl]
        mixed = jnp.dot(pooled.astype(_BF16), wp_ref[g], preferred_element_type=_F32)
        pool_out = mixed * scale_ref[:, sl]
        z = za[:, sl]
        a_ref[:, sl] = (z * jax.nn.sigmoid(z) * pool_out).astype(_BF16)
    keep = POOL_STATE - t_new
    np_ref[:, 0:keep, :] = st[:, HALO_ROWS - keep:, :]
    np_ref[:, keep:, :] = u3


def _pool_sample(proj3, state16, w_pool_bf16, pool_scale, a_all, seq0, bs=16):
    nseq = state16.shape[0]
    t_new = proj3.shape[1]
    blk0 = seq0 // bs
    kern = functools.partial(_pool_sample_kernel, bs=bs, t_new=t_new)
    return pl.pallas_call(
        kern,
        grid=(nseq // bs,),
        in_specs=[
            pl.BlockSpec((bs, t_new, POOL_WIDTH), lambda i: (blk0 + i, 0, COL_U // POOL_WIDTH)),
            pl.BlockSpec((bs, t_new, POOL_WIDTH), lambda i: (blk0 + i, 0, COL_ZA // POOL_WIDTH)),
            pl.BlockSpec((bs, HALO_ROWS, POOL_WIDTH), lambda i: (i, 0, 0)),
            pl.BlockSpec((4, POOL_GROUP_WIDTH, POOL_GROUP_WIDTH), lambda i: (0, 0, 0)),
            pl.BlockSpec((1, POOL_WIDTH), lambda i: (0, 0)),
            pl.BlockSpec(memory_space=pl.ANY),
        ],
        out_specs=[
            pl.BlockSpec((bs * t_new, POOL_WIDTH), lambda i: (blk0 + i, 0)),
            pl.BlockSpec((bs, POOL_STATE, POOL_WIDTH), lambda i: (i, 0, 0)),
        ],
        out_shape=[
            jax.ShapeDtypeStruct(a_all.shape, _BF16),
            jax.ShapeDtypeStruct((nseq, POOL_STATE, POOL_WIDTH), _F32),
        ],
        input_output_aliases={5: 0},
        compiler_params=_params(("arbitrary",), 48),
        name="pool_sample",
    )(proj3, proj3, state16, w_pool_bf16, pool_scale, a_all)


def _softmax_with_sink(l, sink):
    m = jnp.maximum(jnp.max(l, axis=-1, keepdims=True), sink)
    p = jnp.exp(l - m)
    den = jnp.sum(p, axis=-1, keepdims=True) + jnp.exp(sink - m)
    return p * (1.0 / den)


def _penalty(slope, dist, valid):
    return jnp.where(valid, slope * dist.astype(_F32), jnp.inf)


def _attn_prompt_kernel(slopes_ref, sinks_ref, q_ref, kvc_ref, kvp_ref, zb0, zb1, zb2, zb3, o_ref,
                        pen_ref, sink_ref):
    blk = WINDOW
    gw = GQA_GROUP * HEAD_DIM
    nq = GQA_GROUP * blk
    n = pl.program_id(1)

    @pl.when((pl.program_id(0) == 0) & (n == 0))
    def _fill_tables():
        si = lax.broadcasted_iota(jnp.int32, (2 * blk, blk), 0)
        ti = lax.broadcasted_iota(jnp.int32, (2 * blk, blk), 1)
        dist = blk + ti - si
        for n_tbl in (0, 1):
            key_pos = (n_tbl - 1) * blk + si
            valid = (dist >= 0) & (dist < WINDOW) & (key_pos >= 0)
            for h in range(N_HEADS):
                kh, g = divmod(h, GQA_GROUP)
                pen_ref[n_tbl, kh, :, g * blk:(g + 1) * blk] = _penalty(slopes_ref[h], dist, valid)
        for h in range(N_HEADS):
            kh, g = divmod(h, GQA_GROUP)
            sink_ref[kh, :, g * blk:(g + 1) * blk] = jnp.full((1, blk), sinks_ref[h], _F32)

    tbl = jnp.minimum(n, 1)
    kv = jnp.concatenate([kvp_ref[...], kvc_ref[...]], axis=0)
    k_all = kv[:, 0:KV_WIDTH].astype(_BF16)
    v_t = kv[:, KV_WIDTH:2 * KV_WIDTH].T.astype(_BF16)
    zbs = (zb0, zb1, zb2, zb3)
    o_t = []
    for kh in range(N_KV_HEADS):
        hd = slice(kh * HEAD_DIM, (kh + 1) * HEAD_DIM)
        q_kh = (q_ref[:, kh * gw:(kh + 1) * gw] * (HEAD_DIM ** -0.5)).astype(_BF16)
        qs = jnp.concatenate([q_kh[:, g * HEAD_DIM:(g + 1) * HEAD_DIM] for g in range(GQA_GROUP)], axis=0)
        logits_t = lax.dot_general(k_all[:, hd], qs, (((1,), (1,)), ((), ())),
                                   preferred_element_type=_F32)
        l = logits_t - pen_ref[tbl, kh]
        sink = sink_ref[kh]
        m = jnp.maximum(jnp.max(l, axis=0, keepdims=True), sink)
        p = jnp.exp(l - m)
        den = jnp.sum(p, axis=0, keepdims=True) + jnp.exp(sink - m)
        pr_t = (p * (1.0 / den)).astype(_BF16)
        o_t.append(jnp.dot(v_t[hd, :], pr_t, preferred_element_type=_F32))
    for pair in range(N_KV_HEADS // 2):
        o_pair = jnp.concatenate([o_t[2 * pair], o_t[2 * pair + 1]], axis=0)
        tiles = [o_pair[:, g * blk:(g + 1) * blk].T for g in range(GQA_GROUP)]
        for j in range(2):
            kh = 2 * pair + j
            o_kh = jnp.concatenate([t[:, j * HEAD_DIM:(j + 1) * HEAD_DIM] for t in tiles], axis=1)
            zb = zbs[kh][...]
            o_ref[:, kh * gw:(kh + 1) * gw] = (zb * jax.nn.sigmoid(zb) * o_kh).astype(_BF16)


def _attn_prompt(proj, slopes, sinks, batch, seq):
    blk = WINDOW
    nb = seq // blk
    gw = GQA_GROUP * HEAD_DIM
    smem = pl.BlockSpec(memory_space=pltpu.SMEM)

    def zb_spec(kh):
        return pl.BlockSpec((blk, gw), lambda b, n: (b * nb + n, COL_ZB // gw + kh))

    return pl.pallas_call(
        _attn_prompt_kernel,
        grid=(batch, nb),
        in_specs=[
            smem, smem,
            pl.BlockSpec((blk, ATTN_WIDTH), lambda b, n: (b * nb + n, COL_Q // ATTN_WIDTH)),
            pl.BlockSpec((blk, 2 * KV_WIDTH), lambda b, n: (b * nb + n, COL_K // (2 * KV_WIDTH))),
            pl.BlockSpec((blk, 2 * KV_WIDTH), lambda b, n: (b * nb + jnp.maximum(n - 1, 0), COL_K // (2 * KV_WIDTH))),
            zb_spec(0), zb_spec(1), zb_spec(2), zb_spec(3),
        ],
        out_specs=pl.BlockSpec((blk, ATTN_WIDTH), lambda b, n: (b * nb + n, 0)),
        out_shape=jax.ShapeDtypeStruct((proj.shape[0], ATTN_WIDTH), _BF16),
        scratch_shapes=[pltpu.VMEM((2, N_KV_HEADS, 2 * blk, GQA_GROUP * blk), _F32),
                        pltpu.VMEM((N_KV_HEADS, 1, GQA_GROUP * blk), _F32)],
        compiler_params=_params(("arbitrary", "arbitrary"), 48),
        name="attn_prompt",
    )(slopes, sinks, proj, proj, proj, proj, proj, proj, proj)


def _attn_sample_kernel(slopes_ref, sinks_ref, q_ref, kvn_ref, ck_ref, cv_ref, zb0, zb1, zb2, zb3, b_in_ref,
                        o_ref, nk_ref, nv_ref, pen_ref, *, bs, t_new):
    del b_in_ref
    s_all = WINDOW + t_new
    gw = GQA_GROUP * HEAD_DIM

    @pl.when(pl.program_id(0) == 0)
    def _fill_penalty_table():
        ti = lax.broadcasted_iota(jnp.int32, (t_new, s_all), 0)
        si = lax.broadcasted_iota(jnp.int32, (t_new, s_all), 1)
        dist = WINDOW + ti - si
        key_pos = PAST_LEN - WINDOW + si
        valid = (dist >= 0) & (dist < WINDOW) & (key_pos >= 0)
        for h in range(N_HEADS):
            pen_ref[h] = _penalty(slopes_ref[h], dist, valid)

    kvn = kvn_ref[...]
    k_all = jnp.concatenate([ck_ref[...], kvn[:, :, 0:KV_WIDTH]], axis=1)
    v_all = jnp.concatenate([cv_ref[...], kvn[:, :, KV_WIDTH:2 * KV_WIDTH]], axis=1)
    nk_ref[...] = k_all[:, t_new:, :]
    nv_ref[...] = v_all[:, t_new:, :]
    kb = k_all.astype(_BF16)
    vb = v_all.astype(_BF16)
    zbs = (zb0, zb1, zb2, zb3)
    for kh in range(N_KV_HEADS):
        q_kh = (q_ref[:, :, kh * gw:(kh + 1) * gw] * (HEAD_DIM ** -0.5)).astype(_BF16)
        qs = jnp.concatenate([q_kh[:, :, g * HEAD_DIM:(g + 1) * HEAD_DIM] for g in range(GQA_GROUP)], axis=1)
        hd = slice(kh * HEAD_DIM, (kh + 1) * HEAD_DIM)
        logits = jnp.stack(
            [lax.dot_general(qs[b], kb[b, :, hd], (((1,), (1,)), ((), ())), preferred_element_type=_F32)
             for b in range(bs)], axis=0)
        probs = []
        for g in range(GQA_GROUP):
            h = kh * GQA_GROUP + g
            l = logits[:, g * t_new:(g + 1) * t_new, :] - pen_ref[h]
            probs.append(_softmax_with_sink(l, sinks_ref[h]).astype(_BF16))
        pr = jnp.concatenate(probs, axis=1)
        o = jnp.stack([jnp.dot(pr[b], vb[b, :, hd], preferred_element_type=_F32) for b in range(bs)], axis=0)
        o_kh = jnp.concatenate([o[:, g * t_new:(g + 1) * t_new, :] for g in range(GQA_GROUP)], axis=2)
        zb = zbs[kh][...]
        res = (zb * jax.nn.sigmoid(zb) * o_kh).reshape(bs * t_new, gw)
        o_ref[:, kh * gw:(kh + 1) * gw] = res.astype(_BF16)


def _attn_sample(proj3, cache_k, cache_v, slopes, sinks, b_all, seq0, bs=16):
    nseq = cache_k.shape[0]
    t_new = proj3.shape[1]
    blk0 = seq0 // bs
    gw = GQA_GROUP * HEAD_DIM
    smem = pl.BlockSpec(memory_space=pltpu.SMEM)
    kern = functools.partial(_attn_sample_kernel, bs=bs, t_new=t_new)

    def zb_spec(kh):
        return pl.BlockSpec((bs, t_new, gw), lambda i: (blk0 + i, 0, COL_ZB // gw + kh))

    cache_spec = pl.BlockSpec((bs, WINDOW, KV_WIDTH), lambda i: (i, 0, 0))
    return pl.pallas_call(
        kern,
        grid=(nseq // bs,),
        in_specs=[
            smem, smem,
            pl.BlockSpec((bs, t_new, ATTN_WIDTH), lambda i: (blk0 + i, 0, COL_Q // ATTN_WIDTH)),
            pl.BlockSpec((bs, t_new, 2 * KV_WIDTH), lambda i: (blk0 + i, 0, COL_K // (2 * KV_WIDTH))),
            cache_spec, cache_spec,
            zb_spec(0), zb_spec(1), zb_spec(2), zb_spec(3),
            pl.BlockSpec(memory_space=pl.ANY),
        ],
        out_specs=[
            pl.BlockSpec((bs * t_new, ATTN_WIDTH), lambda i: (blk0 + i, 0)),
            cache_spec, cache_spec,
        ],
        out_shape=[
            jax.ShapeDtypeStruct(b_all.shape, _BF16),
            jax.ShapeDtypeStruct((nseq, WINDOW, KV_WIDTH), _F32),
            jax.ShapeDtypeStruct((nseq, WINDOW, KV_WIDTH), _F32),
        ],
        input_output_aliases={10: 0},
        scratch_shapes=[pltpu.VMEM((N_HEADS, t_new, WINDOW + t_new), _F32)],
        compiler_params=_params(("arbitrary",), 48),
        name="attn_sample",
    )(slopes, sinks, proj3, proj3, cache_k, cache_v, proj3, proj3, proj3, proj3, b_all)


def _merge_kernel(a_ref, b_ref, wpc_ref, wac_ref, ga0_ref, ga1_ref, gb0_ref, gb1_ref, h_ref,
                  wp0_ref, wp1_ref, wa0_ref, wa1_ref, *, row_chunk):
    j = pl.program_id(0)
    chunk = wpc_ref.shape[0]
    wrows = pl.ds(pl.multiple_of(pl.program_id(1) * chunk, chunk), chunk)
    half = ga0_ref.shape[1]

    def step(wp_next_ref, wa_next_ref, wp_cur_ref, wa_cur_ref):
        wp_next_ref[wrows, :] = wpc_ref[...].astype(_BF16)
        wa_next_ref[wrows, :] = wac_ref[...].astype(_BF16)
        if wp_cur_ref is None:
            return
        for r in range(0, a_ref.shape[0], row_chunk):
            rows = pl.ds(r, row_chunk)
            ya = jnp.dot(a_ref[rows, :], wp_cur_ref[...], preferred_element_type=_F32)
            yb = jnp.dot(b_ref[rows, :], wa_cur_ref[...], preferred_element_type=_F32)
            for c, (ga_ref, gb_ref) in enumerate(((ga0_ref, gb0_ref), (ga1_ref, gb1_ref))):
                cols = slice(c * half, (c + 1) * half)
                h = jax.nn.sigmoid(ga_ref[rows, :]) * ya[:, cols] + jax.nn.sigmoid(gb_ref[rows, :]) * yb[:, cols]
                h_ref[rows, cols] = h.astype(_BF16)

    @pl.when(j == 0)
    def _stage_only():
        step(wp0_ref, wa0_ref, None, None)

    @pl.when((j > 0) & (j % 2 == 0))
    def _even():
        step(wp0_ref, wa0_ref, wp1_ref, wa1_ref)

    @pl.when(j % 2 == 1)
    def _odd():
        step(wp1_ref, wa1_ref, wp0_ref, wa0_ref)


def _merge(a, b, proj, w_out_pool, w_out_attn, steps=STREAM_STEPS, tn=MERGE_TN):
    rows = a.shape[0]
    n_tiles = D_MODEL // tn
    assert rows % steps == 0 and POOL_WIDTH % steps == 0 and ATTN_WIDTH == POOL_WIDTH
    tm = rows // steps
    chunk = POOL_WIDTH // steps
    half = tn // 2
    row_blk, col_blk, chunk_blk, stage_blk = _stream_cast_steps(n_tiles, steps)

    def gate_spec(col0, c):
        return pl.BlockSpec((tm, half), lambda j, i: (row_blk(j, i), col0 // half + 2 * col_blk(j, i) + c))

    return pl.pallas_call(
        functools.partial(_merge_kernel, row_chunk=tm // 2),
        grid=(n_tiles + 1, steps),
        in_specs=[
            pl.BlockSpec((tm, POOL_WIDTH), lambda j, i: (row_blk(j, i), 0)),
            pl.BlockSpec((tm, ATTN_WIDTH), lambda j, i: (row_blk(j, i), 0)),
            pl.BlockSpec((chunk, tn), lambda j, i: (chunk_blk(j, i), stage_blk(j, i))),
            pl.BlockSpec((chunk, tn), lambda j, i: (chunk_blk(j, i), stage_blk(j, i))),
            gate_spec(COL_GA, 0), gate_spec(COL_GA, 1), gate_spec(COL_GB, 0), gate_spec(COL_GB, 1),
        ],
        out_specs=pl.BlockSpec((tm, tn), lambda j, i: (row_blk(j, i), col_blk(j, i))),
        out_shape=jax.ShapeDtypeStruct((rows, D_MODEL), _BF16),
        scratch_shapes=[pltpu.VMEM((POOL_WIDTH, tn), _BF16)] * 2 + [pltpu.VMEM((ATTN_WIDTH, tn), _BF16)] * 2,
        compiler_params=_params(("arbitrary", "arbitrary"), 56),
        name="merge",
    )(a, b, w_out_pool, w_out_attn, proj, proj, proj, proj)


def _out_kernel(h_ref, w_ref, x_ref, gain_ref, bias_ref, y_ref, *, sub):
    gain = gain_ref[...]
    bias = bias_ref[...]
    for r0 in range(0, h_ref.shape[0], sub):
        y_ref[pl.ds(r0, sub), :] = jnp.dot(h_ref[pl.ds(r0, sub), :], w_ref[...], preferred_element_type=_F32)
        for g in range(sub // SUBLANES):
            rows = pl.ds(r0 + g * SUBLANES, SUBLANES)
            y = DEEPNORM_ALPHA * x_ref[rows, :] + y_ref[rows, :]
            mu = jnp.mean(y, axis=-1, keepdims=True)
            d = y - mu
            var = jnp.mean(d * d, axis=-1, keepdims=True)
            y_ref[rows, :] = d * lax.rsqrt(var + LN_EPS) * gain + bias


def _out(h_all, row0, w_out_bf16, x, gain, bias, tm=256, sub=128):
    rows = x.shape[0]
    blk0 = row0 // tm
    return pl.pallas_call(
        functools.partial(_out_kernel, sub=sub),
        grid=(rows // tm,),
        in_specs=[
            pl.BlockSpec((tm, D_MODEL), lambda i: (blk0 + i, 0)),
            pl.BlockSpec((D_MODEL, D_MODEL), lambda i: (0, 0), pipeline_mode=pl.Buffered(1)),
            pl.BlockSpec((tm, D_MODEL), lambda i: (i, 0)),
            pl.BlockSpec((1, D_MODEL), lambda i: (0, 0)),
            pl.BlockSpec((1, D_MODEL), lambda i: (0, 0)),
        ],
        out_specs=pl.BlockSpec((tm, D_MODEL), lambda i: (i, 0)),
        out_shape=jax.ShapeDtypeStruct((rows, D_MODEL), _F32),
        compiler_params=_params(("arbitrary",), 60),
        name="out",
    )(h_all, w_out_bf16, x, gain, bias)


def _alibi_slopes():
    h = jnp.arange(N_HEADS, dtype=_F32)
    return 2.0 ** (-8.0 * (h + 1.0) / N_HEADS)


def kernel(x_prompt, x_sample, cache_k, cache_v, state_pool, w_in, w_pool, pool_scale, sinks,
           w_out_pool, w_out_attn, w_out, ln_gain, ln_bias):
    depth = w_in.shape[0]
    assert depth == 1
    bp, tp, _ = x_prompt.shape
    bs, ts, _ = x_sample.shape
    slopes = _alibi_slopes()

    l = 0
    w_pool_b = w_pool[l].astype(_BF16)
    w_o_b = w_out[l].astype(_BF16)
    scale = pool_scale[l].reshape(1, POOL_WIDTH)
    gain = ln_gain[l].reshape(1, D_MODEL)
    bias = ln_bias[l].reshape(1, D_MODEL)
    sink = sinks[l]

    xp2 = x_prompt.reshape(bp * tp, D_MODEL)
    xs2 = x_sample.reshape(bs * ts, D_MODEL)
    rows_p = bp * tp
    rows_all = rows_p + bs * ts
    x_all, proj = _proj_cast(xp2, w_in[l], 0, rows_all)
    x_all, proj = _proj_cast(xs2, w_in[l], rows_p, rows_all, x_all, proj)
    proj = _proj_main(x_all, w_in[l], proj)
    proj3 = proj.reshape(proj.shape[0] // ts, ts, N_IN)

    a_all = _pool_prompt(proj, w_pool_b, scale, bp, tp)
    state16 = jnp.pad(state_pool[l], ((0, 0), (HALO_ROWS - POOL_STATE, 0), (0, 0)))
    a_all, new_pool_s = _pool_sample(proj3, state16, w_pool_b, scale, a_all, rows_p // ts)

    b_all = _attn_prompt(proj, slopes, sink, bp, tp)
    ck = cache_k[l].reshape(bs, WINDOW, KV_WIDTH)
    cv = cache_v[l].reshape(bs, WINDOW, KV_WIDTH)
    b_all, new_k_s, new_v_s = _attn_sample(proj3, ck, cv, slopes, sink, b_all, rows_p // ts)

    h_all = _merge(a_all, b_all, proj, w_out_pool[l], w_out_attn[l])
    y_p = _out(h_all, 0, w_o_b, xp2, gain, bias).reshape(bp, tp, D_MODEL)
    y_s = _out(h_all, rows_p, w_o_b, xs2, gain, bias).reshape(bs, ts, D_MODEL)

    def tail(n_rows, col0, width):
        return jnp.stack([lax.slice(proj, ((b + 1) * tp - n_rows, col0), ((b + 1) * tp, col0 + width))
                          for b in range(bp)])

    new_k_p = tail(WINDOW, COL_K, KV_WIDTH).reshape(1, bp, WINDOW, N_KV_HEADS, HEAD_DIM)
    new_v_p = tail(WINDOW, COL_V, KV_WIDTH).reshape(1, bp, WINDOW, N_KV_HEADS, HEAD_DIM)
    new_pool_p = tail(POOL_STATE, COL_U, POOL_WIDTH)[None]
    new_k_s = new_k_s.reshape(1, bs, WINDOW, N_KV_HEADS, HEAD_DIM)
    new_v_s = new_v_s.reshape(1, bs, WINDOW, N_KV_HEADS, HEAD_DIM)

    return (y_p, y_s, new_k_p, new_v_p, new_pool_p, new_k_s, new_v_s, new_pool_s[None])
```

```python
import functools

import jax
import jax.numpy as jnp
from jax import lax
from jax.experimental import pallas as pl
from jax.experimental.pallas import tpu as pltpu

D_MODEL = 4096
POOL_WIDTH = 2048
POOL_WINDOWS = (2, 4, 8, 16)
POOL_GROUP_WIDTH = 512
POOL_STATE = 15
HEAD_DIM = 64
N_HEADS = 32
N_KV_HEADS = 4
GQA_GROUP = 8
ATTN_WIDTH = 2048
KV_WIDTH = 256
WINDOW = 128
PAST_LEN = 8192
N_IN = 16896
COL_U, COL_ZA, COL_Q, COL_K, COL_V, COL_ZB, COL_GA, COL_GB = 0, 2048, 4096, 6144, 6400, 6656, 8704, 12800
DEEPNORM_ALPHA = 2.0 ** 0.25
LN_EPS = 1e-5

V7X_VMEM_BYTES = 64 * 1024 * 1024
SUBLANES = 8
HALO_ROWS = 16
MXU_WIDTH = 256
PROJ_TN = 3 * MXU_WIDTH
PROJ_MAIN_TN = 7 * MXU_WIDTH
MERGE_TN = 4 * MXU_WIDTH
WO_SLICE_ROWS = 32
STREAM_STEPS = 16

_F32 = jnp.float32
_BF16 = jnp.bfloat16


def _params(sem, vmem_mib):
    return pltpu.CompilerParams(dimension_semantics=sem, vmem_limit_bytes=vmem_mib * 1024 * 1024)


def _stream_cast_steps(n_tiles, steps_per_pass):
    last = n_tiles - 1

    def row_blk(j, i):
        return jnp.where(j > 0, i, 0)

    def col_blk(j, i):
        return jnp.maximum(j - 1, 0)

    def chunk_blk(j, i):
        return jnp.where(j <= last, i, steps_per_pass - 1)

    def stage_blk(j, i):
        return jnp.minimum(j, last)

    return row_blk, col_blk, chunk_blk, stage_blk


def _proj_main_kernel(x_ref, wc_ref, wo_ref, proj_in_ref, o_ref, wo_bf16_ref, wb0_ref, wb1_ref):
    del proj_in_ref
    j = pl.program_id(0)
    chunk = wc_ref.shape[0]
    rows = pl.ds(pl.multiple_of(pl.program_id(1) * chunk, chunk), chunk)

    def step(w_next_ref, w_cur_ref):
        wo_bf16_ref[...] = wo_ref[...].astype(_BF16)
        w_next_ref[rows, :] = wc_ref[...].astype(_BF16)
        if w_cur_ref is not None:
            o_ref[...] = jnp.dot(x_ref[...], w_cur_ref[...], preferred_element_type=_F32)

    @pl.when(j == 0)
    def _stage_only():
        step(wb0_ref, None)

    @pl.when((j > 0) & (j % 2 == 0))
    def _even():
        step(wb0_ref, wb1_ref)

    @pl.when(j % 2 == 1)
    def _odd():
        step(wb1_ref, wb0_ref)


def _proj_main(x_bf16, w_f32, proj, w_out_f32, steps=STREAM_STEPS, tn=PROJ_MAIN_TN):
    rows, k = x_bf16.shape
    n = w_f32.shape[1]
    n_tiles = (n - PROJ_TN) // tn
    assert n_tiles * tn + PROJ_TN == n and rows % steps == 0 and k % steps == 0
    tm = rows // steps
    chunk = k // steps
    row_blk, col_blk, chunk_blk, stage_blk = _stream_cast_steps(n_tiles, steps)
    n_slices = w_out_f32.shape[0] // WO_SLICE_ROWS
    assert n_slices <= (n_tiles + 1) * steps
    wo_spec = pl.BlockSpec((WO_SLICE_ROWS, w_out_f32.shape[1]),
                           lambda j, i: (jnp.minimum(j * steps + i, n_slices - 1), 0))
    return pl.pallas_call(
        _proj_main_kernel,
        grid=(n_tiles + 1, steps),
        in_specs=[pl.BlockSpec((tm, k), lambda j, i: (row_blk(j, i), 0)),
                  pl.BlockSpec((chunk, tn), lambda j, i: (chunk_blk(j, i), stage_blk(j, i))),
                  wo_spec,
                  pl.BlockSpec(memory_space=pl.ANY)],
        out_specs=[pl.BlockSpec((tm, tn), lambda j, i: (row_blk(j, i), col_blk(j, i))), wo_spec],
        out_shape=[jax.ShapeDtypeStruct((rows, n), _F32), jax.ShapeDtypeStruct(w_out_f32.shape, _BF16)],
        input_output_aliases={3: 0},
        scratch_shapes=[pltpu.VMEM((k, tn), _BF16), pltpu.VMEM((k, tn), _BF16)],
        compiler_params=_params(("arbitrary", "arbitrary"), 58),
        name="proj",
    )(x_bf16, w_f32, w_out_f32, proj)


def _proj_cast_kernel(x_ref, w_ref, *refs):
    xb_ref, o_ref, wb_ref = refs[-3:]

    @pl.when(pl.program_id(0) == 0)
    def _cast_weight_tile():
        wb_ref[...] = w_ref[...].astype(_BF16)

    xb = x_ref[...].astype(_BF16)
    xb_ref[...] = xb
    o_ref[...] = jnp.dot(xb, wb_ref[...], preferred_element_type=_F32)


def _proj_cast(x_f32, w_f32, row0, total_rows, x_all=None, proj=None, tm=512, tn=PROJ_TN):
    rows, k = x_f32.shape
    n = w_f32.shape[1]
    blk0 = row0 // tm
    col_blk = n // tn - 1
    aliased = [] if x_all is None else [x_all, proj]
    return pl.pallas_call(
        _proj_cast_kernel,
        grid=(rows // tm,),
        in_specs=[pl.BlockSpec((tm, k), lambda i: (i, 0)),
                  pl.BlockSpec((k, tn), lambda i: (0, col_blk), pipeline_mode=pl.Buffered(1))]
                 + [pl.BlockSpec(memory_space=pl.ANY)] * len(aliased),
        out_specs=[pl.BlockSpec((tm, k), lambda i: (blk0 + i, 0)),
                   pl.BlockSpec((tm, tn), lambda i: (blk0 + i, col_blk))],
        out_shape=[jax.ShapeDtypeStruct((total_rows, k), _BF16),
                   jax.ShapeDtypeStruct((total_rows, n), _F32)],
        input_output_aliases={2: 0, 3: 1} if aliased else {},
        scratch_shapes=[pltpu.VMEM((k, tn), _BF16)],
        compiler_params=_params(("arbitrary",), 56),
        name="proj_cast",
    )(x_f32, w_f32, *aliased)


def _window_sums(ext, n_doublings):
    s = ext
    for d in range(n_doublings):
        s = s + pltpu.roll(s, 1 << d, axis=0)
    return s


def _pool_groups(ext, u, z_a, pos, wp_ref, scale_ref, halo):
    outs = []
    for g, w in enumerate(POOL_WINDOWS):
        sl = slice(g * POOL_GROUP_WIDTH, (g + 1) * POOL_GROUP_WIDTH)
        s = _window_sums(ext[:, sl], g + 1)[halo:]
        count = jnp.minimum(w, pos + 1).astype(_F32)
        pooled = s / count - u[:, sl]
        mixed = jnp.dot(pooled.astype(_BF16), wp_ref[g], preferred_element_type=_F32)
        pool_out = mixed * scale_ref[:, sl]
        z = z_a[:, sl]
        outs.append(z * jax.nn.sigmoid(z) * pool_out)
    return outs


def _pool_prompt_tile(t, u_ref, halo_ref, za_ref, wp_ref, scale_ref, a_ref):
    tm = u_ref.shape[0]
    u = u_ref[...]
    halo = jnp.where(t == 0, 0.0, halo_ref[...])
    ext = jnp.concatenate([halo, u], axis=0)
    pos = t * tm + lax.broadcasted_iota(jnp.int32, (tm, 1), 0)
    outs = _pool_groups(ext, u, za_ref[...], pos, wp_ref, scale_ref, HALO_ROWS)
    for g, o in enumerate(outs):
        a_ref[:, g * POOL_GROUP_WIDTH:(g + 1) * POOL_GROUP_WIDTH] = o.astype(_BF16)


def _pool_sample_kernel(u_ref, za_ref, st_ref, wp_ref, scale_ref, a_in_ref, a_ref, np_ref, *, bs, t_new):
    del a_in_ref
    u3 = u_ref[...]
    st = st_ref[...]
    rows_per_seq = HALO_ROWS + t_new
    ext = jnp.concatenate([st, u3], axis=1).reshape(bs * rows_per_seq, POOL_WIDTH)
    pos3 = PAST_LEN + lax.broadcasted_iota(jnp.int32, (bs, t_new, 1), 1)
    pos = pos3.reshape(bs * t_new, 1)
    u = u3.reshape(bs * t_new, POOL_WIDTH)
    za = za_ref[...].reshape(bs * t_new, POOL_WIDTH)
    for g, w in enumerate(POOL_WINDOWS):
        sl = slice(g * POOL_GROUP_WIDTH, (g + 1) * POOL_GROUP_WIDTH)
        s = _window_sums(ext[:, sl], g + 1)
        s = s.reshape(bs, rows_per_seq, POOL_GROUP_WIDTH)[:, HALO_ROWS:, :].reshape(bs * t_new, POOL_GROUP_WIDTH)
        count = jnp.minimum(w, pos + 1).astype(_F32)
        pooled = s / count - u[:, sl]
        mixed = jnp.dot(pooled.astype(_BF16), wp_ref[g], preferred_element_type=_F32)
        pool_out = mixed * scale_ref[:, sl]
        z = za[:, sl]
        a_ref[:, sl] = (z * jax.nn.sigmoid(z) * pool_out).astype(_BF16)
    keep = POOL_STATE - t_new
    np_ref[:, 0:keep, :] = st[:, HALO_ROWS - keep:, :]
    np_ref[:, keep:, :] = u3


def _pool_sample(proj3, state16, w_pool_bf16, pool_scale, a_all, seq0, bs=16):
    nseq = state16.shape[0]
    t_new = proj3.shape[1]
    blk0 = seq0 // bs
    kern = functools.partial(_pool_sample_kernel, bs=bs, t_new=t_new)
    return pl.pallas_call(
        kern,
        grid=(nseq // bs,),
        in_specs=[
            pl.BlockSpec((bs, t_new, POOL_WIDTH), lambda i: (blk0 + i, 0, COL_U // POOL_WIDTH)),
            pl.BlockSpec((bs, t_new, POOL_WIDTH), lambda i: (blk0 + i, 0, COL_ZA // POOL_WIDTH)),
            pl.BlockSpec((bs, HALO_ROWS, POOL_WIDTH), lambda i: (i, 0, 0)),
            pl.BlockSpec((4, POOL_GROUP_WIDTH, POOL_GROUP_WIDTH), lambda i: (0, 0, 0)),
            pl.BlockSpec((1, POOL_WIDTH), lambda i: (0, 0)),
            pl.BlockSpec(memory_space=pl.ANY),
        ],
        out_specs=[
            pl.BlockSpec((bs * t_new, POOL_WIDTH), lambda i: (blk0 + i, 0)),
            pl.BlockSpec((bs, POOL_STATE, POOL_WIDTH), lambda i: (i, 0, 0)),
        ],
        out_shape=[
            jax.ShapeDtypeStruct(a_all.shape, _BF16),
            jax.ShapeDtypeStruct((nseq, POOL_STATE, POOL_WIDTH), _F32),
        ],
        input_output_aliases={5: 0},
        compiler_params=_params(("arbitrary",), 48),
        name="pool_sample",
    )(proj3, proj3, state16, w_pool_bf16, pool_scale, a_all)


def _softmax_with_sink(l, sink):
    m = jnp.maximum(jnp.max(l, axis=-1, keepdims=True), sink)
    p = jnp.exp(l - m)
    den = jnp.sum(p, axis=-1, keepdims=True) + jnp.exp(sink - m)
    return p * (1.0 / den)


def _penalty(slope, dist, valid):
    return jnp.where(valid, slope * dist.astype(_F32), jnp.inf)


def _mix_prompt_kernel(slopes_ref, sinks_ref, q_ref, kvc_ref, kvp_ref, zb0, zb1, zb2, zb3,
                       u_ref, halo_ref, za_ref, wp_ref, scale_ref, o_ref, a_ref, pen_ref, sink_ref):
    _pool_prompt_tile(pl.program_id(1), u_ref, halo_ref, za_ref, wp_ref, scale_ref, a_ref)
    blk = WINDOW
    gw = GQA_GROUP * HEAD_DIM
    nq = GQA_GROUP * blk
    n = pl.program_id(1)

    @pl.when((pl.program_id(0) == 0) & (n == 0))
    def _fill_tables():
        si = lax.broadcasted_iota(jnp.int32, (2 * blk, blk), 0)
        ti = lax.broadcasted_iota(jnp.int32, (2 * blk, blk), 1)
        dist = blk + ti - si
        for n_tbl in (0, 1):
            key_pos = (n_tbl - 1) * blk + si
            valid = (dist >= 0) & (dist < WINDOW) & (key_pos >= 0)
            for h in range(N_HEADS):
                kh, g = divmod(h, GQA_GROUP)
                pen_ref[n_tbl, kh, :, g * blk:(g + 1) * blk] = _penalty(slopes_ref[h], dist, valid)
        for h in range(N_HEADS):
            kh, g = divmod(h, GQA_GROUP)
            sink_ref[kh, :, g * blk:(g + 1) * blk] = jnp.full((1, blk), sinks_ref[h], _F32)

    tbl = jnp.minimum(n, 1)
    kv = jnp.concatenate([kvp_ref[...], kvc_ref[...]], axis=0)
    k_all = kv[:, 0:KV_WIDTH].astype(_BF16)
    v_t = kv[:, KV_WIDTH:2 * KV_WIDTH].T.astype(_BF16)
    zbs = (zb0, zb1, zb2, zb3)
    o_t = []
    for kh in range(N_KV_HEADS):
        hd = slice(kh * HEAD_DIM, (kh + 1) * HEAD_DIM)
        q_kh = (q_ref[:, kh * gw:(kh + 1) * gw] * (HEAD_DIM ** -0.5)).astype(_BF16)
        qs = jnp.concatenate([q_kh[:, g * HEAD_DIM:(g + 1) * HEAD_DIM] for g in range(GQA_GROUP)], axis=0)
        logits_t = lax.dot_general(k_all[:, hd], qs, (((1,), (1,)), ((), ())),
                                   preferred_element_type=_F32)
        l = logits_t - pen_ref[tbl, kh]
        sink = sink_ref[kh]
        m = jnp.maximum(jnp.max(l, axis=0, keepdims=True), sink)
        p = jnp.exp(l - m)
        den = jnp.sum(p, axis=0, keepdims=True) + jnp.exp(sink - m)
        pr_t = (p * (1.0 / den)).astype(_BF16)
        o_t.append(jnp.dot(v_t[hd, :], pr_t, preferred_element_type=_F32))
    for pair in range(N_KV_HEADS // 2):
        o_pair = jnp.concatenate([o_t[2 * pair], o_t[2 * pair + 1]], axis=0)
        tiles = [o_pair[:, g * blk:(g + 1) * blk].T for g in range(GQA_GROUP)]
        for j in range(2):
            kh = 2 * pair + j
            o_kh = jnp.concatenate([t[:, j * HEAD_DIM:(j + 1) * HEAD_DIM] for t in tiles], axis=1)
            zb = zbs[kh][...]
            o_ref[:, kh * gw:(kh + 1) * gw] = (zb * jax.nn.sigmoid(zb) * o_kh).astype(_BF16)


def _mix_prompt(proj, slopes, sinks, w_pool_bf16, pool_scale, batch, seq):
    blk = WINDOW
    nb = seq // blk
    gw = GQA_GROUP * HEAD_DIM
    halo_blocks = blk // HALO_ROWS
    smem = pl.BlockSpec(memory_space=pltpu.SMEM)

    def zb_spec(kh):
        return pl.BlockSpec((blk, gw), lambda b, n: (b * nb + n, COL_ZB // gw + kh))

    rows = proj.shape[0]
    return pl.pallas_call(
        _mix_prompt_kernel,
        grid=(batch, nb),
        in_specs=[
            smem, smem,
            pl.BlockSpec((blk, ATTN_WIDTH), lambda b, n: (b * nb + n, COL_Q // ATTN_WIDTH)),
            pl.BlockSpec((blk, 2 * KV_WIDTH), lambda b, n: (b * nb + n, COL_K // (2 * KV_WIDTH))),
            pl.BlockSpec((blk, 2 * KV_WIDTH), lambda b, n: (b * nb + jnp.maximum(n - 1, 0), COL_K // (2 * KV_WIDTH))),
            zb_spec(0), zb_spec(1), zb_spec(2), zb_spec(3),
            pl.BlockSpec((blk, POOL_WIDTH), lambda b, n: (b * nb + n, COL_U // POOL_WIDTH)),
            pl.BlockSpec((HALO_ROWS, POOL_WIDTH),
                         lambda b, n: (jnp.maximum((b * nb + n) * halo_blocks - 1, 0), COL_U // POOL_WIDTH)),
            pl.BlockSpec((blk, POOL_WIDTH), lambda b, n: (b * nb + n, COL_ZA // POOL_WIDTH)),
            pl.BlockSpec((4, POOL_GROUP_WIDTH, POOL_GROUP_WIDTH), lambda b, n: (0, 0, 0)),
            pl.BlockSpec((1, POOL_WIDTH), lambda b, n: (0, 0)),
        ],
        out_specs=[pl.BlockSpec((blk, ATTN_WIDTH), lambda b, n: (b * nb + n, 0)),
                   pl.BlockSpec((blk, POOL_WIDTH), lambda b, n: (b * nb + n, 0))],
        out_shape=[jax.ShapeDtypeStruct((rows, ATTN_WIDTH), _BF16),
                   jax.ShapeDtypeStruct((rows, POOL_WIDTH), _BF16)],
        scratch_shapes=[pltpu.VMEM((2, N_KV_HEADS, 2 * blk, GQA_GROUP * blk), _F32),
                        pltpu.VMEM((N_KV_HEADS, 1, GQA_GROUP * blk), _F32)],
        compiler_params=_params(("arbitrary", "arbitrary"), 48),
        name="mix_prompt",
    )(slopes, sinks, proj, proj, proj, proj, proj, proj, proj, proj, proj, proj, w_pool_bf16, pool_scale)


def _attn_sample_kernel(slopes_ref, sinks_ref, q_ref, kvn_ref, ck_ref, cv_ref, zb0, zb1, zb2, zb3, b_in_ref,
                        o_ref, nk_ref, nv_ref, pen_ref, *, bs, t_new):
    del b_in_ref
    s_all = WINDOW + t_new
    gw = GQA_GROUP * HEAD_DIM

    @pl.when(pl.program_id(0) == 0)
    def _fill_penalty_table():
        ti = lax.broadcasted_iota(jnp.int32, (t_new, s_all), 0)
        si = lax.broadcasted_iota(jnp.int32, (t_new, s_all), 1)
        dist = WINDOW + ti - si
        key_pos = PAST_LEN - WINDOW + si
        valid = (dist >= 0) & (dist < WINDOW) & (key_pos >= 0)
        for h in range(N_HEADS):
            pen_ref[h] = _penalty(slopes_ref[h], dist, valid)

    kvn = kvn_ref[...]
    k_all = jnp.concatenate([ck_ref[...], kvn[:, :, 0:KV_WIDTH]], axis=1)
    v_all = jnp.concatenate([cv_ref[...], kvn[:, :, KV_WIDTH:2 * KV_WIDTH]], axis=1)
    nk_ref[...] = k_all[:, t_new:, :]
    nv_ref[...] = v_all[:, t_new:, :]
    kb = k_all.astype(_BF16)
    vb = v_all.astype(_BF16)
    zbs = (zb0, zb1, zb2, zb3)
    for kh in range(N_KV_HEADS):
        q_kh = (q_ref[:, :, kh * gw:(kh + 1) * gw] * (HEAD_DIM ** -0.5)).astype(_BF16)
        qs = jnp.concatenate([q_kh[:, :, g * HEAD_DIM:(g + 1) * HEAD_DIM] for g in range(GQA_GROUP)], axis=1)
        hd = slice(kh * HEAD_DIM, (kh + 1) * HEAD_DIM)
        logits = jnp.stack(
            [lax.dot_general(qs[b], kb[b, :, hd], (((1,), (1,)), ((), ())), preferred_element_type=_F32)
             for b in range(bs)], axis=0)
        probs = []
        for g in range(GQA_GROUP):
            h = kh * GQA_GROUP + g
            l = logits[:, g * t_new:(g + 1) * t_new, :] - pen_ref[h]
            probs.append(_softmax_with_sink(l, sinks_ref[h]).astype(_BF16))
        pr = jnp.concatenate(probs, axis=1)
        o = jnp.stack([jnp.dot(pr[b], vb[b, :, hd], preferred_element_type=_F32) for b in range(bs)], axis=0)
        o_kh = jnp.concatenate([o[:, g * t_new:(g + 1) * t_new, :] for g in range(GQA_GROUP)], axis=2)
        zb = zbs[kh][...]
        res = (zb * jax.nn.sigmoid(zb) * o_kh).reshape(bs * t_new, gw)
        o_ref[:, kh * gw:(kh + 1) * gw] = res.astype(_BF16)


def _attn_sample(proj3, cache_k, cache_v, slopes, sinks, b_all, seq0, bs=16):
    nseq = cache_k.shape[0]
    t_new = proj3.shape[1]
    blk0 = seq0 // bs
    gw = GQA_GROUP * HEAD_DIM
    smem = pl.BlockSpec(memory_space=pltpu.SMEM)
    kern = functools.partial(_attn_sample_kernel, bs=bs, t_new=t_new)

    def zb_spec(kh):
        return pl.BlockSpec((bs, t_new, gw), lambda i: (blk0 + i, 0, COL_ZB // gw + kh))

    cache_spec = pl.BlockSpec((bs, WINDOW, KV_WIDTH), lambda i: (i, 0, 0))
    return pl.pallas_call(
        kern,
        grid=(nseq // bs,),
        in_specs=[
            smem, smem,
            pl.BlockSpec((bs, t_new, ATTN_WIDTH), lambda i: (blk0 + i, 0, COL_Q // ATTN_WIDTH)),
            pl.BlockSpec((bs, t_new, 2 * KV_WIDTH), lambda i: (blk0 + i, 0, COL_K // (2 * KV_WIDTH))),
            cache_spec, cache_spec,
            zb_spec(0), zb_spec(1), zb_spec(2), zb_spec(3),
            pl.BlockSpec(memory_space=pl.ANY),
        ],
        out_specs=[
            pl.BlockSpec((bs * t_new, ATTN_WIDTH), lambda i: (blk0 + i, 0)),
            cache_spec, cache_spec,
        ],
        out_shape=[
            jax.ShapeDtypeStruct(b_all.shape, _BF16),
            jax.ShapeDtypeStruct((nseq, WINDOW, KV_WIDTH), _F32),
            jax.ShapeDtypeStruct((nseq, WINDOW, KV_WIDTH), _F32),
        ],
        input_output_aliases={10: 0},
        scratch_shapes=[pltpu.VMEM((N_HEADS, t_new, WINDOW + t_new), _F32)],
        compiler_params=_params(("arbitrary",), 48),
        name="attn_sample",
    )(slopes, sinks, proj3, proj3, cache_k, cache_v, proj3, proj3, proj3, proj3, b_all)


def _merge_kernel(a_ref, b_ref, wpc_ref, wac_ref, ga0_ref, ga1_ref, gb0_ref, gb1_ref, h_ref,
                  wp0_ref, wp1_ref, wa0_ref, wa1_ref, *, row_chunk):
    j = pl.program_id(0)
    chunk = wpc_ref.shape[0]
    wrows = pl.ds(pl.multiple_of(pl.program_id(1) * chunk, chunk), chunk)
    half = ga0_ref.shape[1]

    def step(wp_next_ref, wa_next_ref, wp_cur_ref, wa_cur_ref):
        wp_next_ref[wrows, :] = wpc_ref[...].astype(_BF16)
        wa_next_ref[wrows, :] = wac_ref[...].astype(_BF16)
        if wp_cur_ref is None:
            return
        for r in range(0, a_ref.shape[0], row_chunk):
            rows = pl.ds(r, row_chunk)
            ya = jnp.dot(a_ref[rows, :], wp_cur_ref[...], preferred_element_type=_F32)
            yb = jnp.dot(b_ref[rows, :], wa_cur_ref[...], preferred_element_type=_F32)
            for c, (ga_ref, gb_ref) in enumerate(((ga0_ref, gb0_ref), (ga1_ref, gb1_ref))):
                cols = slice(c * half, (c + 1) * half)
                h = jax.nn.sigmoid(ga_ref[rows, :]) * ya[:, cols] + jax.nn.sigmoid(gb_ref[rows, :]) * yb[:, cols]
                h_ref[rows, cols] = h.astype(_BF16)

    @pl.when(j == 0)
    def _stage_only():
        step(wp0_ref, wa0_ref, None, None)

    @pl.when((j > 0) & (j % 2 == 0))
    def _even():
        step(wp0_ref, wa0_ref, wp1_ref, wa1_ref)

    @pl.when(j % 2 == 1)
    def _odd():
        step(wp1_ref, wa1_ref, wp0_ref, wa0_ref)


def _merge(a, b, proj, w_out_pool, w_out_attn, steps=STREAM_STEPS, tn=MERGE_TN):
    rows = a.shape[0]
    n_tiles = D_MODEL // tn
    assert rows % steps == 0 and POOL_WIDTH % steps == 0 and ATTN_WIDTH == POOL_WIDTH
    tm = rows // steps
    chunk = POOL_WIDTH // steps
    half = tn // 2
    row_blk, col_blk, chunk_blk, stage_blk = _stream_cast_steps(n_tiles, steps)

    def gate_spec(col0, c):
        return pl.BlockSpec((tm, half), lambda j, i: (row_blk(j, i), col0 // half + 2 * col_blk(j, i) + c))

    return pl.pallas_call(
        functools.partial(_merge_kernel, row_chunk=tm // 2),
        grid=(n_tiles + 1, steps),
        in_specs=[
            pl.BlockSpec((tm, POOL_WIDTH), lambda j, i: (row_blk(j, i), 0)),
            pl.BlockSpec((tm, ATTN_WIDTH), lambda j, i: (row_blk(j, i), 0)),
            pl.BlockSpec((chunk, tn), lambda j, i: (chunk_blk(j, i), stage_blk(j, i))),
            pl.BlockSpec((chunk, tn), lambda j, i: (chunk_blk(j, i), stage_blk(j, i))),
            gate_spec(COL_GA, 0), gate_spec(COL_GA, 1), gate_spec(COL_GB, 0), gate_spec(COL_GB, 1),
        ],
        out_specs=pl.BlockSpec((tm, tn), lambda j, i: (row_blk(j, i), col_blk(j, i))),
        out_shape=jax.ShapeDtypeStruct((rows, D_MODEL), _BF16),
        scratch_shapes=[pltpu.VMEM((POOL_WIDTH, tn), _BF16)] * 2 + [pltpu.VMEM((ATTN_WIDTH, tn), _BF16)] * 2,
        compiler_params=_params(("arbitrary", "arbitrary"), 56),
        name="merge",
    )(a, b, w_out_pool, w_out_attn, proj, proj, proj, proj)


def _out_kernel(h_ref, w_ref, x_ref, gain_ref, bias_ref, y_ref, *, sub):
    gain = gain_ref[...]
    bias = bias_ref[...]
    for r0 in range(0, h_ref.shape[0], sub):
        y_ref[pl.ds(r0, sub), :] = jnp.dot(h_ref[pl.ds(r0, sub), :], w_ref[...], preferred_element_type=_F32)
        for g in range(sub // SUBLANES):
            rows = pl.ds(r0 + g * SUBLANES, SUBLANES)
            y = DEEPNORM_ALPHA * x_ref[rows, :] + y_ref[rows, :]
            mu = jnp.mean(y, axis=-1, keepdims=True)
            d = y - mu
            var = jnp.mean(d * d, axis=-1, keepdims=True)
            y_ref[rows, :] = d * lax.rsqrt(var + LN_EPS) * gain + bias


def _out(h_all, row0, w_out_bf16, x, gain, bias, tm=256, sub=128):
    rows = x.shape[0]
    blk0 = row0 // tm
    return pl.pallas_call(
        functools.partial(_out_kernel, sub=sub),
        grid=(rows // tm,),
        in_specs=[
            pl.BlockSpec((tm, D_MODEL), lambda i: (blk0 + i, 0)),
            pl.BlockSpec((D_MODEL, D_MODEL), lambda i: (0, 0), pipeline_mode=pl.Buffered(1)),
            pl.BlockSpec((tm, D_MODEL), lambda i: (i, 0)),
            pl.BlockSpec((1, D_MODEL), lambda i: (0, 0)),
            pl.BlockSpec((1, D_MODEL), lambda i: (0, 0)),
        ],
        out_specs=pl.BlockSpec((tm, D_MODEL), lambda i: (i, 0)),
        out_shape=jax.ShapeDtypeStruct((rows, D_MODEL), _F32),
        compiler_params=_params(("arbitrary",), 60),
        name="out",
    )(h_all, w_out_bf16, x, gain, bias)


def _alibi_slopes():
    h = jnp.arange(N_HEADS, dtype=_F32)
    return 2.0 ** (-8.0 * (h + 1.0) / N_HEADS)


def kernel(x_prompt, x_sample, cache_k, cache_v, state_pool, w_in, w_pool, pool_scale, sinks,
           w_out_pool, w_out_attn, w_out, ln_gain, ln_bias):
    depth = w_in.shape[0]
    assert depth == 1
    bp, tp, _ = x_prompt.shape
    bs, ts, _ = x_sample.shape
    slopes = _alibi_slopes()

    l = 0
    w_pool_b = w_pool[l].astype(_BF16)
    scale = pool_scale[l].reshape(1, POOL_WIDTH)
    gain = ln_gain[l].reshape(1, D_MODEL)
    bias = ln_bias[l].reshape(1, D_MODEL)
    sink = sinks[l]

    xp2 = x_prompt.reshape(bp * tp, D_MODEL)
    xs2 = x_sample.reshape(bs * ts, D_MODEL)
    rows_p = bp * tp
    rows_all = rows_p + bs * ts
    x_all, proj = _proj_cast(xp2, w_in[l], 0, rows_all)
    x_all, proj = _proj_cast(xs2, w_in[l], rows_p, rows_all, x_all, proj)
    proj, w_o_b = _proj_main(x_all, w_in[l], proj, w_out[l])
    proj3 = proj.reshape(proj.shape[0] // ts, ts, N_IN)

    b_all, a_all = _mix_prompt(proj, slopes, sink, w_pool_b, scale, bp, tp)
    state16 = jnp.pad(state_pool[l], ((0, 0), (HALO_ROWS - POOL_STATE, 0), (0, 0)))
    a_all, new_pool_s = _pool_sample(proj3, state16, w_pool_b, scale, a_all, rows_p // ts)

    ck = cache_k[l].reshape(bs, WINDOW, KV_WIDTH)
    cv = cache_v[l].reshape(bs, WINDOW, KV_WIDTH)
    b_all, new_k_s, new_v_s = _attn_sample(proj3, ck, cv, slopes, sink, b_all, rows_p // ts)

    h_all = _merge(a_all, b_all, proj, w_out_pool[l], w_out_attn[l])
    y_p = _out(h_all, 0, w_o_b, xp2, gain, bias).reshape(bp, tp, D_MODEL)
    y_s = _out(h_all, rows_p, w_o_b, xs2, gain, bias).reshape(bs, ts, D_MODEL)

    def tail(n_rows, col0, width):
        return jnp.stack([lax.slice(proj, ((b + 1) * tp - n_rows, col0), ((b + 1) * tp, col0 + width))
                          for b in range(bp)])

    new_k_p = tail(WINDOW, COL_K, KV_WIDTH).reshape(1, bp, WINDOW, N_KV_HEADS, HEAD_DIM)
    new_v_p = tail(WINDOW, COL_V, KV_WIDTH).reshape(1, bp, WINDOW, N_KV_HEADS, HEAD_DIM)
    new_pool_p = tail(POOL_STATE, COL_U, POOL_WIDTH)[None]
    new_k_s = new_k_s.reshape(1, bs, WINDOW, N_KV_HEADS, HEAD_DIM)
    new_v_s = new_v_s.reshape(1, bs, WINDOW, N_KV_HEADS, HEAD_DIM)

    return (y_p, y_s, new_k_p, new_v_p, new_pool_p, new_k_s, new_v_s, new_pool_s[None])
```

```python
import functools

import jax
import jax.numpy as jnp
from jax import lax
from jax.experimental import pallas as pl
from jax.experimental.pallas import tpu as pltpu

D_MODEL = 4096
POOL_WIDTH = 2048
POOL_WINDOWS = (2, 4, 8, 16)
POOL_GROUP_WIDTH = 512
POOL_STATE = 15
HEAD_DIM = 64
N_HEADS = 32
N_KV_HEADS = 4
GQA_GROUP = 8
ATTN_WIDTH = 2048
KV_WIDTH = 256
WINDOW = 128
PAST_LEN = 8192
N_IN = 16896
COL_U, COL_ZA, COL_Q, COL_K, COL_V, COL_ZB, COL_GA, COL_GB = 0, 2048, 4096, 6144, 6400, 6656, 8704, 12800
DEEPNORM_ALPHA = 2.0 ** 0.25
LN_EPS = 1e-5

V7X_VMEM_BYTES = 64 * 1024 * 1024
SUBLANES = 8
HALO_ROWS = 16
MXU_WIDTH = 256
PROJ_TN = 3 * MXU_WIDTH
PROJ_MAIN_TN = 7 * MXU_WIDTH
MERGE_TN = 4 * MXU_WIDTH
WO_SLICE_ROWS = 32
STREAM_STEPS = 16

_F32 = jnp.float32
_BF16 = jnp.bfloat16


def _params(sem, vmem_mib):
    return pltpu.CompilerParams(dimension_semantics=sem, vmem_limit_bytes=vmem_mib * 1024 * 1024)


def _stream_cast_steps(n_tiles, steps_per_pass):
    last = n_tiles - 1

    def row_blk(j, i):
        return jnp.where(j > 0, i, 0)

    def col_blk(j, i):
        return jnp.maximum(j - 1, 0)

    def chunk_blk(j, i):
        return jnp.where(j <= last, i, steps_per_pass - 1)

    def stage_blk(j, i):
        return jnp.minimum(j, last)

    return row_blk, col_blk, chunk_blk, stage_blk


def _proj_main_kernel(x_ref, wc_ref, wo_ref, proj_in_ref, o_ref, wo_bf16_ref, wb0_ref, wb1_ref):
    del proj_in_ref
    j = pl.program_id(0)
    chunk = wc_ref.shape[0]
    rows = pl.ds(pl.multiple_of(pl.program_id(1) * chunk, chunk), chunk)

    def step(w_next_ref, w_cur_ref):
        wo_bf16_ref[...] = wo_ref[...].astype(_BF16)
        w_next_ref[rows, :] = wc_ref[...].astype(_BF16)
        if w_cur_ref is not None:
            o_ref[...] = jnp.dot(x_ref[...], w_cur_ref[...], preferred_element_type=_F32)

    @pl.when(j == 0)
    def _stage_only():
        step(wb0_ref, None)

    @pl.when((j > 0) & (j % 2 == 0))
    def _even():
        step(wb0_ref, wb1_ref)

    @pl.when(j % 2 == 1)
    def _odd():
        step(wb1_ref, wb0_ref)


def _proj_main(x_bf16, w_f32, proj, w_out_f32, steps=STREAM_STEPS, tn=PROJ_MAIN_TN):
    rows, k = x_bf16.shape
    n = w_f32.shape[1]
    n_tiles = (n - PROJ_TN) // tn
    assert n_tiles * tn + PROJ_TN == n and rows % steps == 0 and k % steps == 0
    tm = rows // steps
    chunk = k // steps
    row_blk, col_blk, chunk_blk, stage_blk = _stream_cast_steps(n_tiles, steps)
    n_slices = w_out_f32.shape[0] // WO_SLICE_ROWS
    assert n_slices <= (n_tiles + 1) * steps
    wo_spec = pl.BlockSpec((WO_SLICE_ROWS, w_out_f32.shape[1]),
                           lambda j, i: (jnp.minimum(j * steps + i, n_slices - 1), 0))
    return pl.pallas_call(
        _proj_main_kernel,
        grid=(n_tiles + 1, steps),
        in_specs=[pl.BlockSpec((tm, k), lambda j, i: (row_blk(j, i), 0)),
                  pl.BlockSpec((chunk, tn), lambda j, i: (chunk_blk(j, i), stage_blk(j, i))),
                  wo_spec,
                  pl.BlockSpec(memory_space=pl.ANY)],
        out_specs=[pl.BlockSpec((tm, tn), lambda j, i: (row_blk(j, i), col_blk(j, i))), wo_spec],
        out_shape=[jax.ShapeDtypeStruct((rows, n), _F32), jax.ShapeDtypeStruct(w_out_f32.shape, _BF16)],
        input_output_aliases={3: 0},
        scratch_shapes=[pltpu.VMEM((k, tn), _BF16), pltpu.VMEM((k, tn), _BF16)],
        compiler_params=_params(("arbitrary", "arbitrary"), 58),
        name="proj",
    )(x_bf16, w_f32, w_out_f32, proj)


def _proj_cast_kernel(x_ref, w_ref, *refs):
    xb_ref, o_ref, wb_ref = refs[-3:]

    @pl.when(pl.program_id(0) == 0)
    def _cast_weight_tile():
        wb_ref[...] = w_ref[...].astype(_BF16)

    xb = x_ref[...].astype(_BF16)
    xb_ref[...] = xb
    o_ref[...] = jnp.dot(xb, wb_ref[...], preferred_element_type=_F32)


def _proj_cast(x_f32, w_f32, row0, total_rows, x_all=None, proj=None, tm=512, tn=PROJ_TN):
    rows, k = x_f32.shape
    n = w_f32.shape[1]
    blk0 = row0 // tm
    col_blk = n // tn - 1
    aliased = [] if x_all is None else [x_all, proj]
    return pl.pallas_call(
        _proj_cast_kernel,
        grid=(rows // tm,),
        in_specs=[pl.BlockSpec((tm, k), lambda i: (i, 0)),
                  pl.BlockSpec((k, tn), lambda i: (0, col_blk), pipeline_mode=pl.Buffered(1))]
                 + [pl.BlockSpec(memory_space=pl.ANY)] * len(aliased),
        out_specs=[pl.BlockSpec((tm, k), lambda i: (blk0 + i, 0)),
                   pl.BlockSpec((tm, tn), lambda i: (blk0 + i, col_blk))],
        out_shape=[jax.ShapeDtypeStruct((total_rows, k), _BF16),
                   jax.ShapeDtypeStruct((total_rows, n), _F32)],
        input_output_aliases={2: 0, 3: 1} if aliased else {},
        scratch_shapes=[pltpu.VMEM((k, tn), _BF16)],
        compiler_params=_params(("arbitrary",), 56),
        name="proj_cast",
    )(x_f32, w_f32, *aliased)


def _window_sums(ext, n_doublings):
    s = ext
    for d in range(n_doublings):
        s = s + pltpu.roll(s, 1 << d, axis=0)
    return s


def _pool_groups(ext, u, z_a, pos, wp_ref, scale_ref, halo):
    outs = []
    for g, w in enumerate(POOL_WINDOWS):
        sl = slice(g * POOL_GROUP_WIDTH, (g + 1) * POOL_GROUP_WIDTH)
        s = _window_sums(ext[:, sl], g + 1)[halo:]
        count = jnp.minimum(w, pos + 1).astype(_F32)
        pooled = s / count - u[:, sl]
        mixed = jnp.dot(pooled.astype(_BF16), wp_ref[g], preferred_element_type=_F32)
        pool_out = mixed * scale_ref[:, sl]
        z = z_a[:, sl]
        outs.append(z * jax.nn.sigmoid(z) * pool_out)
    return outs


def _pool_prompt_tile(t, u_ref, halo_ref, za_ref, wp_ref, scale_ref, a_ref):
    tm = u_ref.shape[0]
    u = u_ref[...]
    halo = jnp.where(t == 0, 0.0, halo_ref[...])
    ext = jnp.concatenate([halo, u], axis=0)
    pos = t * tm + lax.broadcasted_iota(jnp.int32, (tm, 1), 0)
    outs = _pool_groups(ext, u, za_ref[...], pos, wp_ref, scale_ref, HALO_ROWS)
    for g, o in enumerate(outs):
        a_ref[:, g * POOL_GROUP_WIDTH:(g + 1) * POOL_GROUP_WIDTH] = o.astype(_BF16)


def _pool_sample_tile(u_ref, za_ref, st_ref, wp_ref, scale_ref, a_ref, np_ref, bs, t_new):
    u3 = u_ref[...]
    st = st_ref[...]
    rows_per_seq = HALO_ROWS + t_new
    ext = jnp.concatenate([st, u3], axis=1).reshape(bs * rows_per_seq, POOL_WIDTH)
    pos3 = PAST_LEN + lax.broadcasted_iota(jnp.int32, (bs, t_new, 1), 1)
    pos = pos3.reshape(bs * t_new, 1)
    u = u3.reshape(bs * t_new, POOL_WIDTH)
    za = za_ref[...].reshape(bs * t_new, POOL_WIDTH)
    for g, w in enumerate(POOL_WINDOWS):
        sl = slice(g * POOL_GROUP_WIDTH, (g + 1) * POOL_GROUP_WIDTH)
        s = _window_sums(ext[:, sl], g + 1)
        s = s.reshape(bs, rows_per_seq, POOL_GROUP_WIDTH)[:, HALO_ROWS:, :].reshape(bs * t_new, POOL_GROUP_WIDTH)
        count = jnp.minimum(w, pos + 1).astype(_F32)
        pooled = s / count - u[:, sl]
        mixed = jnp.dot(pooled.astype(_BF16), wp_ref[g], preferred_element_type=_F32)
        pool_out = mixed * scale_ref[:, sl]
        z = za[:, sl]
        a_ref[:, sl] = (z * jax.nn.sigmoid(z) * pool_out).astype(_BF16)
    keep = POOL_STATE - t_new
    np_ref[:, 0:keep, :] = st[:, HALO_ROWS - keep:, :]
    np_ref[:, keep:, :] = u3


def _softmax_with_sink(l, sink):
    m = jnp.maximum(jnp.max(l, axis=-1, keepdims=True), sink)
    p = jnp.exp(l - m)
    den = jnp.sum(p, axis=-1, keepdims=True) + jnp.exp(sink - m)
    return p * (1.0 / den)


def _penalty(slope, dist, valid):
    return jnp.where(valid, slope * dist.astype(_F32), jnp.inf)


def _mix_prompt_kernel(slopes_ref, sinks_ref, q_ref, kvc_ref, kvp_ref, zb0, zb1, zb2, zb3,
                       u_ref, halo_ref, za_ref, wp_ref, scale_ref, o_ref, a_ref, pen_ref, sink_ref):
    _pool_prompt_tile(pl.program_id(1), u_ref, halo_ref, za_ref, wp_ref, scale_ref, a_ref)
    blk = WINDOW
    gw = GQA_GROUP * HEAD_DIM
    nq = GQA_GROUP * blk
    n = pl.program_id(1)

    @pl.when((pl.program_id(0) == 0) & (n == 0))
    def _fill_tables():
        si = lax.broadcasted_iota(jnp.int32, (2 * blk, blk), 0)
        ti = lax.broadcasted_iota(jnp.int32, (2 * blk, blk), 1)
        dist = blk + ti - si
        for n_tbl in (0, 1):
            key_pos = (n_tbl - 1) * blk + si
            valid = (dist >= 0) & (dist < WINDOW) & (key_pos >= 0)
            for h in range(N_HEADS):
                kh, g = divmod(h, GQA_GROUP)
                pen_ref[n_tbl, kh, :, g * blk:(g + 1) * blk] = _penalty(slopes_ref[h], dist, valid)
        for h in range(N_HEADS):
            kh, g = divmod(h, GQA_GROUP)
            sink_ref[kh, :, g * blk:(g + 1) * blk] = jnp.full((1, blk), sinks_ref[h], _F32)

    tbl = jnp.minimum(n, 1)
    kv = jnp.concatenate([kvp_ref[...], kvc_ref[...]], axis=0)
    k_all = kv[:, 0:KV_WIDTH].astype(_BF16)
    v_t = kv[:, KV_WIDTH:2 * KV_WIDTH].T.astype(_BF16)
    zbs = (zb0, zb1, zb2, zb3)
    o_t = []
    for kh in range(N_KV_HEADS):
        hd = slice(kh * HEAD_DIM, (kh + 1) * HEAD_DIM)
        q_kh = (q_ref[:, kh * gw:(kh + 1) * gw] * (HEAD_DIM ** -0.5)).astype(_BF16)
        qs = jnp.concatenate([q_kh[:, g * HEAD_DIM:(g + 1) * HEAD_DIM] for g in range(GQA_GROUP)], axis=0)
        logits_t = lax.dot_general(k_all[:, hd], qs, (((1,), (1,)), ((), ())),
                                   preferred_element_type=_F32)
        l = logits_t - pen_ref[tbl, kh]
        sink = sink_ref[kh]
        m = jnp.maximum(jnp.max(l, axis=0, keepdims=True), sink)
        p = jnp.exp(l - m)
        den = jnp.sum(p, axis=0, keepdims=True) + jnp.exp(sink - m)
        pr_t = (p * (1.0 / den)).astype(_BF16)
        o_t.append(jnp.dot(v_t[hd, :], pr_t, preferred_element_type=_F32))
    for pair in range(N_KV_HEADS // 2):
        o_pair = jnp.concatenate([o_t[2 * pair], o_t[2 * pair + 1]], axis=0)
        tiles = [o_pair[:, g * blk:(g + 1) * blk].T for g in range(GQA_GROUP)]
        for j in range(2):
            kh = 2 * pair + j
            o_kh = jnp.concatenate([t[:, j * HEAD_DIM:(j + 1) * HEAD_DIM] for t in tiles], axis=1)
            zb = zbs[kh][...]
            o_ref[:, kh * gw:(kh + 1) * gw] = (zb * jax.nn.sigmoid(zb) * o_kh).astype(_BF16)


def _mix_prompt(proj, slopes, sinks, w_pool_bf16, pool_scale, batch, seq):
    blk = WINDOW
    nb = seq // blk
    gw = GQA_GROUP * HEAD_DIM
    halo_blocks = blk // HALO_ROWS
    smem = pl.BlockSpec(memory_space=pltpu.SMEM)

    def zb_spec(kh):
        return pl.BlockSpec((blk, gw), lambda b, n: (b * nb + n, COL_ZB // gw + kh))

    rows = proj.shape[0]
    return pl.pallas_call(
        _mix_prompt_kernel,
        grid=(batch, nb),
        in_specs=[
            smem, smem,
            pl.BlockSpec((blk, ATTN_WIDTH), lambda b, n: (b * nb + n, COL_Q // ATTN_WIDTH)),
            pl.BlockSpec((blk, 2 * KV_WIDTH), lambda b, n: (b * nb + n, COL_K // (2 * KV_WIDTH))),
            pl.BlockSpec((blk, 2 * KV_WIDTH), lambda b, n: (b * nb + jnp.maximum(n - 1, 0), COL_K // (2 * KV_WIDTH))),
            zb_spec(0), zb_spec(1), zb_spec(2), zb_spec(3),
            pl.BlockSpec((blk, POOL_WIDTH), lambda b, n: (b * nb + n, COL_U // POOL_WIDTH)),
            pl.BlockSpec((HALO_ROWS, POOL_WIDTH),
                         lambda b, n: (jnp.maximum((b * nb + n) * halo_blocks - 1, 0), COL_U // POOL_WIDTH)),
            pl.BlockSpec((blk, POOL_WIDTH), lambda b, n: (b * nb + n, COL_ZA // POOL_WIDTH)),
            pl.BlockSpec((4, POOL_GROUP_WIDTH, POOL_GROUP_WIDTH), lambda b, n: (0, 0, 0)),
            pl.BlockSpec((1, POOL_WIDTH), lambda b, n: (0, 0)),
        ],
        out_specs=[pl.BlockSpec((blk, ATTN_WIDTH), lambda b, n: (b * nb + n, 0)),
                   pl.BlockSpec((blk, POOL_WIDTH), lambda b, n: (b * nb + n, 0))],
        out_shape=[jax.ShapeDtypeStruct((rows, ATTN_WIDTH), _BF16),
                   jax.ShapeDtypeStruct((rows, POOL_WIDTH), _BF16)],
        scratch_shapes=[pltpu.VMEM((2, N_KV_HEADS, 2 * blk, GQA_GROUP * blk), _F32),
                        pltpu.VMEM((N_KV_HEADS, 1, GQA_GROUP * blk), _F32)],
        compiler_params=_params(("arbitrary", "arbitrary"), 48),
        name="mix_prompt",
    )(slopes, sinks, proj, proj, proj, proj, proj, proj, proj, proj, proj, proj, w_pool_bf16, pool_scale)


def _mix_sample_kernel(slopes_ref, sinks_ref, q_ref, kvn_ref, ck_ref, cv_ref, zb0, zb1, zb2, zb3,
                       u_ref, za_ref, st_ref, wp_ref, scale_ref, b_in_ref, a_in_ref,
                       o_ref, nk_ref, nv_ref, a_ref, np_ref, pen_ref, *, bs, t_new):
    del b_in_ref, a_in_ref
    _pool_sample_tile(u_ref, za_ref, st_ref, wp_ref, scale_ref, a_ref, np_ref, bs, t_new)
    s_all = WINDOW + t_new
    gw = GQA_GROUP * HEAD_DIM

    @pl.when(pl.program_id(0) == 0)
    def _fill_penalty_table():
        ti = lax.broadcasted_iota(jnp.int32, (t_new, s_all), 0)
        si = lax.broadcasted_iota(jnp.int32, (t_new, s_all), 1)
        dist = WINDOW + ti - si
        key_pos = PAST_LEN - WINDOW + si
        valid = (dist >= 0) & (dist < WINDOW) & (key_pos >= 0)
        for h in range(N_HEADS):
            pen_ref[h] = _penalty(slopes_ref[h], dist, valid)

    kvn = kvn_ref[...]
    k_all = jnp.concatenate([ck_ref[...], kvn[:, :, 0:KV_WIDTH]], axis=1)
    v_all = jnp.concatenate([cv_ref[...], kvn[:, :, KV_WIDTH:2 * KV_WIDTH]], axis=1)
    nk_ref[...] = k_all[:, t_new:, :]
    nv_ref[...] = v_all[:, t_new:, :]
    kb = k_all.astype(_BF16)
    vb = v_all.astype(_BF16)
    zbs = (zb0, zb1, zb2, zb3)
    for kh in range(N_KV_HEADS):
        q_kh = (q_ref[:, :, kh * gw:(kh + 1) * gw] * (HEAD_DIM ** -0.5)).astype(_BF16)
        qs = jnp.concatenate([q_kh[:, :, g * HEAD_DIM:(g + 1) * HEAD_DIM] for g in range(GQA_GROUP)], axis=1)
        hd = slice(kh * HEAD_DIM, (kh + 1) * HEAD_DIM)
        logits = jnp.stack(
            [lax.dot_general(qs[b], kb[b, :, hd], (((1,), (1,)), ((), ())), preferred_element_type=_F32)
             for b in range(bs)], axis=0)
        probs = []
        for g in range(GQA_GROUP):
            h = kh * GQA_GROUP + g
            l = logits[:, g * t_new:(g + 1) * t_new, :] - pen_ref[h]
            probs.append(_softmax_with_sink(l, sinks_ref[h]).astype(_BF16))
        pr = jnp.concatenate(probs, axis=1)
        o = jnp.stack([jnp.dot(pr[b], vb[b, :, hd], preferred_element_type=_F32) for b in range(bs)], axis=0)
        o_kh = jnp.concatenate([o[:, g * t_new:(g + 1) * t_new, :] for g in range(GQA_GROUP)], axis=2)
        zb = zbs[kh][...]
        res = (zb * jax.nn.sigmoid(zb) * o_kh).reshape(bs * t_new, gw)
        o_ref[:, kh * gw:(kh + 1) * gw] = res.astype(_BF16)


def _mix_sample(proj3, cache_k, cache_v, state16, slopes, sinks, w_pool_bf16, pool_scale, b_all, a_all, seq0, bs=16):
    nseq = cache_k.shape[0]
    t_new = proj3.shape[1]
    blk0 = seq0 // bs
    gw = GQA_GROUP * HEAD_DIM
    smem = pl.BlockSpec(memory_space=pltpu.SMEM)
    alias = pl.BlockSpec(memory_space=pl.ANY)
    kern = functools.partial(_mix_sample_kernel, bs=bs, t_new=t_new)

    def zb_spec(kh):
        return pl.BlockSpec((bs, t_new, gw), lambda i: (blk0 + i, 0, COL_ZB // gw + kh))

    cache_spec = pl.BlockSpec((bs, WINDOW, KV_WIDTH), lambda i: (i, 0, 0))
    in_specs = [
        smem, smem,
        pl.BlockSpec((bs, t_new, ATTN_WIDTH), lambda i: (blk0 + i, 0, COL_Q // ATTN_WIDTH)),
        pl.BlockSpec((bs, t_new, 2 * KV_WIDTH), lambda i: (blk0 + i, 0, COL_K // (2 * KV_WIDTH))),
        cache_spec, cache_spec,
        zb_spec(0), zb_spec(1), zb_spec(2), zb_spec(3),
        pl.BlockSpec((bs, t_new, POOL_WIDTH), lambda i: (blk0 + i, 0, COL_U // POOL_WIDTH)),
        pl.BlockSpec((bs, t_new, POOL_WIDTH), lambda i: (blk0 + i, 0, COL_ZA // POOL_WIDTH)),
        pl.BlockSpec((bs, HALO_ROWS, POOL_WIDTH), lambda i: (i, 0, 0)),
        pl.BlockSpec((4, POOL_GROUP_WIDTH, POOL_GROUP_WIDTH), lambda i: (0, 0, 0)),
        pl.BlockSpec((1, POOL_WIDTH), lambda i: (0, 0)),
        alias, alias,
    ]
    operands = (slopes, sinks, proj3, proj3, cache_k, cache_v, proj3, proj3, proj3, proj3,
                proj3, proj3, state16, w_pool_bf16, pool_scale, b_all, a_all)
    return pl.pallas_call(
        kern,
        grid=(nseq // bs,),
        in_specs=in_specs,
        out_specs=[
            pl.BlockSpec((bs * t_new, ATTN_WIDTH), lambda i: (blk0 + i, 0)),
            cache_spec, cache_spec,
            pl.BlockSpec((bs * t_new, POOL_WIDTH), lambda i: (blk0 + i, 0)),
            pl.BlockSpec((bs, POOL_STATE, POOL_WIDTH), lambda i: (i, 0, 0)),
        ],
        out_shape=[
            jax.ShapeDtypeStruct(b_all.shape, _BF16),
            jax.ShapeDtypeStruct((nseq, WINDOW, KV_WIDTH), _F32),
            jax.ShapeDtypeStruct((nseq, WINDOW, KV_WIDTH), _F32),
            jax.ShapeDtypeStruct(a_all.shape, _BF16),
            jax.ShapeDtypeStruct((nseq, POOL_STATE, POOL_WIDTH), _F32),
        ],
        input_output_aliases={len(operands) - 2: 0, len(operands) - 1: 3},
        scratch_shapes=[pltpu.VMEM((N_HEADS, t_new, WINDOW + t_new), _F32)],
        compiler_params=_params(("arbitrary",), 48),
        name="mix_sample",
    )(*operands)


def _merge_kernel(a_ref, b_ref, wpc_ref, wac_ref, ga0_ref, ga1_ref, gb0_ref, gb1_ref, h_ref,
                  wp0_ref, wp1_ref, wa0_ref, wa1_ref, *, row_chunk):
    j = pl.program_id(0)
    chunk = wpc_ref.shape[0]
    wrows = pl.ds(pl.multiple_of(pl.program_id(1) * chunk, chunk), chunk)
    half = ga0_ref.shape[1]

    def step(wp_next_ref, wa_next_ref, wp_cur_ref, wa_cur_ref):
        wp_next_ref[wrows, :] = wpc_ref[...].astype(_BF16)
        wa_next_ref[wrows, :] = wac_ref[...].astype(_BF16)
        if wp_cur_ref is None:
            return
        for r in range(0, a_ref.shape[0], row_chunk):
            rows = pl.ds(r, row_chunk)
            ya = jnp.dot(a_ref[rows, :], wp_cur_ref[...], preferred_element_type=_F32)
            yb = jnp.dot(b_ref[rows, :], wa_cur_ref[...], preferred_element_type=_F32)
            for c, (ga_ref, gb_ref) in enumerate(((ga0_ref, gb0_ref), (ga1_ref, gb1_ref))):
                cols = slice(c * half, (c + 1) * half)
                h = jax.nn.sigmoid(ga_ref[rows, :]) * ya[:, cols] + jax.nn.sigmoid(gb_ref[rows, :]) * yb[:, cols]
                h_ref[rows, cols] = h.astype(_BF16)

    @pl.when(j == 0)
    def _stage_only():
        step(wp0_ref, wa0_ref, None, None)

    @pl.when((j > 0) & (j % 2 == 0))
    def _even():
        step(wp0_ref, wa0_ref, wp1_ref, wa1_ref)

    @pl.when(j % 2 == 1)
    def _odd():
        step(wp1_ref, wa1_ref, wp0_ref, wa0_ref)


def _merge(a, b, proj, w_out_pool, w_out_attn, steps=STREAM_STEPS, tn=MERGE_TN):
    rows = a.shape[0]
    n_tiles = D_MODEL // tn
    assert rows % steps == 0 and POOL_WIDTH % steps == 0 and ATTN_WIDTH == POOL_WIDTH
    tm = rows // steps
    chunk = POOL_WIDTH // steps
    half = tn // 2
    row_blk, col_blk, chunk_blk, stage_blk = _stream_cast_steps(n_tiles, steps)

    def gate_spec(col0, c):
        return pl.BlockSpec((tm, half), lambda j, i: (row_blk(j, i), col0 // half + 2 * col_blk(j, i) + c))

    return pl.pallas_call(
        functools.partial(_merge_kernel, row_chunk=tm // 2),
        grid=(n_tiles + 1, steps),
        in_specs=[
            pl.BlockSpec((tm, POOL_WIDTH), lambda j, i: (row_blk(j, i), 0)),
            pl.BlockSpec((tm, ATTN_WIDTH), lambda j, i: (row_blk(j, i), 0)),
            pl.BlockSpec((chunk, tn), lambda j, i: (chunk_blk(j, i), stage_blk(j, i))),
            pl.BlockSpec((chunk, tn), lambda j, i: (chunk_blk(j, i), stage_blk(j, i))),
            gate_spec(COL_GA, 0), gate_spec(COL_GA, 1), gate_spec(COL_GB, 0), gate_spec(COL_GB, 1),
        ],
        out_specs=pl.BlockSpec((tm, tn), lambda j, i: (row_blk(j, i), col_blk(j, i))),
        out_shape=jax.ShapeDtypeStruct((rows, D_MODEL), _BF16),
        scratch_shapes=[pltpu.VMEM((POOL_WIDTH, tn), _BF16)] * 2 + [pltpu.VMEM((ATTN_WIDTH, tn), _BF16)] * 2,
        compiler_params=_params(("arbitrary", "arbitrary"), 56),
        name="merge",
    )(a, b, w_out_pool, w_out_attn, proj, proj, proj, proj)


def _out_kernel(h_ref, w_ref, x_ref, gain_ref, bias_ref, y_ref, *, sub):
    gain = gain_ref[...]
    bias = bias_ref[...]
    for r0 in range(0, h_ref.shape[0], sub):
        y_ref[pl.ds(r0, sub), :] = jnp.dot(h_ref[pl.ds(r0, sub), :], w_ref[...], preferred_element_type=_F32)
        for g in range(sub // SUBLANES):
            rows = pl.ds(r0 + g * SUBLANES, SUBLANES)
            y = DEEPNORM_ALPHA * x_ref[rows, :] + y_ref[rows, :]
            mu = jnp.mean(y, axis=-1, keepdims=True)
            d = y - mu
            var = jnp.mean(d * d, axis=-1, keepdims=True)
            y_ref[rows, :] = d * lax.rsqrt(var + LN_EPS) * gain + bias


def _out(h_all, row0, w_out_bf16, x, gain, bias, tm=256, sub=128):
    rows = x.shape[0]
    blk0 = row0 // tm
    return pl.pallas_call(
        functools.partial(_out_kernel, sub=sub),
        grid=(rows // tm,),
        in_specs=[
            pl.BlockSpec((tm, D_MODEL), lambda i: (blk0 + i, 0)),
            pl.BlockSpec((D_MODEL, D_MODEL), lambda i: (0, 0), pipeline_mode=pl.Buffered(1)),
            pl.BlockSpec((tm, D_MODEL), lambda i: (i, 0)),
            pl.BlockSpec((1, D_MODEL), lambda i: (0, 0)),
            pl.BlockSpec((1, D_MODEL), lambda i: (0, 0)),
        ],
        out_specs=pl.BlockSpec((tm, D_MODEL), lambda i: (i, 0)),
        out_shape=jax.ShapeDtypeStruct((rows, D_MODEL), _F32),
        compiler_params=_params(("arbitrary",), 60),
        name="out",
    )(h_all, w_out_bf16, x, gain, bias)


def _alibi_slopes():
    h = jnp.arange(N_HEADS, dtype=_F32)
    return 2.0 ** (-8.0 * (h + 1.0) / N_HEADS)


def kernel(x_prompt, x_sample, cache_k, cache_v, state_pool, w_in, w_pool, pool_scale, sinks,
           w_out_pool, w_out_attn, w_out, ln_gain, ln_bias):
    depth = w_in.shape[0]
    assert depth == 1
    bp, tp, _ = x_prompt.shape
    bs, ts, _ = x_sample.shape
    slopes = _alibi_slopes()

    l = 0
    w_pool_b = w_pool[l].astype(_BF16)
    scale = pool_scale[l].reshape(1, POOL_WIDTH)
    gain = ln_gain[l].reshape(1, D_MODEL)
    bias = ln_bias[l].reshape(1, D_MODEL)
    sink = sinks[l]

    xp2 = x_prompt.reshape(bp * tp, D_MODEL)
    xs2 = x_sample.reshape(bs * ts, D_MODEL)
    rows_p = bp * tp
    rows_all = rows_p + bs * ts
    x_all, proj = _proj_cast(xp2, w_in[l], 0, rows_all)
    x_all, proj = _proj_cast(xs2, w_in[l], rows_p, rows_all, x_all, proj)
    proj, w_o_b = _proj_main(x_all, w_in[l], proj, w_out[l])
    proj3 = proj.reshape(proj.shape[0] // ts, ts, N_IN)

    b_all, a_all = _mix_prompt(proj, slopes, sink, w_pool_b, scale, bp, tp)
    state_rows_major = jnp.pad(jnp.transpose(state_pool[l], (1, 0, 2)), ((HALO_ROWS - POOL_STATE, 0), (0, 0), (0, 0)))
    state16 = jnp.transpose(state_rows_major, (1, 0, 2))
    ck = cache_k[l].reshape(bs, WINDOW, KV_WIDTH)
    cv = cache_v[l].reshape(bs, WINDOW, KV_WIDTH)
    b_all, new_k_s, new_v_s, a_all, new_pool_s = _mix_sample(
        proj3, ck, cv, state16, slopes, sink, w_pool_b, scale, b_all, a_all, rows_p // ts)

    h_all = _merge(a_all, b_all, proj, w_out_pool[l], w_out_attn[l])
    y_p = _out(h_all, 0, w_o_b, xp2, gain, bias).reshape(bp, tp, D_MODEL)
    y_s = _out(h_all, rows_p, w_o_b, xs2, gain, bias).reshape(bs, ts, D_MODEL)

    def tail(n_rows, col0, width):
        return jnp.stack([lax.slice(proj, ((b + 1) * tp - n_rows, col0), ((b + 1) * tp, col0 + width))
                          for b in range(bp)])

    new_k_p = tail(WINDOW, COL_K, KV_WIDTH).reshape(1, bp, WINDOW, N_KV_HEADS, HEAD_DIM)
    new_v_p = tail(WINDOW, COL_V, KV_WIDTH).reshape(1, bp, WINDOW, N_KV_HEADS, HEAD_DIM)
    new_pool_p = tail(POOL_STATE, COL_U, POOL_WIDTH)[None]
    new_k_s = new_k_s.reshape(1, bs, WINDOW, N_KV_HEADS, HEAD_DIM)
    new_v_s = new_v_s.reshape(1, bs, WINDOW, N_KV_HEADS, HEAD_DIM)

    return (y_p, y_s, new_k_p, new_v_p, new_pool_p, new_k_s, new_v_s, new_pool_s[None])
```

```python
import functools

import jax
import jax.numpy as jnp
from jax import lax
from jax.experimental import pallas as pl
from jax.experimental.pallas import tpu as pltpu

D_MODEL = 4096
POOL_WIDTH = 2048
POOL_WINDOWS = (2, 4, 8, 16)
POOL_GROUP_WIDTH = 512
POOL_STATE = 15
HEAD_DIM = 64
N_HEADS = 32
N_KV_HEADS = 4
GQA_GROUP = 8
ATTN_WIDTH = 2048
KV_WIDTH = 256
WINDOW = 128
PAST_LEN = 8192
N_IN = 16896
COL_U, COL_ZA, COL_Q, COL_K, COL_V, COL_ZB, COL_GA, COL_GB = 0, 2048, 4096, 6144, 6400, 6656, 8704, 12800
DEEPNORM_ALPHA = 2.0 ** 0.25
LN_EPS = 1e-5

V7X_VMEM_BYTES = 64 * 1024 * 1024
SUBLANES = 8
HALO_ROWS = 16
MXU_WIDTH = 256
PROJ_TN = 3 * MXU_WIDTH
PROJ_MAIN_TN = 7 * MXU_WIDTH
MERGE_TN = 4 * MXU_WIDTH
WO_SLICE_ROWS = 32
STREAM_STEPS = 16

_F32 = jnp.float32
_BF16 = jnp.bfloat16


def _params(sem, vmem_mib):
    return pltpu.CompilerParams(dimension_semantics=sem, vmem_limit_bytes=vmem_mib * 1024 * 1024)


def _stream_cast_steps(n_tiles, steps_per_pass):
    last = n_tiles - 1

    def row_blk(j, i):
        return jnp.where(j > 0, i, 0)

    def col_blk(j, i):
        return jnp.maximum(j - 1, 0)

    def chunk_blk(j, i):
        return jnp.where(j <= last, i, steps_per_pass - 1)

    def stage_blk(j, i):
        return jnp.minimum(j, last)

    return row_blk, col_blk, chunk_blk, stage_blk


def _proj_main_kernel(x_ref, wc_ref, wo_ref, proj_in_ref, o_ref, wo_bf16_ref, wb0_ref, wb1_ref):
    del proj_in_ref
    j = pl.program_id(0)
    chunk = wc_ref.shape[0]
    rows = pl.ds(pl.multiple_of(pl.program_id(1) * chunk, chunk), chunk)

    def step(w_next_ref, w_cur_ref):
        wo_bf16_ref[...] = wo_ref[...].astype(_BF16)
        w_next_ref[rows, :] = wc_ref[...].astype(_BF16)
        if w_cur_ref is not None:
            o_ref[...] = jnp.dot(x_ref[...], w_cur_ref[...], preferred_element_type=_F32)

    @pl.when(j == 0)
    def _stage_only():
        step(wb0_ref, None)

    @pl.when((j > 0) & (j % 2 == 0))
    def _even():
        step(wb0_ref, wb1_ref)

    @pl.when(j % 2 == 1)
    def _odd():
        step(wb1_ref, wb0_ref)


def _proj_main(x_bf16, w_f32, proj, w_out_f32, steps=STREAM_STEPS, tn=PROJ_MAIN_TN):
    rows, k = x_bf16.shape
    n = w_f32.shape[1]
    n_tiles = (n - PROJ_TN) // tn
    assert n_tiles * tn + PROJ_TN == n and rows % steps == 0 and k % steps == 0
    tm = rows // steps
    chunk = k // steps
    row_blk, col_blk, chunk_blk, stage_blk = _stream_cast_steps(n_tiles, steps)
    n_slices = w_out_f32.shape[0] // WO_SLICE_ROWS
    assert n_slices <= (n_tiles + 1) * steps
    wo_spec = pl.BlockSpec((WO_SLICE_ROWS, w_out_f32.shape[1]),
                           lambda j, i: (jnp.minimum(j * steps + i, n_slices - 1), 0))
    return pl.pallas_call(
        _proj_main_kernel,
        grid=(n_tiles + 1, steps),
        in_specs=[pl.BlockSpec((tm, k), lambda j, i: (row_blk(j, i), 0)),
                  pl.BlockSpec((chunk, tn), lambda j, i: (chunk_blk(j, i), stage_blk(j, i))),
                  wo_spec,
                  pl.BlockSpec(memory_space=pl.ANY)],
        out_specs=[pl.BlockSpec((tm, tn), lambda j, i: (row_blk(j, i), col_blk(j, i))), wo_spec],
        out_shape=[jax.ShapeDtypeStruct((rows, n), _F32), jax.ShapeDtypeStruct(w_out_f32.shape, _BF16)],
        input_output_aliases={3: 0},
        scratch_shapes=[pltpu.VMEM((k, tn), _BF16), pltpu.VMEM((k, tn), _BF16)],
        compiler_params=_params(("arbitrary", "arbitrary"), 58),
        name="proj",
    )(x_bf16, w_f32, w_out_f32, proj)


def _proj_cast_kernel(x_ref, w_ref, *refs):
    xb_ref, o_ref, wb_ref = refs[-3:]

    @pl.when(pl.program_id(0) == 0)
    def _cast_weight_tile():
        wb_ref[...] = w_ref[...].astype(_BF16)

    xb = x_ref[...].astype(_BF16)
    xb_ref[...] = xb
    o_ref[...] = jnp.dot(xb, wb_ref[...], preferred_element_type=_F32)


def _proj_cast(x_f32, w_f32, row0, total_rows, x_all=None, proj=None, tm=512, tn=PROJ_TN):
    rows, k = x_f32.shape
    n = w_f32.shape[1]
    blk0 = row0 // tm
    col_blk = n // tn - 1
    aliased = [] if x_all is None else [x_all, proj]
    return pl.pallas_call(
        _proj_cast_kernel,
        grid=(rows // tm,),
        in_specs=[pl.BlockSpec((tm, k), lambda i: (i, 0)),
                  pl.BlockSpec((k, tn), lambda i: (0, col_blk), pipeline_mode=pl.Buffered(1))]
                 + [pl.BlockSpec(memory_space=pl.ANY)] * len(aliased),
        out_specs=[pl.BlockSpec((tm, k), lambda i: (blk0 + i, 0)),
                   pl.BlockSpec((tm, tn), lambda i: (blk0 + i, col_blk))],
        out_shape=[jax.ShapeDtypeStruct((total_rows, k), _BF16),
                   jax.ShapeDtypeStruct((total_rows, n), _F32)],
        input_output_aliases={2: 0, 3: 1} if aliased else {},
        scratch_shapes=[pltpu.VMEM((k, tn), _BF16)],
        compiler_params=_params(("arbitrary",), 56),
        name="proj_cast",
    )(x_f32, w_f32, *aliased)


def _window_sums(ext, n_doublings):
    s = ext
    for d in range(n_doublings):
        s = s + pltpu.roll(s, 1 << d, axis=0)
    return s


def _pool_groups(ext, u, z_a, pos, wp_ref, scale_ref, halo):
    outs = []
    for g, w in enumerate(POOL_WINDOWS):
        sl = slice(g * POOL_GROUP_WIDTH, (g + 1) * POOL_GROUP_WIDTH)
        s = _window_sums(ext[:, sl], g + 1)[halo:]
        count = jnp.minimum(w, pos + 1).astype(_F32)
        pooled = s / count - u[:, sl]
        mixed = jnp.dot(pooled.astype(_BF16), wp_ref[g], preferred_element_type=_F32)
        pool_out = mixed * scale_ref[:, sl]
        z = z_a[:, sl]
        outs.append(z * jax.nn.sigmoid(z) * pool_out)
    return outs


def _pool_prompt_tile(t, u_ref, halo_ref, za_ref, wp_ref, scale_ref, a_ref):
    tm = u_ref.shape[0]
    u = u_ref[...]
    halo = jnp.where(t == 0, 0.0, halo_ref[...])
    ext = jnp.concatenate([halo, u], axis=0)
    pos = t * tm + lax.broadcasted_iota(jnp.int32, (tm, 1), 0)
    outs = _pool_groups(ext, u, za_ref[...], pos, wp_ref, scale_ref, HALO_ROWS)
    for g, o in enumerate(outs):
        a_ref[:, g * POOL_GROUP_WIDTH:(g + 1) * POOL_GROUP_WIDTH] = o.astype(_BF16)


def _pool_sample_tile(u_ref, za_ref, st_ref, wp_ref, scale_ref, a_ref, np_ref, bs, t_new):
    u3 = u_ref[...]
    st = st_ref[...]
    rows_per_seq = HALO_ROWS + t_new
    ext = jnp.concatenate([st, u3], axis=1).reshape(bs * rows_per_seq, POOL_WIDTH)
    pos3 = PAST_LEN + lax.broadcasted_iota(jnp.int32, (bs, t_new, 1), 1)
    pos = pos3.reshape(bs * t_new, 1)
    u = u3.reshape(bs * t_new, POOL_WIDTH)
    za = za_ref[...].reshape(bs * t_new, POOL_WIDTH)
    for g, w in enumerate(POOL_WINDOWS):
        sl = slice(g * POOL_GROUP_WIDTH, (g + 1) * POOL_GROUP_WIDTH)
        s = _window_sums(ext[:, sl], g + 1)
        s = s.reshape(bs, rows_per_seq, POOL_GROUP_WIDTH)[:, HALO_ROWS:, :].reshape(bs * t_new, POOL_GROUP_WIDTH)
        count = jnp.minimum(w, pos + 1).astype(_F32)
        pooled = s / count - u[:, sl]
        mixed = jnp.dot(pooled.astype(_BF16), wp_ref[g], preferred_element_type=_F32)
        pool_out = mixed * scale_ref[:, sl]
        z = za[:, sl]
        a_ref[:, sl] = (z * jax.nn.sigmoid(z) * pool_out).astype(_BF16)
    keep = POOL_STATE - t_new
    np_ref[:, 0:keep, :] = st[:, HALO_ROWS - keep:, :]
    np_ref[:, keep:, :] = u3


def _softmax_with_sink(l, sink):
    m = jnp.maximum(jnp.max(l, axis=-1, keepdims=True), sink)
    p = jnp.exp(l - m)
    den = jnp.sum(p, axis=-1, keepdims=True) + jnp.exp(sink - m)
    return p * (1.0 / den)


def _penalty(slope, dist, valid):
    return jnp.where(valid, slope * dist.astype(_F32), jnp.inf)


def _mix_prompt_kernel(slopes_ref, sinks_ref, qkvz_ref, kvp_ref, uz_ref, halo_ref, wp_ref, scale_ref,
                       o_ref, a_ref, pen_ref, sink_ref):
    u_ref = uz_ref.at[:, COL_U:COL_U + POOL_WIDTH]
    za_ref = uz_ref.at[:, COL_ZA:COL_ZA + POOL_WIDTH]
    q_ref = qkvz_ref.at[:, 0:ATTN_WIDTH]
    kvc_ref = qkvz_ref.at[:, COL_K - COL_Q:COL_K - COL_Q + 2 * KV_WIDTH]
    zb_ref = qkvz_ref.at[:, COL_ZB - COL_Q:COL_ZB - COL_Q + ATTN_WIDTH]
    _pool_prompt_tile(pl.program_id(1), u_ref, halo_ref, za_ref, wp_ref, scale_ref, a_ref)
    blk = WINDOW
    gw = GQA_GROUP * HEAD_DIM
    nq = GQA_GROUP * blk
    n = pl.program_id(1)

    @pl.when((pl.program_id(0) == 0) & (n == 0))
    def _fill_tables():
        si = lax.broadcasted_iota(jnp.int32, (2 * blk, blk), 0)
        ti = lax.broadcasted_iota(jnp.int32, (2 * blk, blk), 1)
        dist = blk + ti - si
        for n_tbl in (0, 1):
            key_pos = (n_tbl - 1) * blk + si
            valid = (dist >= 0) & (dist < WINDOW) & (key_pos >= 0)
            for h in range(N_HEADS):
                kh, g = divmod(h, GQA_GROUP)
                pen_ref[n_tbl, kh, :, g * blk:(g + 1) * blk] = _penalty(slopes_ref[h], dist, valid)
        for h in range(N_HEADS):
            kh, g = divmod(h, GQA_GROUP)
            sink_ref[kh, :, g * blk:(g + 1) * blk] = jnp.full((1, blk), sinks_ref[h], _F32)

    tbl = jnp.minimum(n, 1)
    kv = jnp.concatenate([kvp_ref[...], kvc_ref[...]], axis=0)
    k_all = kv[:, 0:KV_WIDTH].astype(_BF16)
    v_t = kv[:, KV_WIDTH:2 * KV_WIDTH].T.astype(_BF16)
    o_t = []
    for kh in range(N_KV_HEADS):
        hd = slice(kh * HEAD_DIM, (kh + 1) * HEAD_DIM)
        q_kh = (q_ref[:, kh * gw:(kh + 1) * gw] * (HEAD_DIM ** -0.5)).astype(_BF16)
        qs = jnp.concatenate([q_kh[:, g * HEAD_DIM:(g + 1) * HEAD_DIM] for g in range(GQA_GROUP)], axis=0)
        logits_t = lax.dot_general(k_all[:, hd], qs, (((1,), (1,)), ((), ())),
                                   preferred_element_type=_F32)
        l = logits_t - pen_ref[tbl, kh]
        sink = sink_ref[kh]
        m = jnp.maximum(jnp.max(l, axis=0, keepdims=True), sink)
        p = jnp.exp(l - m)
        den = jnp.sum(p, axis=0, keepdims=True) + jnp.exp(sink - m)
        pr_t = (p * (1.0 / den)).astype(_BF16)
        o_t.append(jnp.dot(v_t[hd, :], pr_t, preferred_element_type=_F32))
    for pair in range(N_KV_HEADS // 2):
        o_pair = jnp.concatenate([o_t[2 * pair], o_t[2 * pair + 1]], axis=0)
        tiles = [o_pair[:, g * blk:(g + 1) * blk].T for g in range(GQA_GROUP)]
        for j in range(2):
            kh = 2 * pair + j
            o_kh = jnp.concatenate([t[:, j * HEAD_DIM:(j + 1) * HEAD_DIM] for t in tiles], axis=1)
            zb = zb_ref[:, kh * gw:(kh + 1) * gw]
            o_ref[:, kh * gw:(kh + 1) * gw] = (zb * jax.nn.sigmoid(zb) * o_kh).astype(_BF16)


def _mix_prompt(proj, slopes, sinks, w_pool_bf16, pool_scale, batch, seq):
    blk = WINDOW
    nb = seq // blk
    halo_blocks = blk // HALO_ROWS
    smem = pl.BlockSpec(memory_space=pltpu.SMEM)
    rows = proj.shape[0]
    return pl.pallas_call(
        _mix_prompt_kernel,
        grid=(batch, nb),
        in_specs=[
            smem, smem,
            pl.BlockSpec((pl.Element(blk), pl.Element(COL_GA - COL_Q)), lambda b, n: ((b * nb + n) * blk, COL_Q)),
            pl.BlockSpec((blk, 2 * KV_WIDTH), lambda b, n: (b * nb + jnp.maximum(n - 1, 0), COL_K // (2 * KV_WIDTH))),
            pl.BlockSpec((blk, COL_Q - COL_U), lambda b, n: (b * nb + n, 0)),
            pl.BlockSpec((HALO_ROWS, POOL_WIDTH),
                         lambda b, n: (jnp.maximum((b * nb + n) * halo_blocks - 1, 0), COL_U // POOL_WIDTH)),
            pl.BlockSpec((4, POOL_GROUP_WIDTH, POOL_GROUP_WIDTH), lambda b, n: (0, 0, 0)),
            pl.BlockSpec((1, POOL_WIDTH), lambda b, n: (0, 0)),
        ],
        out_specs=[pl.BlockSpec((blk, ATTN_WIDTH), lambda b, n: (b * nb + n, 0)),
                   pl.BlockSpec((blk, POOL_WIDTH), lambda b, n: (b * nb + n, 0))],
        out_shape=[jax.ShapeDtypeStruct((rows, ATTN_WIDTH), _BF16),
                   jax.ShapeDtypeStruct((rows, POOL_WIDTH), _BF16)],
        scratch_shapes=[pltpu.VMEM((2, N_KV_HEADS, 2 * blk, GQA_GROUP * blk), _F32),
                        pltpu.VMEM((N_KV_HEADS, 1, GQA_GROUP * blk), _F32)],
        compiler_params=_params(("arbitrary", "arbitrary"), 48),
        name="mix_prompt",
    )(slopes, sinks, proj, proj, proj, proj, w_pool_bf16, pool_scale)


def _mix_sample_kernel(slopes_ref, sinks_ref, qkvz_ref, ck_ref, cv_ref, uz_ref, st_ref, wp_ref, scale_ref,
                       b_in_ref, a_in_ref, o_ref, nk_ref, nv_ref, a_ref, np_ref, pen_ref, *, bs, t_new):
    del b_in_ref, a_in_ref
    u_ref = uz_ref.at[:, :, COL_U:COL_U + POOL_WIDTH]
    za_ref = uz_ref.at[:, :, COL_ZA:COL_ZA + POOL_WIDTH]
    q_ref = qkvz_ref.at[:, :, 0:ATTN_WIDTH]
    kvn_ref = qkvz_ref.at[:, :, COL_K - COL_Q:COL_K - COL_Q + 2 * KV_WIDTH]
    zb_ref = qkvz_ref.at[:, :, COL_ZB - COL_Q:COL_ZB - COL_Q + ATTN_WIDTH]
    _pool_sample_tile(u_ref, za_ref, st_ref, wp_ref, scale_ref, a_ref, np_ref, bs, t_new)
    s_all = WINDOW + t_new
    gw = GQA_GROUP * HEAD_DIM

    @pl.when(pl.program_id(0) == 0)
    def _fill_penalty_table():
        ti = lax.broadcasted_iota(jnp.int32, (t_new, s_all), 0)
        si = lax.broadcasted_iota(jnp.int32, (t_new, s_all), 1)
        dist = WINDOW + ti - si
        key_pos = PAST_LEN - WINDOW + si
        valid = (dist >= 0) & (dist < WINDOW) & (key_pos >= 0)
        for h in range(N_HEADS):
            pen_ref[h] = _penalty(slopes_ref[h], dist, valid)

    kvn = kvn_ref[...]
    k_all = jnp.concatenate([ck_ref[...], kvn[:, :, 0:KV_WIDTH]], axis=1)
    v_all = jnp.concatenate([cv_ref[...], kvn[:, :, KV_WIDTH:2 * KV_WIDTH]], axis=1)
    nk_ref[...] = k_all[:, t_new:, :]
    nv_ref[...] = v_all[:, t_new:, :]
    kb = k_all.astype(_BF16)
    vb = v_all.astype(_BF16)
    for kh in range(N_KV_HEADS):
        q_kh = (q_ref[:, :, kh * gw:(kh + 1) * gw] * (HEAD_DIM ** -0.5)).astype(_BF16)
        qs = jnp.concatenate([q_kh[:, :, g * HEAD_DIM:(g + 1) * HEAD_DIM] for g in range(GQA_GROUP)], axis=1)
        hd = slice(kh * HEAD_DIM, (kh + 1) * HEAD_DIM)
        logits = jnp.stack(
            [lax.dot_general(qs[b], kb[b, :, hd], (((1,), (1,)), ((), ())), preferred_element_type=_F32)
             for b in range(bs)], axis=0)
        probs = []
        for g in range(GQA_GROUP):
            h = kh * GQA_GROUP + g
            l = logits[:, g * t_new:(g + 1) * t_new, :] - pen_ref[h]
            probs.append(_softmax_with_sink(l, sinks_ref[h]).astype(_BF16))
        pr = jnp.concatenate(probs, axis=1)
        o = jnp.stack([jnp.dot(pr[b], vb[b, :, hd], preferred_element_type=_F32) for b in range(bs)], axis=0)
        o_kh = jnp.concatenate([o[:, g * t_new:(g + 1) * t_new, :] for g in range(GQA_GROUP)], axis=2)
        zb = zb_ref[:, :, kh * gw:(kh + 1) * gw]
        res = (zb * jax.nn.sigmoid(zb) * o_kh).reshape(bs * t_new, gw)
        o_ref[:, kh * gw:(kh + 1) * gw] = res.astype(_BF16)


def _mix_sample(proj3, cache_k, cache_v, state16, slopes, sinks, w_pool_bf16, pool_scale, b_all, a_all, seq0, bs=16):
    nseq = cache_k.shape[0]
    t_new = proj3.shape[1]
    blk0 = seq0 // bs
    smem = pl.BlockSpec(memory_space=pltpu.SMEM)
    alias = pl.BlockSpec(memory_space=pl.ANY)
    kern = functools.partial(_mix_sample_kernel, bs=bs, t_new=t_new)
    cache_spec = pl.BlockSpec((bs, WINDOW, KV_WIDTH), lambda i: (i, 0, 0))
    in_specs = [
        smem, smem,
        pl.BlockSpec((pl.Element(bs), pl.Element(t_new), pl.Element(COL_GA - COL_Q)),
                     lambda i: ((blk0 + i) * bs, 0, COL_Q)),
        cache_spec, cache_spec,
        pl.BlockSpec((bs, t_new, COL_Q - COL_U), lambda i: (blk0 + i, 0, 0)),
        pl.BlockSpec((bs, HALO_ROWS, POOL_WIDTH), lambda i: (i, 0, 0)),
        pl.BlockSpec((4, POOL_GROUP_WIDTH, POOL_GROUP_WIDTH), lambda i: (0, 0, 0)),
        pl.BlockSpec((1, POOL_WIDTH), lambda i: (0, 0)),
        alias, alias,
    ]
    operands = (slopes, sinks, proj3, cache_k, cache_v, proj3, state16, w_pool_bf16, pool_scale, b_all, a_all)
    return pl.pallas_call(
        kern,
        grid=(nseq // bs,),
        in_specs=in_specs,
        out_specs=[
            pl.BlockSpec((bs * t_new, ATTN_WIDTH), lambda i: (blk0 + i, 0)),
            cache_spec, cache_spec,
            pl.BlockSpec((bs * t_new, POOL_WIDTH), lambda i: (blk0 + i, 0)),
            pl.BlockSpec((bs, POOL_STATE, POOL_WIDTH), lambda i: (i, 0, 0)),
        ],
        out_shape=[
            jax.ShapeDtypeStruct(b_all.shape, _BF16),
            jax.ShapeDtypeStruct((nseq, WINDOW, KV_WIDTH), _F32),
            jax.ShapeDtypeStruct((nseq, WINDOW, KV_WIDTH), _F32),
            jax.ShapeDtypeStruct(a_all.shape, _BF16),
            jax.ShapeDtypeStruct((nseq, POOL_STATE, POOL_WIDTH), _F32),
        ],
        input_output_aliases={len(operands) - 2: 0, len(operands) - 1: 3},
        scratch_shapes=[pltpu.VMEM((N_HEADS, t_new, WINDOW + t_new), _F32)],
        compiler_params=_params(("arbitrary",), 48),
        name="mix_sample",
    )(*operands)


def _merge_kernel(a_ref, b_ref, wpc_ref, wac_ref, ga_ref, gb_ref, h_ref,
                  wp0_ref, wp1_ref, wa0_ref, wa1_ref, *, row_chunk):
    j = pl.program_id(0)
    chunk = wpc_ref.shape[0]
    wrows = pl.ds(pl.multiple_of(pl.program_id(1) * chunk, chunk), chunk)

    def step(wp_next_ref, wa_next_ref, wp_cur_ref, wa_cur_ref):
        wp_next_ref[wrows, :] = wpc_ref[...].astype(_BF16)
        wa_next_ref[wrows, :] = wac_ref[...].astype(_BF16)
        if wp_cur_ref is None:
            return
        for r in range(0, a_ref.shape[0], row_chunk):
            rows = pl.ds(r, row_chunk)
            ya = jnp.dot(a_ref[rows, :], wp_cur_ref[...], preferred_element_type=_F32)
            yb = jnp.dot(b_ref[rows, :], wa_cur_ref[...], preferred_element_type=_F32)
            h = jax.nn.sigmoid(ga_ref[rows, :]) * ya + jax.nn.sigmoid(gb_ref[rows, :]) * yb
            h_ref[rows, :] = h.astype(_BF16)

    @pl.when(j == 0)
    def _stage_only():
        step(wp0_ref, wa0_ref, None, None)

    @pl.when((j > 0) & (j % 2 == 0))
    def _even():
        step(wp0_ref, wa0_ref, wp1_ref, wa1_ref)

    @pl.when(j % 2 == 1)
    def _odd():
        step(wp1_ref, wa1_ref, wp0_ref, wa0_ref)


def _merge(a, b, proj, w_out_pool, w_out_attn, steps=STREAM_STEPS, tn=MERGE_TN):
    rows = a.shape[0]
    n_tiles = D_MODEL // tn
    assert rows % steps == 0 and POOL_WIDTH % steps == 0 and ATTN_WIDTH == POOL_WIDTH
    tm = rows // steps
    chunk = POOL_WIDTH // steps
    row_blk, col_blk, chunk_blk, stage_blk = _stream_cast_steps(n_tiles, steps)

    def gate_spec(col0):
        assert col0 % MXU_WIDTH == 0 and tn % MXU_WIDTH == 0 and tm % SUBLANES == 0
        return pl.BlockSpec((pl.Element(tm), pl.Element(tn)),
                            lambda j, i: (pl.multiple_of(row_blk(j, i) * tm, SUBLANES),
                                          pl.multiple_of(col0 + col_blk(j, i) * tn, MXU_WIDTH)))

    return pl.pallas_call(
        functools.partial(_merge_kernel, row_chunk=tm // 2),
        grid=(n_tiles + 1, steps),
        in_specs=[
            pl.BlockSpec((tm, POOL_WIDTH), lambda j, i: (row_blk(j, i), 0)),
            pl.BlockSpec((tm, ATTN_WIDTH), lambda j, i: (row_blk(j, i), 0)),
            pl.BlockSpec((chunk, tn), lambda j, i: (chunk_blk(j, i), stage_blk(j, i))),
            pl.BlockSpec((chunk, tn), lambda j, i: (chunk_blk(j, i), stage_blk(j, i))),
            gate_spec(COL_GA), gate_spec(COL_GB),
        ],
        out_specs=pl.BlockSpec((tm, tn), lambda j, i: (row_blk(j, i), col_blk(j, i))),
        out_shape=jax.ShapeDtypeStruct((rows, D_MODEL), _BF16),
        scratch_shapes=[pltpu.VMEM((POOL_WIDTH, tn), _BF16)] * 2 + [pltpu.VMEM((ATTN_WIDTH, tn), _BF16)] * 2,
        compiler_params=_params(("arbitrary", "arbitrary"), 56),
        name="merge",
    )(a, b, w_out_pool, w_out_attn, proj, proj)


def _out_kernel(h_ref, w_ref, x_ref, gain_ref, bias_ref, y_ref, *, sub):
    gain = gain_ref[...]
    bias = bias_ref[...]
    for r0 in range(0, h_ref.shape[0], sub):
        y_ref[pl.ds(r0, sub), :] = jnp.dot(h_ref[pl.ds(r0, sub), :], w_ref[...], preferred_element_type=_F32)
        for g in range(sub // SUBLANES):
            rows = pl.ds(r0 + g * SUBLANES, SUBLANES)
            y = DEEPNORM_ALPHA * x_ref[rows, :] + y_ref[rows, :]
            mu = jnp.mean(y, axis=-1, keepdims=True)
            d = y - mu
            var = jnp.mean(d * d, axis=-1, keepdims=True)
            y_ref[rows, :] = d * lax.rsqrt(var + LN_EPS) * gain + bias


def _out(h_all, row0, w_out_bf16, x, gain, bias, tm=256, sub=128):
    rows = x.shape[0]
    blk0 = row0 // tm
    return pl.pallas_call(
        functools.partial(_out_kernel, sub=sub),
        grid=(rows // tm,),
        in_specs=[
            pl.BlockSpec((tm, D_MODEL), lambda i: (blk0 + i, 0)),
            pl.BlockSpec((D_MODEL, D_MODEL), lambda i: (0, 0), pipeline_mode=pl.Buffered(1)),
            pl.BlockSpec((tm, D_MODEL), lambda i: (i, 0)),
            pl.BlockSpec((1, D_MODEL), lambda i: (0, 0)),
            pl.BlockSpec((1, D_MODEL), lambda i: (0, 0)),
        ],
        out_specs=pl.BlockSpec((tm, D_MODEL), lambda i: (i, 0)),
        out_shape=jax.ShapeDtypeStruct((rows, D_MODEL), _F32),
        compiler_params=_params(("arbitrary",), 60),
        name="out",
    )(h_all, w_out_bf16, x, gain, bias)


def _alibi_slopes():
    h = jnp.arange(N_HEADS, dtype=_F32)
    return 2.0 ** (-8.0 * (h + 1.0) / N_HEADS)


def kernel(x_prompt, x_sample, cache_k, cache_v, state_pool, w_in, w_pool, pool_scale, sinks,
           w_out_pool, w_out_attn, w_out, ln_gain, ln_bias):
    depth = w_in.shape[0]
    assert depth == 1
    bp, tp, _ = x_prompt.shape
    bs, ts, _ = x_sample.shape
    slopes = _alibi_slopes()

    l = 0
    w_pool_b = w_pool[l].astype(_BF16)
    scale = pool_scale[l].reshape(1, POOL_WIDTH)
    gain = ln_gain[l].reshape(1, D_MODEL)
    bias = ln_bias[l].reshape(1, D_MODEL)
    sink = sinks[l]

    xp2 = x_prompt.reshape(bp * tp, D_MODEL)
    xs2 = x_sample.reshape(bs * ts, D_MODEL)
    rows_p = bp * tp
    rows_all = rows_p + bs * ts
    x_all, proj = _proj_cast(xp2, w_in[l], 0, rows_all)
    x_all, proj = _proj_cast(xs2, w_in[l], rows_p, rows_all, x_all, proj)
    proj, w_o_b = _proj_main(x_all, w_in[l], proj, w_out[l])
    proj3 = proj.reshape(proj.shape[0] // ts, ts, N_IN)

    b_all, a_all = _mix_prompt(proj, slopes, sink, w_pool_b, scale, bp, tp)
    state_rows_major = jnp.pad(jnp.transpose(state_pool[l], (1, 0, 2)), ((HALO_ROWS - POOL_STATE, 0), (0, 0), (0, 0)))
    state16 = jnp.transpose(state_rows_major, (1, 0, 2))
    ck = cache_k[l].reshape(bs, WINDOW, KV_WIDTH)
    cv = cache_v[l].reshape(bs, WINDOW, KV_WIDTH)
    b_all, new_k_s, new_v_s, a_all, new_pool_s = _mix_sample(
        proj3, ck, cv, state16, slopes, sink, w_pool_b, scale, b_all, a_all, rows_p // ts)

    h_all = _merge(a_all, b_all, proj, w_out_pool[l], w_out_attn[l])
    y_p = _out(h_all, 0, w_o_b, xp2, gain, bias).reshape(bp, tp, D_MODEL)
    y_s = _out(h_all, rows_p, w_o_b, xs2, gain, bias).reshape(bs, ts, D_MODEL)

    def tail(n_rows, col0, width):
        return jnp.stack([lax.slice(proj, ((b + 1) * tp - n_rows, col0), ((b + 1) * tp, col0 + width))
                          for b in range(bp)])

    new_k_p = tail(WINDOW, COL_K, KV_WIDTH).reshape(1, bp, WINDOW, N_KV_HEADS, HEAD_DIM)
    new_v_p = tail(WINDOW, COL_V, KV_WIDTH).reshape(1, bp, WINDOW, N_KV_HEADS, HEAD_DIM)
    new_pool_p = tail(POOL_STATE, COL_U, POOL_WIDTH)[None]
    new_k_s = new_k_s.reshape(1, bs, WINDOW, N_KV_HEADS, HEAD_DIM)
    new_v_s = new_v_s.reshape(1, bs, WINDOW, N_KV_HEADS, HEAD_DIM)

    return (y_p, y_s, new_k_p, new_v_p, new_pool_p, new_k_s, new_v_s, new_pool_s[None])
```

```python
import functools

import jax
import jax.numpy as jnp
from jax import lax
from jax.experimental import pallas as pl
from jax.experimental.pallas import tpu as pltpu

D_MODEL = 4096
POOL_WIDTH = 2048
POOL_WINDOWS = (2, 4, 8, 16)
POOL_GROUP_WIDTH = 512
POOL_STATE = 15
HEAD_DIM = 64
N_HEADS = 32
N_KV_HEADS = 4
GQA_GROUP = 8
ATTN_WIDTH = 2048
KV_WIDTH = 256
WINDOW = 128
PAST_LEN = 8192
N_IN = 16896
COL_U, COL_ZA, COL_Q, COL_K, COL_V, COL_ZB, COL_GA, COL_GB = 0, 2048, 4096, 6144, 6400, 6656, 8704, 12800
DEEPNORM_ALPHA = 2.0 ** 0.25
LN_EPS = 1e-5

V7X_VMEM_BYTES = 64 * 1024 * 1024
SUBLANES = 8
HALO_ROWS = 16
MXU_WIDTH = 256
PROJ_TN = 3 * MXU_WIDTH
PROJ_MAIN_TN = 7 * MXU_WIDTH
MERGE_TN = 4 * MXU_WIDTH
WO_SLICE_ROWS = 32
STREAM_STEPS = 16

_F32 = jnp.float32
_BF16 = jnp.bfloat16


def _params(sem, vmem_mib):
    return pltpu.CompilerParams(dimension_semantics=sem, vmem_limit_bytes=vmem_mib * 1024 * 1024)


def _stream_cast_steps(n_tiles, steps_per_pass):
    last = n_tiles - 1

    def row_blk(j, i):
        return jnp.where(j > 0, i, 0)

    def col_blk(j, i):
        return jnp.maximum(j - 1, 0)

    def chunk_blk(j, i):
        return jnp.where(j <= last, i, steps_per_pass - 1)

    def stage_blk(j, i):
        return jnp.minimum(j, last)

    return row_blk, col_blk, chunk_blk, stage_blk


def _proj_main_kernel(x_ref, wc_ref, wo_ref, ckt_ref, cvt_ref, proj_in_ref,
                      o_ref, wo_bf16_ref, ck_ref, cv_ref, wb0_ref, wb1_ref):
    del proj_in_ref
    j = pl.program_id(0)
    chunk = wc_ref.shape[0]
    rows = pl.ds(pl.multiple_of(pl.program_id(1) * chunk, chunk), chunk)

    def step(w_next_ref, w_cur_ref):
        wo_bf16_ref[...] = wo_ref[...].astype(_BF16)
        ck_ref[0] = ckt_ref[0].T
        cv_ref[0] = cvt_ref[0].T
        w_next_ref[rows, :] = wc_ref[...].astype(_BF16)
        if w_cur_ref is not None:
            o_ref[...] = jnp.dot(x_ref[...], w_cur_ref[...], preferred_element_type=_F32)

    @pl.when(j == 0)
    def _stage_only():
        step(wb0_ref, None)

    @pl.when((j > 0) & (j % 2 == 0))
    def _even():
        step(wb0_ref, wb1_ref)

    @pl.when(j % 2 == 1)
    def _odd():
        step(wb1_ref, wb0_ref)


def _proj_main(x_bf16, w_f32, proj, w_out_f32, cache_kt, cache_vt, steps=STREAM_STEPS, tn=PROJ_MAIN_TN):
    rows, k = x_bf16.shape
    n = w_f32.shape[1]
    n_tiles = (n - PROJ_TN) // tn
    assert n_tiles * tn + PROJ_TN == n and rows % steps == 0 and k % steps == 0
    tm = rows // steps
    chunk = k // steps
    total_steps = (n_tiles + 1) * steps
    row_blk, col_blk, chunk_blk, stage_blk = _stream_cast_steps(n_tiles, steps)
    n_slices = w_out_f32.shape[0] // WO_SLICE_ROWS
    nseq, feat, win = cache_kt.shape
    assert n_slices <= total_steps and nseq <= total_steps
    wo_spec = pl.BlockSpec((WO_SLICE_ROWS, w_out_f32.shape[1]),
                           lambda j, i: (jnp.minimum(j * steps + i, n_slices - 1), 0))

    def seq_spec(d1, d2):
        return pl.BlockSpec((1, d1, d2), lambda j, i: (jnp.minimum(j * steps + i, nseq - 1), 0, 0))

    return pl.pallas_call(
        _proj_main_kernel,
        grid=(n_tiles + 1, steps),
        in_specs=[pl.BlockSpec((tm, k), lambda j, i: (row_blk(j, i), 0)),
                  pl.BlockSpec((chunk, tn), lambda j, i: (chunk_blk(j, i), stage_blk(j, i))),
                  wo_spec, seq_spec(feat, win), seq_spec(feat, win),
                  pl.BlockSpec(memory_space=pl.ANY)],
        out_specs=[pl.BlockSpec((tm, tn), lambda j, i: (row_blk(j, i), col_blk(j, i))), wo_spec,
                   seq_spec(win, feat), seq_spec(win, feat)],
        out_shape=[jax.ShapeDtypeStruct((rows, n), _F32), jax.ShapeDtypeStruct(w_out_f32.shape, _BF16),
                   jax.ShapeDtypeStruct((nseq, win, feat), _F32), jax.ShapeDtypeStruct((nseq, win, feat), _F32)],
        input_output_aliases={5: 0},
        scratch_shapes=[pltpu.VMEM((k, tn), _BF16), pltpu.VMEM((k, tn), _BF16)],
        compiler_params=_params(("arbitrary", "arbitrary"), 58),
        name="proj",
    )(x_bf16, w_f32, w_out_f32, cache_kt, cache_vt, proj)


def _proj_cast_kernel(x_ref, w_ref, *refs):
    xb_ref, o_ref, wb_ref = refs[-3:]

    @pl.when(pl.program_id(0) == 0)
    def _cast_weight_tile():
        wb_ref[...] = w_ref[...].astype(_BF16)

    xb = x_ref[...].astype(_BF16)
    xb_ref[...] = xb
    o_ref[...] = jnp.dot(xb, wb_ref[...], preferred_element_type=_F32)


def _proj_cast(x_f32, w_f32, row0, total_rows, x_all=None, proj=None, tm=512, tn=PROJ_TN):
    rows, k = x_f32.shape
    n = w_f32.shape[1]
    blk0 = row0 // tm
    col_blk = n // tn - 1
    aliased = [] if x_all is None else [x_all, proj]
    return pl.pallas_call(
        _proj_cast_kernel,
        grid=(rows // tm,),
        in_specs=[pl.BlockSpec((tm, k), lambda i: (i, 0)),
                  pl.BlockSpec((k, tn), lambda i: (0, col_blk), pipeline_mode=pl.Buffered(1))]
                 + [pl.BlockSpec(memory_space=pl.ANY)] * len(aliased),
        out_specs=[pl.BlockSpec((tm, k), lambda i: (blk0 + i, 0)),
                   pl.BlockSpec((tm, tn), lambda i: (blk0 + i, col_blk))],
        out_shape=[jax.ShapeDtypeStruct((total_rows, k), _BF16),
                   jax.ShapeDtypeStruct((total_rows, n), _F32)],
        input_output_aliases={2: 0, 3: 1} if aliased else {},
        scratch_shapes=[pltpu.VMEM((k, tn), _BF16)],
        compiler_params=_params(("arbitrary",), 56),
        name="proj_cast",
    )(x_f32, w_f32, *aliased)


def _window_sums(ext, n_doublings):
    s = ext
    for d in range(n_doublings):
        s = s + pltpu.roll(s, 1 << d, axis=0)
    return s


def _pool_groups(ext, u, z_a, pos, wp_ref, scale_ref, halo):
    outs = []
    for g, w in enumerate(POOL_WINDOWS):
        sl = slice(g * POOL_GROUP_WIDTH, (g + 1) * POOL_GROUP_WIDTH)
        s = _window_sums(ext[:, sl], g + 1)[halo:]
        count = jnp.minimum(w, pos + 1).astype(_F32)
        pooled = s / count - u[:, sl]
        mixed = jnp.dot(pooled.astype(_BF16), wp_ref[g], preferred_element_type=_F32)
        pool_out = mixed * scale_ref[:, sl]
        z = z_a[:, sl]
        outs.append(z * jax.nn.sigmoid(z) * pool_out)
    return outs


def _pool_prompt_tile(t, u_ref, halo_ref, za_ref, wp_ref, scale_ref, a_ref):
    tm = u_ref.shape[0]
    u = u_ref[...]
    halo = jnp.where(t == 0, 0.0, halo_ref[...])
    ext = jnp.concatenate([halo, u], axis=0)
    pos = t * tm + lax.broadcasted_iota(jnp.int32, (tm, 1), 0)
    outs = _pool_groups(ext, u, za_ref[...], pos, wp_ref, scale_ref, HALO_ROWS)
    for g, o in enumerate(outs):
        a_ref[:, g * POOL_GROUP_WIDTH:(g + 1) * POOL_GROUP_WIDTH] = o.astype(_BF16)


def _pool_sample_tile(u_ref, za_ref, stt_ref, wp_ref, scale_ref, a_ref, np_ref, st_ref, bs, t_new):
    st_ref[:, 0, :] = jnp.zeros((bs, POOL_WIDTH), _F32)
    for t in range(POOL_STATE):
        st_ref[:, t + 1, :] = stt_ref[t]
    u3 = u_ref[...]
    st = st_ref[...]
    rows_per_seq = HALO_ROWS + t_new
    ext = jnp.concatenate([st, u3], axis=1).reshape(bs * rows_per_seq, POOL_WIDTH)
    pos3 = PAST_LEN + lax.broadcasted_iota(jnp.int32, (bs, t_new, 1), 1)
    pos = pos3.reshape(bs * t_new, 1)
    u = u3.reshape(bs * t_new, POOL_WIDTH)
    za = za_ref[...].reshape(bs * t_new, POOL_WIDTH)
    for g, w in enumerate(POOL_WINDOWS):
        sl = slice(g * POOL_GROUP_WIDTH, (g + 1) * POOL_GROUP_WIDTH)
        s = _window_sums(ext[:, sl], g + 1)
        s = s.reshape(bs, rows_per_seq, POOL_GROUP_WIDTH)[:, HALO_ROWS:, :].reshape(bs * t_new, POOL_GROUP_WIDTH)
        count = jnp.minimum(w, pos + 1).astype(_F32)
        pooled = s / count - u[:, sl]
        mixed = jnp.dot(pooled.astype(_BF16), wp_ref[g], preferred_element_type=_F32)
        pool_out = mixed * scale_ref[:, sl]
        z = za[:, sl]
        a_ref[:, sl] = (z * jax.nn.sigmoid(z) * pool_out).astype(_BF16)
    keep = POOL_STATE - t_new
    np_ref[:, 0:keep, :] = st[:, HALO_ROWS - keep:, :]
    np_ref[:, keep:, :] = u3


def _softmax_with_sink(l, sink):
    m = jnp.maximum(jnp.max(l, axis=-1, keepdims=True), sink)
    p = jnp.exp(l - m)
    den = jnp.sum(p, axis=-1, keepdims=True) + jnp.exp(sink - m)
    return p * (1.0 / den)


def _penalty(slope, dist, valid):
    return jnp.where(valid, slope * dist.astype(_F32), jnp.inf)


def _mix_prompt_kernel(slopes_ref, sinks_ref, qkvz_ref, kvp_ref, uz_ref, halo_ref, wp_ref, scale_ref,
                       o_ref, a_ref, pen_ref, sink_ref):
    u_ref = uz_ref.at[:, COL_U:COL_U + POOL_WIDTH]
    za_ref = uz_ref.at[:, COL_ZA:COL_ZA + POOL_WIDTH]
    q_ref = qkvz_ref.at[:, 0:ATTN_WIDTH]
    kvc_ref = qkvz_ref.at[:, COL_K - COL_Q:COL_K - COL_Q + 2 * KV_WIDTH]
    zb_ref = qkvz_ref.at[:, COL_ZB - COL_Q:COL_ZB - COL_Q + ATTN_WIDTH]
    _pool_prompt_tile(pl.program_id(1), u_ref, halo_ref, za_ref, wp_ref, scale_ref, a_ref)
    blk = WINDOW
    gw = GQA_GROUP * HEAD_DIM
    nq = GQA_GROUP * blk
    n = pl.program_id(1)

    @pl.when((pl.program_id(0) == 0) & (n == 0))
    def _fill_tables():
        si = lax.broadcasted_iota(jnp.int32, (2 * blk, blk), 0)
        ti = lax.broadcasted_iota(jnp.int32, (2 * blk, blk), 1)
        dist = blk + ti - si
        for n_tbl in (0, 1):
            key_pos = (n_tbl - 1) * blk + si
            valid = (dist >= 0) & (dist < WINDOW) & (key_pos >= 0)
            for h in range(N_HEADS):
                kh, g = divmod(h, GQA_GROUP)
                pen_ref[n_tbl, kh, :, g * blk:(g + 1) * blk] = _penalty(slopes_ref[h], dist, valid)
        for h in range(N_HEADS):
            kh, g = divmod(h, GQA_GROUP)
            sink_ref[kh, :, g * blk:(g + 1) * blk] = jnp.full((1, blk), sinks_ref[h], _F32)

    tbl = jnp.minimum(n, 1)
    kv = jnp.concatenate([kvp_ref[...], kvc_ref[...]], axis=0)
    k_all = kv[:, 0:KV_WIDTH].astype(_BF16)
    v_t = kv[:, KV_WIDTH:2 * KV_WIDTH].T.astype(_BF16)
    o_t = []
    for kh in range(N_KV_HEADS):
        hd = slice(kh * HEAD_DIM, (kh + 1) * HEAD_DIM)
        q_kh = (q_ref[:, kh * gw:(kh + 1) * gw] * (HEAD_DIM ** -0.5)).astype(_BF16)
        qs = jnp.concatenate([q_kh[:, g * HEAD_DIM:(g + 1) * HEAD_DIM] for g in range(GQA_GROUP)], axis=0)
        logits_t = lax.dot_general(k_all[:, hd], qs, (((1,), (1,)), ((), ())),
                                   preferred_element_type=_F32)
        l = logits_t - pen_ref[tbl, kh]
        sink = sink_ref[kh]
        m = jnp.maximum(jnp.max(l, axis=0, keepdims=True), sink)
        p = jnp.exp(l - m)
        den = jnp.sum(p, axis=0, keepdims=True) + jnp.exp(sink - m)
        pr_t = (p * (1.0 / den)).astype(_BF16)
        o_t.append(jnp.dot(v_t[hd, :], pr_t, preferred_element_type=_F32))
    for pair in range(N_KV_HEADS // 2):
        o_pair = jnp.concatenate([o_t[2 * pair], o_t[2 * pair + 1]], axis=0)
        tiles = [o_pair[:, g * blk:(g + 1) * blk].T for g in range(GQA_GROUP)]
        for j in range(2):
            kh = 2 * pair + j
            o_kh = jnp.concatenate([t[:, j * HEAD_DIM:(j + 1) * HEAD_DIM] for t in tiles], axis=1)
            zb = zb_ref[:, kh * gw:(kh + 1) * gw]
            o_ref[:, kh * gw:(kh + 1) * gw] = (zb * jax.nn.sigmoid(zb) * o_kh).astype(_BF16)


def _mix_prompt(proj, slopes, sinks, w_pool_bf16, pool_scale, batch, seq):
    blk = WINDOW
    nb = seq // blk
    halo_blocks = blk // HALO_ROWS
    smem = pl.BlockSpec(memory_space=pltpu.SMEM)
    rows = proj.shape[0]
    return pl.pallas_call(
        _mix_prompt_kernel,
        grid=(batch, nb),
        in_specs=[
            smem, smem,
            pl.BlockSpec((pl.Element(blk), pl.Element(COL_GA - COL_Q)), lambda b, n: ((b * nb + n) * blk, COL_Q)),
            pl.BlockSpec((blk, 2 * KV_WIDTH), lambda b, n: (b * nb + jnp.maximum(n - 1, 0), COL_K // (2 * KV_WIDTH))),
            pl.BlockSpec((blk, COL_Q - COL_U), lambda b, n: (b * nb + n, 0)),
            pl.BlockSpec((HALO_ROWS, POOL_WIDTH),
                         lambda b, n: (jnp.maximum((b * nb + n) * halo_blocks - 1, 0), COL_U // POOL_WIDTH)),
            pl.BlockSpec((4, POOL_GROUP_WIDTH, POOL_GROUP_WIDTH), lambda b, n: (0, 0, 0)),
            pl.BlockSpec((1, POOL_WIDTH), lambda b, n: (0, 0)),
        ],
        out_specs=[pl.BlockSpec((blk, ATTN_WIDTH), lambda b, n: (b * nb + n, 0)),
                   pl.BlockSpec((blk, POOL_WIDTH), lambda b, n: (b * nb + n, 0))],
        out_shape=[jax.ShapeDtypeStruct((rows, ATTN_WIDTH), _BF16),
                   jax.ShapeDtypeStruct((rows, POOL_WIDTH), _BF16)],
        scratch_shapes=[pltpu.VMEM((2, N_KV_HEADS, 2 * blk, GQA_GROUP * blk), _F32),
                        pltpu.VMEM((N_KV_HEADS, 1, GQA_GROUP * blk), _F32)],
        compiler_params=_params(("arbitrary", "arbitrary"), 48),
        name="mix_prompt",
    )(slopes, sinks, proj, proj, proj, proj, w_pool_bf16, pool_scale)


def _mix_sample_kernel(slopes_ref, sinks_ref, qkvz_ref, ck_ref, cv_ref, uz_ref, stt_ref, wp_ref, scale_ref,
                       b_in_ref, a_in_ref, o_ref, nk_ref, nv_ref, a_ref, np_ref, pen_ref, st_ref, *, bs, t_new):
    del b_in_ref, a_in_ref
    u_ref = uz_ref.at[:, :, COL_U:COL_U + POOL_WIDTH]
    za_ref = uz_ref.at[:, :, COL_ZA:COL_ZA + POOL_WIDTH]
    q_ref = qkvz_ref.at[:, :, 0:ATTN_WIDTH]
    kvn_ref = qkvz_ref.at[:, :, COL_K - COL_Q:COL_K - COL_Q + 2 * KV_WIDTH]
    zb_ref = qkvz_ref.at[:, :, COL_ZB - COL_Q:COL_ZB - COL_Q + ATTN_WIDTH]
    _pool_sample_tile(u_ref, za_ref, stt_ref, wp_ref, scale_ref, a_ref, np_ref, st_ref, bs, t_new)
    s_all = WINDOW + t_new
    gw = GQA_GROUP * HEAD_DIM

    @pl.when(pl.program_id(0) == 0)
    def _fill_penalty_table():
        ti = lax.broadcasted_iota(jnp.int32, (t_new, s_all), 0)
        si = lax.broadcasted_iota(jnp.int32, (t_new, s_all), 1)
        dist = WINDOW + ti - si
        key_pos = PAST_LEN - WINDOW + si
        valid = (dist >= 0) & (dist < WINDOW) & (key_pos >= 0)
        for h in range(N_HEADS):
            pen_ref[h] = _penalty(slopes_ref[h], dist, valid)

    kvn = kvn_ref[...]
    k_all = jnp.concatenate([ck_ref[...], kvn[:, :, 0:KV_WIDTH]], axis=1)
    v_all = jnp.concatenate([cv_ref[...], kvn[:, :, KV_WIDTH:2 * KV_WIDTH]], axis=1)
    nk_ref[...] = k_all[:, t_new:, :]
    nv_ref[...] = v_all[:, t_new:, :]
    kb = k_all.astype(_BF16)
    vb = v_all.astype(_BF16)
    for kh in range(N_KV_HEADS):
        q_kh = (q_ref[:, :, kh * gw:(kh + 1) * gw] * (HEAD_DIM ** -0.5)).astype(_BF16)
        qs = jnp.concatenate([q_kh[:, :, g * HEAD_DIM:(g + 1) * HEAD_DIM] for g in range(GQA_GROUP)], axis=1)
        hd = slice(kh * HEAD_DIM, (kh + 1) * HEAD_DIM)
        logits = jnp.stack(
            [lax.dot_general(qs[b], kb[b, :, hd], (((1,), (1,)), ((), ())), preferred_element_type=_F32)
             for b in range(bs)], axis=0)
        probs = []
        for g in range(GQA_GROUP):
            h = kh * GQA_GROUP + g
            l = logits[:, g * t_new:(g + 1) * t_new, :] - pen_ref[h]
            probs.append(_softmax_with_sink(l, sinks_ref[h]).astype(_BF16))
        pr = jnp.concatenate(probs, axis=1)
        o = jnp.stack([jnp.dot(pr[b], vb[b, :, hd], preferred_element_type=_F32) for b in range(bs)], axis=0)
        o_kh = jnp.concatenate([o[:, g * t_new:(g + 1) * t_new, :] for g in range(GQA_GROUP)], axis=2)
        zb = zb_ref[:, :, kh * gw:(kh + 1) * gw]
        res = (zb * jax.nn.sigmoid(zb) * o_kh).reshape(bs * t_new, gw)
        o_ref[:, kh * gw:(kh + 1) * gw] = res.astype(_BF16)


def _mix_sample(proj3, cache_k, cache_v, state_t, slopes, sinks, w_pool_bf16, pool_scale, b_all, a_all, seq0, bs=16):
    nseq = cache_k.shape[0]
    t_new = proj3.shape[1]
    blk0 = seq0 // bs
    smem = pl.BlockSpec(memory_space=pltpu.SMEM)
    alias = pl.BlockSpec(memory_space=pl.ANY)
    kern = functools.partial(_mix_sample_kernel, bs=bs, t_new=t_new)
    cache_spec = pl.BlockSpec((bs, WINDOW, KV_WIDTH), lambda i: (i, 0, 0))
    in_specs = [
        smem, smem,
        pl.BlockSpec((pl.Element(bs), pl.Element(t_new), pl.Element(COL_GA - COL_Q)),
                     lambda i: ((blk0 + i) * bs, 0, COL_Q)),
        cache_spec, cache_spec,
        pl.BlockSpec((bs, t_new, COL_Q - COL_U), lambda i: (blk0 + i, 0, 0)),
        pl.BlockSpec((POOL_STATE, bs, POOL_WIDTH), lambda i: (0, i, 0)),
        pl.BlockSpec((4, POOL_GROUP_WIDTH, POOL_GROUP_WIDTH), lambda i: (0, 0, 0)),
        pl.BlockSpec((1, POOL_WIDTH), lambda i: (0, 0)),
        alias, alias,
    ]
    operands = (slopes, sinks, proj3, cache_k, cache_v, proj3, state_t, w_pool_bf16, pool_scale, b_all, a_all)
    return pl.pallas_call(
        kern,
        grid=(nseq // bs,),
        in_specs=in_specs,
        out_specs=[
            pl.BlockSpec((bs * t_new, ATTN_WIDTH), lambda i: (blk0 + i, 0)),
            cache_spec, cache_spec,
            pl.BlockSpec((bs * t_new, POOL_WIDTH), lambda i: (blk0 + i, 0)),
            pl.BlockSpec((bs, POOL_STATE, POOL_WIDTH), lambda i: (i, 0, 0)),
        ],
        out_shape=[
            jax.ShapeDtypeStruct(b_all.shape, _BF16),
            jax.ShapeDtypeStruct((nseq, WINDOW, KV_WIDTH), _F32),
            jax.ShapeDtypeStruct((nseq, WINDOW, KV_WIDTH), _F32),
            jax.ShapeDtypeStruct(a_all.shape, _BF16),
            jax.ShapeDtypeStruct((nseq, POOL_STATE, POOL_WIDTH), _F32),
        ],
        input_output_aliases={len(operands) - 2: 0, len(operands) - 1: 3},
        scratch_shapes=[pltpu.VMEM((N_HEADS, t_new, WINDOW + t_new), _F32),
                        pltpu.VMEM((bs, HALO_ROWS, POOL_WIDTH), _F32)],
        compiler_params=_params(("arbitrary",), 48),
        name="mix_sample",
    )(*operands)


def _merge_kernel(a_ref, b_ref, wpc_ref, wac_ref, ga_ref, gb_ref, h_ref,
                  wp0_ref, wp1_ref, wa0_ref, wa1_ref, *, row_chunk):
    j = pl.program_id(0)
    chunk = wpc_ref.shape[0]
    wrows = pl.ds(pl.multiple_of(pl.program_id(1) * chunk, chunk), chunk)

    def step(wp_next_ref, wa_next_ref, wp_cur_ref, wa_cur_ref):
        wp_next_ref[wrows, :] = wpc_ref[...].astype(_BF16)
        wa_next_ref[wrows, :] = wac_ref[...].astype(_BF16)
        if wp_cur_ref is None:
            return
        for r in range(0, a_ref.shape[0], row_chunk):
            rows = pl.ds(r, row_chunk)
            ya = jnp.dot(a_ref[rows, :], wp_cur_ref[...], preferred_element_type=_F32)
            yb = jnp.dot(b_ref[rows, :], wa_cur_ref[...], preferred_element_type=_F32)
            h = jax.nn.sigmoid(ga_ref[rows, :]) * ya + jax.nn.sigmoid(gb_ref[rows, :]) * yb
            h_ref[rows, :] = h.astype(_BF16)

    @pl.when(j == 0)
    def _stage_only():
        step(wp0_ref, wa0_ref, None, None)

    @pl.when((j > 0) & (j % 2 == 0))
    def _even():
        step(wp0_ref, wa0_ref, wp1_ref, wa1_ref)

    @pl.when(j % 2 == 1)
    def _odd():
        step(wp1_ref, wa1_ref, wp0_ref, wa0_ref)


def _merge(a, b, proj, w_out_pool, w_out_attn, steps=STREAM_STEPS, tn=MERGE_TN):
    rows = a.shape[0]
    n_tiles = D_MODEL // tn
    assert rows % steps == 0 and POOL_WIDTH % steps == 0 and ATTN_WIDTH == POOL_WIDTH
    tm = rows // steps
    chunk = POOL_WIDTH // steps
    row_blk, col_blk, chunk_blk, stage_blk = _stream_cast_steps(n_tiles, steps)

    def gate_spec(col0):
        assert col0 % MXU_WIDTH == 0 and tn % MXU_WIDTH == 0 and tm % SUBLANES == 0
        return pl.BlockSpec((pl.Element(tm), pl.Element(tn)),
                            lambda j, i: (pl.multiple_of(row_blk(j, i) * tm, SUBLANES),
                                          pl.multiple_of(col0 + col_blk(j, i) * tn, MXU_WIDTH)))

    return pl.pallas_call(
        functools.partial(_merge_kernel, row_chunk=tm // 2),
        grid=(n_tiles + 1, steps),
        in_specs=[
            pl.BlockSpec((tm, POOL_WIDTH), lambda j, i: (row_blk(j, i), 0)),
            pl.BlockSpec((tm, ATTN_WIDTH), lambda j, i: (row_blk(j, i), 0)),
            pl.BlockSpec((chunk, tn), lambda j, i: (chunk_blk(j, i), stage_blk(j, i))),
            pl.BlockSpec((chunk, tn), lambda j, i: (chunk_blk(j, i), stage_blk(j, i))),
            gate_spec(COL_GA), gate_spec(COL_GB),
        ],
        out_specs=pl.BlockSpec((tm, tn), lambda j, i: (row_blk(j, i), col_blk(j, i))),
        out_shape=jax.ShapeDtypeStruct((rows, D_MODEL), _BF16),
        scratch_shapes=[pltpu.VMEM((POOL_WIDTH, tn), _BF16)] * 2 + [pltpu.VMEM((ATTN_WIDTH, tn), _BF16)] * 2,
        compiler_params=_params(("arbitrary", "arbitrary"), 56),
        name="merge",
    )(a, b, w_out_pool, w_out_attn, proj, proj)


def _out_kernel(h_ref, w_ref, x_ref, gain_ref, bias_ref, y_ref, *, sub):
    gain = gain_ref[...]
    bias = bias_ref[...]
    for r0 in range(0, h_ref.shape[0], sub):
        y_ref[pl.ds(r0, sub), :] = jnp.dot(h_ref[pl.ds(r0, sub), :], w_ref[...], preferred_element_type=_F32)
        for g in range(sub // SUBLANES):
            rows = pl.ds(r0 + g * SUBLANES, SUBLANES)
            y = DEEPNORM_ALPHA * x_ref[rows, :] + y_ref[rows, :]
            mu = jnp.mean(y, axis=-1, keepdims=True)
            d = y - mu
            var = jnp.mean(d * d, axis=-1, keepdims=True)
            y_ref[rows, :] = d * lax.rsqrt(var + LN_EPS) * gain + bias


def _out(h_all, row0, w_out_bf16, x, gain, bias, tm=256, sub=128):
    rows = x.shape[0]
    blk0 = row0 // tm
    return pl.pallas_call(
        functools.partial(_out_kernel, sub=sub),
        grid=(rows // tm,),
        in_specs=[
            pl.BlockSpec((tm, D_MODEL), lambda i: (blk0 + i, 0)),
            pl.BlockSpec((D_MODEL, D_MODEL), lambda i: (0, 0), pipeline_mode=pl.Buffered(1)),
            pl.BlockSpec((tm, D_MODEL), lambda i: (i, 0)),
            pl.BlockSpec((1, D_MODEL), lambda i: (0, 0)),
            pl.BlockSpec((1, D_MODEL), lambda i: (0, 0)),
        ],
        out_specs=pl.BlockSpec((tm, D_MODEL), lambda i: (i, 0)),
        out_shape=jax.ShapeDtypeStruct((rows, D_MODEL), _F32),
        compiler_params=_params(("arbitrary",), 60),
        name="out",
    )(h_all, w_out_bf16, x, gain, bias)


def _alibi_slopes():
    h = jnp.arange(N_HEADS, dtype=_F32)
    return 2.0 ** (-8.0 * (h + 1.0) / N_HEADS)


def kernel(x_prompt, x_sample, cache_k, cache_v, state_pool, w_in, w_pool, pool_scale, sinks,
           w_out_pool, w_out_attn, w_out, ln_gain, ln_bias):
    depth = w_in.shape[0]
    assert depth == 1
    bp, tp, _ = x_prompt.shape
    bs, ts, _ = x_sample.shape
    slopes = _alibi_slopes()

    l = 0
    w_pool_b = w_pool[l].astype(_BF16)
    scale = pool_scale[l].reshape(1, POOL_WIDTH)
    gain = ln_gain[l].reshape(1, D_MODEL)
    bias = ln_bias[l].reshape(1, D_MODEL)
    sink = sinks[l]

    xp2 = x_prompt.reshape(bp * tp, D_MODEL)
    xs2 = x_sample.reshape(bs * ts, D_MODEL)
    rows_p = bp * tp
    rows_all = rows_p + bs * ts
    x_all, proj = _proj_cast(xp2, w_in[l], 0, rows_all)
    x_all, proj = _proj_cast(xs2, w_in[l], rows_p, rows_all, x_all, proj)
    ckt = jnp.transpose(cache_k[l], (0, 2, 3, 1)).reshape(bs, KV_WIDTH, WINDOW)
    cvt = jnp.transpose(cache_v[l], (0, 2, 3, 1)).reshape(bs, KV_WIDTH, WINDOW)
    proj, w_o_b, ck, cv = _proj_main(x_all, w_in[l], proj, w_out[l], ckt, cvt)
    proj3 = proj.reshape(proj.shape[0] // ts, ts, N_IN)

    b_all, a_all = _mix_prompt(proj, slopes, sink, w_pool_b, scale, bp, tp)
    state_t = jnp.transpose(state_pool[l], (1, 0, 2))
    b_all, new_k_s, new_v_s, a_all, new_pool_s = _mix_sample(
        proj3, ck, cv, state_t, slopes, sink, w_pool_b, scale, b_all, a_all, rows_p // ts)

    h_all = _merge(a_all, b_all, proj, w_out_pool[l], w_out_attn[l])
    y_p = _out(h_all, 0, w_o_b, xp2, gain, bias).reshape(bp, tp, D_MODEL)
    y_s = _out(h_all, rows_p, w_o_b, xs2, gain, bias).reshape(bs, ts, D_MODEL)

    def tail(n_rows, col0, width):
        return jnp.stack([lax.slice(proj, ((b + 1) * tp - n_rows, col0), ((b + 1) * tp, col0 + width))
                          for b in range(bp)])

    new_k_p = tail(WINDOW, COL_K, KV_WIDTH).reshape(1, bp, WINDOW, N_KV_HEADS, HEAD_DIM)
    new_v_p = tail(WINDOW, COL_V, KV_WIDTH).reshape(1, bp, WINDOW, N_KV_HEADS, HEAD_DIM)
    new_pool_p = tail(POOL_STATE, COL_U, POOL_WIDTH)[None]
    new_k_s = new_k_s.reshape(1, bs, WINDOW, N_KV_HEADS, HEAD_DIM)
    new_v_s = new_v_s.reshape(1, bs, WINDOW, N_KV_HEADS, HEAD_DIM)

    return (y_p, y_s, new_k_p, new_v_p, new_pool_p, new_k_s, new_v_s, new_pool_s[None])
```

```python
import functools

import jax
import jax.numpy as jnp
from jax import lax
from jax.experimental import pallas as pl
from jax.experimental.pallas import tpu as pltpu

D_MODEL = 4096
POOL_WIDTH = 2048
POOL_WINDOWS = (2, 4, 8, 16)
POOL_GROUP_WIDTH = 512
POOL_STATE = 15
HEAD_DIM = 64
N_HEADS = 32
N_KV_HEADS = 4
GQA_GROUP = 8
ATTN_WIDTH = 2048
KV_WIDTH = 256
WINDOW = 128
PAST_LEN = 8192
N_IN = 16896
COL_U, COL_ZA, COL_Q, COL_K, COL_V, COL_ZB, COL_GA, COL_GB = 0, 2048, 4096, 6144, 6400, 6656, 8704, 12800
DEEPNORM_ALPHA = 2.0 ** 0.25
LN_EPS = 1e-5

V7X_VMEM_BYTES = 64 * 1024 * 1024
SUBLANES = 8
HALO_ROWS = 16
MXU_WIDTH = 256
PROJ_TN = 3 * MXU_WIDTH
PROJ_MAIN_TN = 7 * MXU_WIDTH
MERGE_TN = 4 * MXU_WIDTH
WO_SLICE_ROWS = 32
MIX_BLOCKS = 2
STREAM_STEPS = 16

_F32 = jnp.float32
_BF16 = jnp.bfloat16


def _params(sem, vmem_mib):
    return pltpu.CompilerParams(dimension_semantics=sem, vmem_limit_bytes=vmem_mib * 1024 * 1024)


def _stream_cast_steps(n_tiles, steps_per_pass):
    last = n_tiles - 1

    def row_blk(j, i):
        return jnp.where(j > 0, i, 0)

    def col_blk(j, i):
        return jnp.maximum(j - 1, 0)

    def chunk_blk(j, i):
        return jnp.where(j <= last, i, steps_per_pass - 1)

    def stage_blk(j, i):
        return jnp.minimum(j, last)

    return row_blk, col_blk, chunk_blk, stage_blk


def _proj_main_kernel(x_ref, wc_ref, wo_ref, ckt_ref, cvt_ref, proj_in_ref,
                      o_ref, wo_bf16_ref, ck_ref, cv_ref, wb0_ref, wb1_ref):
    del proj_in_ref
    j = pl.program_id(0)
    chunk = wc_ref.shape[0]
    rows = pl.ds(pl.multiple_of(pl.program_id(1) * chunk, chunk), chunk)

    def step(w_next_ref, w_cur_ref):
        wo_bf16_ref[...] = wo_ref[...].astype(_BF16)
        ck_ref[0] = ckt_ref[0].T
        cv_ref[0] = cvt_ref[0].T
        w_next_ref[rows, :] = wc_ref[...].astype(_BF16)
        if w_cur_ref is not None:
            o_ref[...] = jnp.dot(x_ref[...], w_cur_ref[...], preferred_element_type=_F32)

    @pl.when(j == 0)
    def _stage_only():
        step(wb0_ref, None)

    @pl.when((j > 0) & (j % 2 == 0))
    def _even():
        step(wb0_ref, wb1_ref)

    @pl.when(j % 2 == 1)
    def _odd():
        step(wb1_ref, wb0_ref)


def _proj_main(x_bf16, w_f32, proj, w_out_f32, cache_kt, cache_vt, steps=STREAM_STEPS, tn=PROJ_MAIN_TN):
    rows, k = x_bf16.shape
    n = w_f32.shape[1]
    n_tiles = (n - PROJ_TN) // tn
    assert n_tiles * tn + PROJ_TN == n and rows % steps == 0 and k % steps == 0
    tm = rows // steps
    chunk = k // steps
    total_steps = (n_tiles + 1) * steps
    row_blk, col_blk, chunk_blk, stage_blk = _stream_cast_steps(n_tiles, steps)
    n_slices = w_out_f32.shape[0] // WO_SLICE_ROWS
    nseq, feat, win = cache_kt.shape
    assert n_slices <= total_steps and nseq <= total_steps
    wo_spec = pl.BlockSpec((WO_SLICE_ROWS, w_out_f32.shape[1]),
                           lambda j, i: (jnp.minimum(j * steps + i, n_slices - 1), 0))

    def seq_spec(d1, d2):
        return pl.BlockSpec((1, d1, d2), lambda j, i: (jnp.minimum(j * steps + i, nseq - 1), 0, 0))

    return pl.pallas_call(
        _proj_main_kernel,
        grid=(n_tiles + 1, steps),
        in_specs=[pl.BlockSpec((tm, k), lambda j, i: (row_blk(j, i), 0)),
                  pl.BlockSpec((chunk, tn), lambda j, i: (chunk_blk(j, i), stage_blk(j, i))),
                  wo_spec, seq_spec(feat, win), seq_spec(feat, win),
                  pl.BlockSpec(memory_space=pl.ANY)],
        out_specs=[pl.BlockSpec((tm, tn), lambda j, i: (row_blk(j, i), col_blk(j, i))), wo_spec,
                   seq_spec(win, feat), seq_spec(win, feat)],
        out_shape=[jax.ShapeDtypeStruct((rows, n), _F32), jax.ShapeDtypeStruct(w_out_f32.shape, _BF16),
                   jax.ShapeDtypeStruct((nseq, win, feat), _F32), jax.ShapeDtypeStruct((nseq, win, feat), _F32)],
        input_output_aliases={5: 0},
        scratch_shapes=[pltpu.VMEM((k, tn), _BF16), pltpu.VMEM((k, tn), _BF16)],
        compiler_params=_params(("arbitrary", "arbitrary"), 58),
        name="proj",
    )(x_bf16, w_f32, w_out_f32, cache_kt, cache_vt, proj)


def _proj_cast_kernel(x_ref, w_ref, *refs):
    xb_ref, o_ref, wb_ref = refs[-3:]

    @pl.when(pl.program_id(0) == 0)
    def _cast_weight_tile():
        wb_ref[...] = w_ref[...].astype(_BF16)

    xb = x_ref[...].astype(_BF16)
    xb_ref[...] = xb
    o_ref[...] = jnp.dot(xb, wb_ref[...], preferred_element_type=_F32)


def _proj_cast(x_f32, w_f32, row0, total_rows, x_all=None, proj=None, tm=512, tn=PROJ_TN):
    rows, k = x_f32.shape
    n = w_f32.shape[1]
    blk0 = row0 // tm
    col_blk = n // tn - 1
    aliased = [] if x_all is None else [x_all, proj]
    return pl.pallas_call(
        _proj_cast_kernel,
        grid=(rows // tm,),
        in_specs=[pl.BlockSpec((tm, k), lambda i: (i, 0)),
                  pl.BlockSpec((k, tn), lambda i: (0, col_blk), pipeline_mode=pl.Buffered(1))]
                 + [pl.BlockSpec(memory_space=pl.ANY)] * len(aliased),
        out_specs=[pl.BlockSpec((tm, k), lambda i: (blk0 + i, 0)),
                   pl.BlockSpec((tm, tn), lambda i: (blk0 + i, col_blk))],
        out_shape=[jax.ShapeDtypeStruct((total_rows, k), _BF16),
                   jax.ShapeDtypeStruct((total_rows, n), _F32)],
        input_output_aliases={2: 0, 3: 1} if aliased else {},
        scratch_shapes=[pltpu.VMEM((k, tn), _BF16)],
        compiler_params=_params(("arbitrary",), 56),
        name="proj_cast",
    )(x_f32, w_f32, *aliased)


def _window_sums(ext, n_doublings):
    s = ext
    for d in range(n_doublings):
        s = s + pltpu.roll(s, 1 << d, axis=0)
    return s


def _pool_groups(ext, u, z_a, pos, wp_ref, scale_ref, halo):
    outs = []
    for g, w in enumerate(POOL_WINDOWS):
        sl = slice(g * POOL_GROUP_WIDTH, (g + 1) * POOL_GROUP_WIDTH)
        s = _window_sums(ext[:, sl], g + 1)[halo:]
        count = jnp.minimum(w, pos + 1).astype(_F32)
        pooled = s / count - u[:, sl]
        mixed = jnp.dot(pooled.astype(_BF16), wp_ref[g], preferred_element_type=_F32)
        pool_out = mixed * scale_ref[:, sl]
        z = z_a[:, sl]
        outs.append(z * jax.nn.sigmoid(z) * pool_out)
    return outs


def _pool_prompt_tile(t, u_ref, halo_ref, za_ref, wp_ref, scale_ref, a_ref):
    tm = u_ref.shape[0]
    u = u_ref[...]
    halo = jnp.where(t == 0, 0.0, halo_ref[...])
    ext = jnp.concatenate([halo, u], axis=0)
    pos = t * tm + lax.broadcasted_iota(jnp.int32, (tm, 1), 0)
    outs = _pool_groups(ext, u, za_ref[...], pos, wp_ref, scale_ref, HALO_ROWS)
    for g, o in enumerate(outs):
        a_ref[:, g * POOL_GROUP_WIDTH:(g + 1) * POOL_GROUP_WIDTH] = o.astype(_BF16)


def _pool_sample_tile(u_ref, za_ref, stt_ref, wp_ref, scale_ref, a_ref, np_ref, st_ref, bs, t_new):
    st_ref[:, 0, :] = jnp.zeros((bs, POOL_WIDTH), _F32)
    for t in range(POOL_STATE):
        st_ref[:, t + 1, :] = stt_ref[t]
    u3 = u_ref[...]
    st = st_ref[...]
    rows_per_seq = HALO_ROWS + t_new
    ext = jnp.concatenate([st, u3], axis=1).reshape(bs * rows_per_seq, POOL_WIDTH)
    pos3 = PAST_LEN + lax.broadcasted_iota(jnp.int32, (bs, t_new, 1), 1)
    pos = pos3.reshape(bs * t_new, 1)
    u = u3.reshape(bs * t_new, POOL_WIDTH)
    za = za_ref[...].reshape(bs * t_new, POOL_WIDTH)
    for g, w in enumerate(POOL_WINDOWS):
        sl = slice(g * POOL_GROUP_WIDTH, (g + 1) * POOL_GROUP_WIDTH)
        s = _window_sums(ext[:, sl], g + 1)
        s = s.reshape(bs, rows_per_seq, POOL_GROUP_WIDTH)[:, HALO_ROWS:, :].reshape(bs * t_new, POOL_GROUP_WIDTH)
        count = jnp.minimum(w, pos + 1).astype(_F32)
        pooled = s / count - u[:, sl]
        mixed = jnp.dot(pooled.astype(_BF16), wp_ref[g], preferred_element_type=_F32)
        pool_out = mixed * scale_ref[:, sl]
        z = za[:, sl]
        a_ref[:, sl] = (z * jax.nn.sigmoid(z) * pool_out).astype(_BF16)
    keep = POOL_STATE - t_new
    np_ref[:, 0:keep, :] = st[:, HALO_ROWS - keep:, :]
    np_ref[:, keep:, :] = u3


def _softmax_with_sink(l, sink):
    m = jnp.maximum(jnp.max(l, axis=-1, keepdims=True), sink)
    p = jnp.exp(l - m)
    den = jnp.sum(p, axis=-1, keepdims=True) + jnp.exp(sink - m)
    return p * (1.0 / den)


def _penalty(slope, dist, valid):
    return jnp.where(valid, slope * dist.astype(_F32), jnp.inf)


def _attn_prompt_block(q_ref, kvc_ref, kvp_ref, zb_ref, o_ref, pen_ref, sink_ref, tbl):
    blk = WINDOW
    gw = GQA_GROUP * HEAD_DIM
    kv = jnp.concatenate([kvp_ref[...], kvc_ref[...]], axis=0)
    k_all = kv[:, 0:KV_WIDTH].astype(_BF16)
    v_t = kv[:, KV_WIDTH:2 * KV_WIDTH].T.astype(_BF16)
    o_t = []
    for kh in range(N_KV_HEADS):
        hd = slice(kh * HEAD_DIM, (kh + 1) * HEAD_DIM)
        q_kh = (q_ref[:, kh * gw:(kh + 1) * gw] * (HEAD_DIM ** -0.5)).astype(_BF16)
        qs = jnp.concatenate([q_kh[:, g * HEAD_DIM:(g + 1) * HEAD_DIM] for g in range(GQA_GROUP)], axis=0)
        logits_t = lax.dot_general(k_all[:, hd], qs, (((1,), (1,)), ((), ())),
                                   preferred_element_type=_F32)
        l = logits_t - pen_ref[tbl, kh]
        sink = sink_ref[kh]
        m = jnp.maximum(jnp.max(l, axis=0, keepdims=True), sink)
        p = jnp.exp(l - m)
        den = jnp.sum(p, axis=0, keepdims=True) + jnp.exp(sink - m)
        pr_t = (p * (1.0 / den)).astype(_BF16)
        o_t.append(jnp.dot(v_t[hd, :], pr_t, preferred_element_type=_F32))
    for pair in range(N_KV_HEADS // 2):
        o_pair = jnp.concatenate([o_t[2 * pair], o_t[2 * pair + 1]], axis=0)
        tiles = [o_pair[:, g * blk:(g + 1) * blk].T for g in range(GQA_GROUP)]
        for j in range(2):
            kh = 2 * pair + j
            o_kh = jnp.concatenate([t[:, j * HEAD_DIM:(j + 1) * HEAD_DIM] for t in tiles], axis=1)
            zb = zb_ref[:, kh * gw:(kh + 1) * gw]
            o_ref[:, kh * gw:(kh + 1) * gw] = (zb * jax.nn.sigmoid(zb) * o_kh).astype(_BF16)


def _mix_prompt_kernel(slopes_ref, sinks_ref, qkvz_ref, kvp_ref, uz_ref, halo_ref, wp_ref, scale_ref,
                       o_ref, a_ref, pen_ref, sink_ref):
    blk = WINDOW
    n = pl.program_id(1)
    u_ref = uz_ref.at[:, COL_U:COL_U + POOL_WIDTH]
    za_ref = uz_ref.at[:, COL_ZA:COL_ZA + POOL_WIDTH]
    q_ref = qkvz_ref.at[:, 0:ATTN_WIDTH]
    kv_ref = qkvz_ref.at[:, COL_K - COL_Q:COL_K - COL_Q + 2 * KV_WIDTH]
    zb_ref = qkvz_ref.at[:, COL_ZB - COL_Q:COL_ZB - COL_Q + ATTN_WIDTH]
    _pool_prompt_tile(n, u_ref, halo_ref, za_ref, wp_ref, scale_ref, a_ref)

    @pl.when((pl.program_id(0) == 0) & (n == 0))
    def _fill_tables():
        si = lax.broadcasted_iota(jnp.int32, (2 * blk, blk), 0)
        ti = lax.broadcasted_iota(jnp.int32, (2 * blk, blk), 1)
        dist = blk + ti - si
        for n_tbl in (0, 1):
            key_pos = (n_tbl - 1) * blk + si
            valid = (dist >= 0) & (dist < WINDOW) & (key_pos >= 0)
            for h in range(N_HEADS):
                kh, g = divmod(h, GQA_GROUP)
                pen_ref[n_tbl, kh, :, g * blk:(g + 1) * blk] = _penalty(slopes_ref[h], dist, valid)
        for h in range(N_HEADS):
            kh, g = divmod(h, GQA_GROUP)
            sink_ref[kh, :, g * blk:(g + 1) * blk] = jnp.full((1, blk), sinks_ref[h], _F32)

    for s in range(MIX_BLOCKS):
        rows = slice(s * blk, (s + 1) * blk)
        prev = kvp_ref if s == 0 else kv_ref.at[(s - 1) * blk:s * blk, :]
        tbl = jnp.minimum(n, 1) if s == 0 else 1
        _attn_prompt_block(q_ref.at[rows, :], kv_ref.at[rows, :], prev, zb_ref.at[rows, :], o_ref.at[rows, :],
                           pen_ref, sink_ref, tbl)


def _mix_prompt(proj, slopes, sinks, w_pool_bf16, pool_scale, batch, seq):
    blk = WINDOW
    tm = MIX_BLOCKS * blk
    nt = seq // tm
    halo_blocks = tm // HALO_ROWS
    smem = pl.BlockSpec(memory_space=pltpu.SMEM)
    rows = proj.shape[0]
    return pl.pallas_call(
        _mix_prompt_kernel,
        grid=(batch, nt),
        in_specs=[
            smem, smem,
            pl.BlockSpec((pl.Element(tm), pl.Element(COL_GA - COL_Q)), lambda b, n: ((b * nt + n) * tm, COL_Q)),
            pl.BlockSpec((blk, 2 * KV_WIDTH),
                         lambda b, n: (jnp.maximum((b * nt + n) * MIX_BLOCKS - 1, 0), COL_K // (2 * KV_WIDTH))),
            pl.BlockSpec((tm, COL_Q - COL_U), lambda b, n: (b * nt + n, 0)),
            pl.BlockSpec((HALO_ROWS, POOL_WIDTH),
                         lambda b, n: (jnp.maximum((b * nt + n) * halo_blocks - 1, 0), COL_U // POOL_WIDTH)),
            pl.BlockSpec((4, POOL_GROUP_WIDTH, POOL_GROUP_WIDTH), lambda b, n: (0, 0, 0)),
            pl.BlockSpec((1, POOL_WIDTH), lambda b, n: (0, 0)),
        ],
        out_specs=[pl.BlockSpec((tm, ATTN_WIDTH), lambda b, n: (b * nt + n, 0)),
                   pl.BlockSpec((tm, POOL_WIDTH), lambda b, n: (b * nt + n, 0))],
        out_shape=[jax.ShapeDtypeStruct((rows, ATTN_WIDTH), _BF16),
                   jax.ShapeDtypeStruct((rows, POOL_WIDTH), _BF16)],
        scratch_shapes=[pltpu.VMEM((2, N_KV_HEADS, 2 * blk, GQA_GROUP * blk), _F32),
                        pltpu.VMEM((N_KV_HEADS, 1, GQA_GROUP * blk), _F32)],
        compiler_params=_params(("arbitrary", "arbitrary"), 48),
        name="mix_prompt",
    )(slopes, sinks, proj, proj, proj, proj, w_pool_bf16, pool_scale)


def _mix_sample_kernel(slopes_ref, sinks_ref, qkvz_ref, ck_ref, cv_ref, uz_ref, stt_ref, wp_ref, scale_ref,
                       b_in_ref, a_in_ref, o_ref, nk_ref, nv_ref, a_ref, np_ref, pen_ref, st_ref, *, bs, t_new):
    del b_in_ref, a_in_ref
    u_ref = uz_ref.at[:, :, COL_U:COL_U + POOL_WIDTH]
    za_ref = uz_ref.at[:, :, COL_ZA:COL_ZA + POOL_WIDTH]
    q_ref = qkvz_ref.at[:, :, 0:ATTN_WIDTH]
    kvn_ref = qkvz_ref.at[:, :, COL_K - COL_Q:COL_K - COL_Q + 2 * KV_WIDTH]
    zb_ref = qkvz_ref.at[:, :, COL_ZB - COL_Q:COL_ZB - COL_Q + ATTN_WIDTH]
    _pool_sample_tile(u_ref, za_ref, stt_ref, wp_ref, scale_ref, a_ref, np_ref, st_ref, bs, t_new)
    s_all = WINDOW + t_new
    gw = GQA_GROUP * HEAD_DIM

    @pl.when(pl.program_id(0) == 0)
    def _fill_penalty_table():
        ti = lax.broadcasted_iota(jnp.int32, (t_new, s_all), 0)
        si = lax.broadcasted_iota(jnp.int32, (t_new, s_all), 1)
        dist = WINDOW + ti - si
        key_pos = PAST_LEN - WINDOW + si
        valid = (dist >= 0) & (dist < WINDOW) & (key_pos >= 0)
        for h in range(N_HEADS):
            pen_ref[h] = _penalty(slopes_ref[h], dist, valid)

    kvn = kvn_ref[...]
    k_all = jnp.concatenate([ck_ref[...], kvn[:, :, 0:KV_WIDTH]], axis=1)
    v_all = jnp.concatenate([cv_ref[...], kvn[:, :, KV_WIDTH:2 * KV_WIDTH]], axis=1)
    nk_ref[...] = k_all[:, t_new:, :]
    nv_ref[...] = v_all[:, t_new:, :]
    kb = k_all.astype(_BF16)
    vb = v_all.astype(_BF16)
    for kh in range(N_KV_HEADS):
        q_kh = (q_ref[:, :, kh * gw:(kh + 1) * gw] * (HEAD_DIM ** -0.5)).astype(_BF16)
        qs = jnp.concatenate([q_kh[:, :, g * HEAD_DIM:(g + 1) * HEAD_DIM] for g in range(GQA_GROUP)], axis=1)
        hd = slice(kh * HEAD_DIM, (kh + 1) * HEAD_DIM)
        logits = jnp.stack(
            [lax.dot_general(qs[b], kb[b, :, hd], (((1,), (1,)), ((), ())), preferred_element_type=_F32)
             for b in range(bs)], axis=0)
        probs = []
        for g in range(GQA_GROUP):
            h = kh * GQA_GROUP + g
            l = logits[:, g * t_new:(g + 1) * t_new, :] - pen_ref[h]
            probs.append(_softmax_with_sink(l, sinks_ref[h]).astype(_BF16))
        pr = jnp.concatenate(probs, axis=1)
        o = jnp.stack([jnp.dot(pr[b], vb[b, :, hd], preferred_element_type=_F32) for b in range(bs)], axis=0)
        o_kh = jnp.concatenate([o[:, g * t_new:(g + 1) * t_new, :] for g in range(GQA_GROUP)], axis=2)
        zb = zb_ref[:, :, kh * gw:(kh + 1) * gw]
        res = (zb * jax.nn.sigmoid(zb) * o_kh).reshape(bs * t_new, gw)
        o_ref[:, kh * gw:(kh + 1) * gw] = res.astype(_BF16)


def _mix_sample(proj3, cache_k, cache_v, state_t, slopes, sinks, w_pool_bf16, pool_scale, b_all, a_all, seq0, bs=16):
    nseq = cache_k.shape[0]
    t_new = proj3.shape[1]
    blk0 = seq0 // bs
    smem = pl.BlockSpec(memory_space=pltpu.SMEM)
    alias = pl.BlockSpec(memory_space=pl.ANY)
    kern = functools.partial(_mix_sample_kernel, bs=bs, t_new=t_new)
    cache_spec = pl.BlockSpec((bs, WINDOW, KV_WIDTH), lambda i: (i, 0, 0))
    in_specs = [
        smem, smem,
        pl.BlockSpec((pl.Element(bs), pl.Element(t_new), pl.Element(COL_GA - COL_Q)),
                     lambda i: ((blk0 + i) * bs, 0, COL_Q)),
        cache_spec, cache_spec,
        pl.BlockSpec((bs, t_new, COL_Q - COL_U), lambda i: (blk0 + i, 0, 0)),
        pl.BlockSpec((POOL_STATE, bs, POOL_WIDTH), lambda i: (0, i, 0)),
        pl.BlockSpec((4, POOL_GROUP_WIDTH, POOL_GROUP_WIDTH), lambda i: (0, 0, 0)),
        pl.BlockSpec((1, POOL_WIDTH), lambda i: (0, 0)),
        alias, alias,
    ]
    operands = (slopes, sinks, proj3, cache_k, cache_v, proj3, state_t, w_pool_bf16, pool_scale, b_all, a_all)
    return pl.pallas_call(
        kern,
        grid=(nseq // bs,),
        in_specs=in_specs,
        out_specs=[
            pl.BlockSpec((bs * t_new, ATTN_WIDTH), lambda i: (blk0 + i, 0)),
            cache_spec, cache_spec,
            pl.BlockSpec((bs * t_new, POOL_WIDTH), lambda i: (blk0 + i, 0)),
            pl.BlockSpec((bs, POOL_STATE, POOL_WIDTH), lambda i: (i, 0, 0)),
        ],
        out_shape=[
            jax.ShapeDtypeStruct(b_all.shape, _BF16),
            jax.ShapeDtypeStruct((nseq, WINDOW, KV_WIDTH), _F32),
            jax.ShapeDtypeStruct((nseq, WINDOW, KV_WIDTH), _F32),
            jax.ShapeDtypeStruct(a_all.shape, _BF16),
            jax.ShapeDtypeStruct((nseq, POOL_STATE, POOL_WIDTH), _F32),
        ],
        input_output_aliases={len(operands) - 2: 0, len(operands) - 1: 3},
        scratch_shapes=[pltpu.VMEM((N_HEADS, t_new, WINDOW + t_new), _F32),
                        pltpu.VMEM((bs, HALO_ROWS, POOL_WIDTH), _F32)],
        compiler_params=_params(("arbitrary",), 48),
        name="mix_sample",
    )(*operands)


def _merge_kernel(a_ref, b_ref, wpc_ref, wac_ref, ga_ref, gb_ref, h_ref,
                  wp0_ref, wp1_ref, wa0_ref, wa1_ref, *, row_chunk):
    j = pl.program_id(0)
    chunk = wpc_ref.shape[0]
    wrows = pl.ds(pl.multiple_of(pl.program_id(1) * chunk, chunk), chunk)

    def step(wp_next_ref, wa_next_ref, wp_cur_ref, wa_cur_ref):
        wp_next_ref[wrows, :] = wpc_ref[...].astype(_BF16)
        wa_next_ref[wrows, :] = wac_ref[...].astype(_BF16)
        if wp_cur_ref is None:
            return
        for r in range(0, a_ref.shape[0], row_chunk):
            rows = pl.ds(r, row_chunk)
            ya = jnp.dot(a_ref[rows, :], wp_cur_ref[...], preferred_element_type=_F32)
            yb = jnp.dot(b_ref[rows, :], wa_cur_ref[...], preferred_element_type=_F32)
            h = jax.nn.sigmoid(ga_ref[rows, :]) * ya + jax.nn.sigmoid(gb_ref[rows, :]) * yb
            h_ref[rows, :] = h.astype(_BF16)

    @pl.when(j == 0)
    def _stage_only():
        step(wp0_ref, wa0_ref, None, None)

    @pl.when((j > 0) & (j % 2 == 0))
    def _even():
        step(wp0_ref, wa0_ref, wp1_ref, wa1_ref)

    @pl.when(j % 2 == 1)
    def _odd():
        step(wp1_ref, wa1_ref, wp0_ref, wa0_ref)


def _merge(a, b, proj, w_out_pool, w_out_attn, steps=STREAM_STEPS, tn=MERGE_TN):
    rows = a.shape[0]
    n_tiles = D_MODEL // tn
    assert rows % steps == 0 and POOL_WIDTH % steps == 0 and ATTN_WIDTH == POOL_WIDTH
    tm = rows // steps
    chunk = POOL_WIDTH // steps
    row_blk, col_blk, chunk_blk, stage_blk = _stream_cast_steps(n_tiles, steps)

    def gate_spec(col0):
        assert col0 % MXU_WIDTH == 0 and tn % MXU_WIDTH == 0 and tm % SUBLANES == 0
        return pl.BlockSpec((pl.Element(tm), pl.Element(tn)),
                            lambda j, i: (pl.multiple_of(row_blk(j, i) * tm, SUBLANES),
                                          pl.multiple_of(col0 + col_blk(j, i) * tn, MXU_WIDTH)))

    return pl.pallas_call(
        functools.partial(_merge_kernel, row_chunk=tm // 2),
        grid=(n_tiles + 1, steps),
        in_specs=[
            pl.BlockSpec((tm, POOL_WIDTH), lambda j, i: (row_blk(j, i), 0)),
            pl.BlockSpec((tm, ATTN_WIDTH), lambda j, i: (row_blk(j, i), 0)),
            pl.BlockSpec((chunk, tn), lambda j, i: (chunk_blk(j, i), stage_blk(j, i))),
            pl.BlockSpec((chunk, tn), lambda j, i: (chunk_blk(j, i), stage_blk(j, i))),
            gate_spec(COL_GA), gate_spec(COL_GB),
        ],
        out_specs=pl.BlockSpec((tm, tn), lambda j, i: (row_blk(j, i), col_blk(j, i))),
        out_shape=jax.ShapeDtypeStruct((rows, D_MODEL), _BF16),
        scratch_shapes=[pltpu.VMEM((POOL_WIDTH, tn), _BF16)] * 2 + [pltpu.VMEM((ATTN_WIDTH, tn), _BF16)] * 2,
        compiler_params=_params(("arbitrary", "arbitrary"), 56),
        name="merge",
    )(a, b, w_out_pool, w_out_attn, proj, proj)


def _out_kernel(h_ref, w_ref, x_ref, gain_ref, bias_ref, y_ref, *, sub):
    gain = gain_ref[...]
    bias = bias_ref[...]
    for r0 in range(0, h_ref.shape[0], sub):
        y_ref[pl.ds(r0, sub), :] = jnp.dot(h_ref[pl.ds(r0, sub), :], w_ref[...], preferred_element_type=_F32)
        for g in range(sub // SUBLANES):
            rows = pl.ds(r0 + g * SUBLANES, SUBLANES)
            y = DEEPNORM_ALPHA * x_ref[rows, :] + y_ref[rows, :]
            mu = jnp.mean(y, axis=-1, keepdims=True)
            d = y - mu
            var = jnp.mean(d * d, axis=-1, keepdims=True)
            y_ref[rows, :] = d * lax.rsqrt(var + LN_EPS) * gain + bias


def _out(h_all, row0, w_out_bf16, x, gain, bias, tm=256, sub=128):
    rows = x.shape[0]
    blk0 = row0 // tm
    return pl.pallas_call(
        functools.partial(_out_kernel, sub=sub),
        grid=(rows // tm,),
        in_specs=[
            pl.BlockSpec((tm, D_MODEL), lambda i: (blk0 + i, 0)),
            pl.BlockSpec((D_MODEL, D_MODEL), lambda i: (0, 0), pipeline_mode=pl.Buffered(1)),
            pl.BlockSpec((tm, D_MODEL), lambda i: (i, 0)),
            pl.BlockSpec((1, D_MODEL), lambda i: (0, 0)),
            pl.BlockSpec((1, D_MODEL), lambda i: (0, 0)),
        ],
        out_specs=pl.BlockSpec((tm, D_MODEL), lambda i: (i, 0)),
        out_shape=jax.ShapeDtypeStruct((rows, D_MODEL), _F32),
        compiler_params=_params(("arbitrary",), 60),
        name="out",
    )(h_all, w_out_bf16, x, gain, bias)


def _alibi_slopes():
    h = jnp.arange(N_HEADS, dtype=_F32)
    return 2.0 ** (-8.0 * (h + 1.0) / N_HEADS)


def kernel(x_prompt, x_sample, cache_k, cache_v, state_pool, w_in, w_pool, pool_scale, sinks,
           w_out_pool, w_out_attn, w_out, ln_gain, ln_bias):
    depth = w_in.shape[0]
    assert depth == 1
    bp, tp, _ = x_prompt.shape
    bs, ts, _ = x_sample.shape
    slopes = _alibi_slopes()

    l = 0
    w_pool_b = w_pool[l].astype(_BF16)
    scale = pool_scale[l].reshape(1, POOL_WIDTH)
    gain = ln_gain[l].reshape(1, D_MODEL)
    bias = ln_bias[l].reshape(1, D_MODEL)
    sink = sinks[l]

    xp2 = x_prompt.reshape(bp * tp, D_MODEL)
    xs2 = x_sample.reshape(bs * ts, D_MODEL)
    rows_p = bp * tp
    rows_all = rows_p + bs * ts
    x_all, proj = _proj_cast(xp2, w_in[l], 0, rows_all)
    x_all, proj = _proj_cast(xs2, w_in[l], rows_p, rows_all, x_all, proj)
    ckt = jnp.transpose(cache_k[l], (0, 2, 3, 1)).reshape(bs, KV_WIDTH, WINDOW)
    cvt = jnp.transpose(cache_v[l], (0, 2, 3, 1)).reshape(bs, KV_WIDTH, WINDOW)
    proj, w_o_b, ck, cv = _proj_main(x_all, w_in[l], proj, w_out[l], ckt, cvt)
    proj3 = proj.reshape(proj.shape[0] // ts, ts, N_IN)

    b_all, a_all = _mix_prompt(proj, slopes, sink, w_pool_b, scale, bp, tp)
    state_t = jnp.transpose(state_pool[l], (1, 0, 2))
    b_all, new_k_s, new_v_s, a_all, new_pool_s = _mix_sample(
        proj3, ck, cv, state_t, slopes, sink, w_pool_b, scale, b_all, a_all, rows_p // ts)

    h_all = _merge(a_all, b_all, proj, w_out_pool[l], w_out_attn[l])
    y_p = _out(h_all, 0, w_o_b, xp2, gain, bias).reshape(bp, tp, D_MODEL)
    y_s = _out(h_all, rows_p, w_o_b, xs2, gain, bias).reshape(bs, ts, D_MODEL)

    def tail(n_rows, col0, width):
        return jnp.stack([lax.slice(proj, ((b + 1) * tp - n_rows, col0), ((b + 1) * tp, col0 + width))
                          for b in range(bp)])

    new_k_p = tail(WINDOW, COL_K, KV_WIDTH).reshape(1, bp, WINDOW, N_KV_HEADS, HEAD_DIM)
    new_v_p = tail(WINDOW, COL_V, KV_WIDTH).reshape(1, bp, WINDOW, N_KV_HEADS, HEAD_DIM)
    new_pool_p = tail(POOL_STATE, COL_U, POOL_WIDTH)[None]
    new_k_s = new_k_s.reshape(1, bs, WINDOW, N_KV_HEADS, HEAD_DIM)
    new_v_s = new_v_s.reshape(1, bs, WINDOW, N_KV_HEADS, HEAD_DIM)

    return (y_p, y_s, new_k_p, new_v_p, new_pool_p, new_k_s, new_v_s, new_pool_s[None])
```

```python
import functools

import jax
import jax.numpy as jnp
from jax import lax
from jax.experimental import pallas as pl
from jax.experimental.pallas import tpu as pltpu

D_MODEL = 4096
POOL_WIDTH = 2048
POOL_WINDOWS = (2, 4, 8, 16)
POOL_GROUP_WIDTH = 512
POOL_STATE = 15
HEAD_DIM = 64
N_HEADS = 32
N_KV_HEADS = 4
GQA_GROUP = 8
ATTN_WIDTH = 2048
KV_WIDTH = 256
WINDOW = 128
PAST_LEN = 8192
N_IN = 16896
COL_U, COL_ZA, COL_Q, COL_K, COL_V, COL_ZB, COL_GA, COL_GB = 0, 2048, 4096, 6144, 6400, 6656, 8704, 12800
DEEPNORM_ALPHA = 2.0 ** 0.25
LN_EPS = 1e-5

V7X_VMEM_BYTES = 64 * 1024 * 1024
SUBLANES = 8
HALO_ROWS = 16
MXU_WIDTH = 256
PROJ_TN = 3 * MXU_WIDTH
PROJ_MAIN_TN = 7 * MXU_WIDTH
MERGE_TN = 4 * MXU_WIDTH
MERGE_STAGE_ROWS = 16
WO_SLICE_ROWS = 32
MIX_BLOCKS = 4
STREAM_STEPS = 16

_F32 = jnp.float32
_BF16 = jnp.bfloat16


def _params(sem, vmem_mib):
    return pltpu.CompilerParams(dimension_semantics=sem, vmem_limit_bytes=vmem_mib * 1024 * 1024)


def _stream_cast_steps(n_tiles, steps_per_pass):
    last = n_tiles - 1

    def row_blk(j, i):
        return jnp.where(j > 0, i, 0)

    def col_blk(j, i):
        return jnp.maximum(j - 1, 0)

    def chunk_blk(j, i):
        return jnp.where(j <= last, i, steps_per_pass - 1)

    def stage_blk(j, i):
        return jnp.minimum(j, last)

    return row_blk, col_blk, chunk_blk, stage_blk


def _proj_main_kernel(x_ref, wc_ref, wo_ref, ckt_ref, cvt_ref, wpm_ref, wam_ref, proj_in_ref,
                      o_ref, wo_bf16_ref, ck_ref, cv_ref, wpm_bf16_ref, wam_bf16_ref, wb0_ref, wb1_ref):
    del proj_in_ref
    j = pl.program_id(0)
    chunk = wc_ref.shape[0]
    rows = pl.ds(pl.multiple_of(pl.program_id(1) * chunk, chunk), chunk)

    def step(w_next_ref, w_cur_ref):
        wo_bf16_ref[...] = wo_ref[...].astype(_BF16)
        ck_ref[0] = ckt_ref[0].T
        cv_ref[0] = cvt_ref[0].T
        wpm_bf16_ref[...] = wpm_ref[...].astype(_BF16)
        wam_bf16_ref[...] = wam_ref[...].astype(_BF16)
        w_next_ref[rows, :] = wc_ref[...].astype(_BF16)
        if w_cur_ref is not None:
            o_ref[...] = jnp.dot(x_ref[...], w_cur_ref[...], preferred_element_type=_F32)

    @pl.when(j == 0)
    def _stage_only():
        step(wb0_ref, None)

    @pl.when((j > 0) & (j % 2 == 0))
    def _even():
        step(wb0_ref, wb1_ref)

    @pl.when(j % 2 == 1)
    def _odd():
        step(wb1_ref, wb0_ref)


def _proj_main(x_bf16, w_f32, proj, w_out_f32, cache_kt, cache_vt, w_out_pool, w_out_attn,
               steps=STREAM_STEPS, tn=PROJ_MAIN_TN):
    rows, k = x_bf16.shape
    n = w_f32.shape[1]
    n_tiles = (n - PROJ_TN) // tn
    assert n_tiles * tn + PROJ_TN == n and rows % steps == 0 and k % steps == 0
    tm = rows // steps
    chunk = k // steps
    total_steps = (n_tiles + 1) * steps
    row_blk, col_blk, chunk_blk, stage_blk = _stream_cast_steps(n_tiles, steps)
    n_slices = w_out_f32.shape[0] // WO_SLICE_ROWS
    nseq, feat, win = cache_kt.shape
    assert n_slices <= total_steps and nseq <= total_steps
    wo_spec = pl.BlockSpec((WO_SLICE_ROWS, w_out_f32.shape[1]),
                           lambda j, i: (jnp.minimum(j * steps + i, n_slices - 1), 0))

    def seq_spec(d1, d2):
        return pl.BlockSpec((1, d1, d2), lambda j, i: (jnp.minimum(j * steps + i, nseq - 1), 0, 0))

    km = w_out_pool.shape[0]
    n_mchunks = km // MERGE_STAGE_ROWS
    assert n_mchunks <= total_steps and w_out_attn.shape == w_out_pool.shape
    wm_spec = pl.BlockSpec((MERGE_STAGE_ROWS, MERGE_TN), lambda j, i: (jnp.minimum(j * steps + i, n_mchunks - 1), 0))
    wm_shape = jax.ShapeDtypeStruct((km, MERGE_TN), _BF16)

    return pl.pallas_call(
        _proj_main_kernel,
        grid=(n_tiles + 1, steps),
        in_specs=[pl.BlockSpec((tm, k), lambda j, i: (row_blk(j, i), 0)),
                  pl.BlockSpec((chunk, tn), lambda j, i: (chunk_blk(j, i), stage_blk(j, i))),
                  wo_spec, seq_spec(feat, win), seq_spec(feat, win), wm_spec, wm_spec,
                  pl.BlockSpec(memory_space=pl.ANY)],
        out_specs=[pl.BlockSpec((tm, tn), lambda j, i: (row_blk(j, i), col_blk(j, i))), wo_spec,
                   seq_spec(win, feat), seq_spec(win, feat), wm_spec, wm_spec],
        out_shape=[jax.ShapeDtypeStruct((rows, n), _F32), jax.ShapeDtypeStruct(w_out_f32.shape, _BF16),
                   jax.ShapeDtypeStruct((nseq, win, feat), _F32), jax.ShapeDtypeStruct((nseq, win, feat), _F32),
                   wm_shape, wm_shape],
        input_output_aliases={7: 0},
        scratch_shapes=[pltpu.VMEM((k, tn), _BF16), pltpu.VMEM((k, tn), _BF16)],
        compiler_params=_params(("arbitrary", "arbitrary"), 60),
        name="proj",
    )(x_bf16, w_f32, w_out_f32, cache_kt, cache_vt, w_out_pool, w_out_attn, proj)


def _proj_cast_kernel(x_ref, w_ref, *refs):
    xb_ref, o_ref, wb_ref = refs[-3:]

    @pl.when(pl.program_id(0) == 0)
    def _cast_weight_tile():
        wb_ref[...] = w_ref[...].astype(_BF16)

    xb = x_ref[...].astype(_BF16)
    xb_ref[...] = xb
    o_ref[...] = jnp.dot(xb, wb_ref[...], preferred_element_type=_F32)


def _proj_cast(x_f32, w_f32, row0, total_rows, x_all=None, proj=None, tm=512, tn=PROJ_TN):
    rows, k = x_f32.shape
    n = w_f32.shape[1]
    blk0 = row0 // tm
    col_blk = n // tn - 1
    aliased = [] if x_all is None else [x_all, proj]
    return pl.pallas_call(
        _proj_cast_kernel,
        grid=(rows // tm,),
        in_specs=[pl.BlockSpec((tm, k), lambda i: (i, 0)),
                  pl.BlockSpec((k, tn), lambda i: (0, col_blk), pipeline_mode=pl.Buffered(1))]
                 + [pl.BlockSpec(memory_space=pl.ANY)] * len(aliased),
        out_specs=[pl.BlockSpec((tm, k), lambda i: (blk0 + i, 0)),
                   pl.BlockSpec((tm, tn), lambda i: (blk0 + i, col_blk))],
        out_shape=[jax.ShapeDtypeStruct((total_rows, k), _BF16),
                   jax.ShapeDtypeStruct((total_rows, n), _F32)],
        input_output_aliases={2: 0, 3: 1} if aliased else {},
        scratch_shapes=[pltpu.VMEM((k, tn), _BF16)],
        compiler_params=_params(("arbitrary",), 56),
        name="proj_cast",
    )(x_f32, w_f32, *aliased)


def _window_sums(ext, n_doublings):
    s = ext
    for d in range(n_doublings):
        s = s + pltpu.roll(s, 1 << d, axis=0)
    return s


def _pool_groups(ext, u, z_a, pos, wp_ref, scale_ref, halo):
    outs = []
    for g, w in enumerate(POOL_WINDOWS):
        sl = slice(g * POOL_GROUP_WIDTH, (g + 1) * POOL_GROUP_WIDTH)
        s = _window_sums(ext[:, sl], g + 1)[halo:]
        count = jnp.minimum(w, pos + 1).astype(_F32)
        pooled = s / count - u[:, sl]
        mixed = jnp.dot(pooled.astype(_BF16), wp_ref[g], preferred_element_type=_F32)
        pool_out = mixed * scale_ref[:, sl]
        z = z_a[:, sl]
        outs.append(z * jax.nn.sigmoid(z) * pool_out)
    return outs


def _pool_prompt_tile(t, u_ref, halo_ref, za_ref, wp_ref, scale_ref, a_ref):
    tm = u_ref.shape[0]
    u = u_ref[...]
    halo = jnp.where(t == 0, 0.0, halo_ref[...])
    ext = jnp.concatenate([halo, u], axis=0)
    pos = t * tm + lax.broadcasted_iota(jnp.int32, (tm, 1), 0)
    outs = _pool_groups(ext, u, za_ref[...], pos, wp_ref, scale_ref, HALO_ROWS)
    for g, o in enumerate(outs):
        a_ref[:, g * POOL_GROUP_WIDTH:(g + 1) * POOL_GROUP_WIDTH] = o.astype(_BF16)


def _pool_sample_tile(u_ref, za_ref, stt_ref, wp_ref, scale_ref, a_ref, np_ref, st_ref, bs, t_new):
    st_ref[:, 0, :] = jnp.zeros((bs, POOL_WIDTH), _F32)
    for t in range(POOL_STATE):
        st_ref[:, t + 1, :] = stt_ref[t]
    u3 = u_ref[...]
    st = st_ref[...]
    rows_per_seq = HALO_ROWS + t_new
    ext = jnp.concatenate([st, u3], axis=1).reshape(bs * rows_per_seq, POOL_WIDTH)
    pos3 = PAST_LEN + lax.broadcasted_iota(jnp.int32, (bs, t_new, 1), 1)
    pos = pos3.reshape(bs * t_new, 1)
    u = u3.reshape(bs * t_new, POOL_WIDTH)
    za = za_ref[...].reshape(bs * t_new, POOL_WIDTH)
    for g, w in enumerate(POOL_WINDOWS):
        sl = slice(g * POOL_GROUP_WIDTH, (g + 1) * POOL_GROUP_WIDTH)
        s = _window_sums(ext[:, sl], g + 1)
        s = s.reshape(bs, rows_per_seq, POOL_GROUP_WIDTH)[:, HALO_ROWS:, :].reshape(bs * t_new, POOL_GROUP_WIDTH)
        count = jnp.minimum(w, pos + 1).astype(_F32)
        pooled = s / count - u[:, sl]
        mixed = jnp.dot(pooled.astype(_BF16), wp_ref[g], preferred_element_type=_F32)
        pool_out = mixed * scale_ref[:, sl]
        z = za[:, sl]
        a_ref[:, sl] = (z * jax.nn.sigmoid(z) * pool_out).astype(_BF16)
    keep = POOL_STATE - t_new
    np_ref[:, 0:keep, :] = st[:, HALO_ROWS - keep:, :]
    np_ref[:, keep:, :] = u3


def _softmax_with_sink(l, sink):
    m = jnp.maximum(jnp.max(l, axis=-1, keepdims=True), sink)
    p = jnp.exp(l - m)
    den = jnp.sum(p, axis=-1, keepdims=True) + jnp.exp(sink - m)
    return p * (1.0 / den)


def _penalty(slope, dist, valid):
    return jnp.where(valid, slope * dist.astype(_F32), jnp.inf)


def _attn_prompt_block(q_ref, kvc_ref, kvp_ref, zb_ref, o_ref, pen_ref, sink_ref, tbl):
    blk = WINDOW
    gw = GQA_GROUP * HEAD_DIM
    kv = jnp.concatenate([kvp_ref[...], kvc_ref[...]], axis=0)
    k_all = kv[:, 0:KV_WIDTH].astype(_BF16)
    v_t = kv[:, KV_WIDTH:2 * KV_WIDTH].T.astype(_BF16)
    o_t = []
    for kh in range(N_KV_HEADS):
        hd = slice(kh * HEAD_DIM, (kh + 1) * HEAD_DIM)
        q_kh = (q_ref[:, kh * gw:(kh + 1) * gw] * (HEAD_DIM ** -0.5)).astype(_BF16)
        qs = jnp.concatenate([q_kh[:, g * HEAD_DIM:(g + 1) * HEAD_DIM] for g in range(GQA_GROUP)], axis=0)
        logits_t = lax.dot_general(k_all[:, hd], qs, (((1,), (1,)), ((), ())),
                                   preferred_element_type=_F32)
        l = logits_t - pen_ref[tbl, kh]
        sink = sink_ref[kh]
        m = jnp.maximum(jnp.max(l, axis=0, keepdims=True), sink)
        p = jnp.exp(l - m)
        den = jnp.sum(p, axis=0, keepdims=True) + jnp.exp(sink - m)
        pr_t = (p * (1.0 / den)).astype(_BF16)
        o_t.append(jnp.dot(v_t[hd, :], pr_t, preferred_element_type=_F32))
    for pair in range(N_KV_HEADS // 2):
        o_pair = jnp.concatenate([o_t[2 * pair], o_t[2 * pair + 1]], axis=0)
        tiles = [o_pair[:, g * blk:(g + 1) * blk].T for g in range(GQA_GROUP)]
        for j in range(2):
            kh = 2 * pair + j
            o_kh = jnp.concatenate([t[:, j * HEAD_DIM:(j + 1) * HEAD_DIM] for t in tiles], axis=1)
            zb = zb_ref[:, kh * gw:(kh + 1) * gw]
            o_ref[:, kh * gw:(kh + 1) * gw] = (zb * jax.nn.sigmoid(zb) * o_kh).astype(_BF16)


def _mix_prompt_kernel(slopes_ref, sinks_ref, qkvz_ref, kvp_ref, uz_ref, halo_ref, wp_ref, scale_ref,
                       o_ref, a_ref, pen_ref, sink_ref):
    blk = WINDOW
    n = pl.program_id(1)
    u_ref = uz_ref.at[:, COL_U:COL_U + POOL_WIDTH]
    za_ref = uz_ref.at[:, COL_ZA:COL_ZA + POOL_WIDTH]
    q_ref = qkvz_ref.at[:, 0:ATTN_WIDTH]
    kv_ref = qkvz_ref.at[:, COL_K - COL_Q:COL_K - COL_Q + 2 * KV_WIDTH]
    zb_ref = qkvz_ref.at[:, COL_ZB - COL_Q:COL_ZB - COL_Q + ATTN_WIDTH]
    _pool_prompt_tile(n, u_ref, halo_ref, za_ref, wp_ref, scale_ref, a_ref)

    @pl.when((pl.program_id(0) == 0) & (n == 0))
    def _fill_tables():
        si = lax.broadcasted_iota(jnp.int32, (2 * blk, blk), 0)
        ti = lax.broadcasted_iota(jnp.int32, (2 * blk, blk), 1)
        dist = blk + ti - si
        for n_tbl in (0, 1):
            key_pos = (n_tbl - 1) * blk + si
            valid = (dist >= 0) & (dist < WINDOW) & (key_pos >= 0)
            for h in range(N_HEADS):
                kh, g = divmod(h, GQA_GROUP)
                pen_ref[n_tbl, kh, :, g * blk:(g + 1) * blk] = _penalty(slopes_ref[h], dist, valid)
        for h in range(N_HEADS):
            kh, g = divmod(h, GQA_GROUP)
            sink_ref[kh, :, g * blk:(g + 1) * blk] = jnp.full((1, blk), sinks_ref[h], _F32)

    for s in range(MIX_BLOCKS):
        rows = slice(s * blk, (s + 1) * blk)
        prev = kvp_ref if s == 0 else kv_ref.at[(s - 1) * blk:s * blk, :]
        tbl = jnp.minimum(n, 1) if s == 0 else 1
        _attn_prompt_block(q_ref.at[rows, :], kv_ref.at[rows, :], prev, zb_ref.at[rows, :], o_ref.at[rows, :],
                           pen_ref, sink_ref, tbl)


def _mix_prompt(proj, slopes, sinks, w_pool_bf16, pool_scale, batch, seq):
    blk = WINDOW
    tm = MIX_BLOCKS * blk
    nt = seq // tm
    halo_blocks = tm // HALO_ROWS
    smem = pl.BlockSpec(memory_space=pltpu.SMEM)
    rows = proj.shape[0]
    return pl.pallas_call(
        _mix_prompt_kernel,
        grid=(batch, nt),
        in_specs=[
            smem, smem,
            pl.BlockSpec((pl.Element(tm), pl.Element(COL_GA - COL_Q)), lambda b, n: ((b * nt + n) * tm, COL_Q)),
            pl.BlockSpec((blk, 2 * KV_WIDTH),
                         lambda b, n: (jnp.maximum((b * nt + n) * MIX_BLOCKS - 1, 0), COL_K // (2 * KV_WIDTH))),
            pl.BlockSpec((tm, COL_Q - COL_U), lambda b, n: (b * nt + n, 0)),
            pl.BlockSpec((HALO_ROWS, POOL_WIDTH),
                         lambda b, n: (jnp.maximum((b * nt + n) * halo_blocks - 1, 0), COL_U // POOL_WIDTH)),
            pl.BlockSpec((4, POOL_GROUP_WIDTH, POOL_GROUP_WIDTH), lambda b, n: (0, 0, 0)),
            pl.BlockSpec((1, POOL_WIDTH), lambda b, n: (0, 0)),
        ],
        out_specs=[pl.BlockSpec((tm, ATTN_WIDTH), lambda b, n: (b * nt + n, 0)),
                   pl.BlockSpec((tm, POOL_WIDTH), lambda b, n: (b * nt + n, 0))],
        out_shape=[jax.ShapeDtypeStruct((rows, ATTN_WIDTH), _BF16),
                   jax.ShapeDtypeStruct((rows, POOL_WIDTH), _BF16)],
        scratch_shapes=[pltpu.VMEM((2, N_KV_HEADS, 2 * blk, GQA_GROUP * blk), _F32),
                        pltpu.VMEM((N_KV_HEADS, 1, GQA_GROUP * blk), _F32)],
        compiler_params=_params(("arbitrary", "arbitrary"), 62),
        name="mix_prompt",
    )(slopes, sinks, proj, proj, proj, proj, w_pool_bf16, pool_scale)


def _mix_sample_kernel(slopes_ref, sinks_ref, qkvz_ref, ck_ref, cv_ref, uz_ref, stt_ref, wp_ref, scale_ref,
                       b_in_ref, a_in_ref, o_ref, nk_ref, nv_ref, a_ref, np_ref, pen_ref, st_ref, *, bs, t_new):
    del b_in_ref, a_in_ref
    u_ref = uz_ref.at[:, :, COL_U:COL_U + POOL_WIDTH]
    za_ref = uz_ref.at[:, :, COL_ZA:COL_ZA + POOL_WIDTH]
    q_ref = qkvz_ref.at[:, :, 0:ATTN_WIDTH]
    kvn_ref = qkvz_ref.at[:, :, COL_K - COL_Q:COL_K - COL_Q + 2 * KV_WIDTH]
    zb_ref = qkvz_ref.at[:, :, COL_ZB - COL_Q:COL_ZB - COL_Q + ATTN_WIDTH]
    _pool_sample_tile(u_ref, za_ref, stt_ref, wp_ref, scale_ref, a_ref, np_ref, st_ref, bs, t_new)
    s_all = WINDOW + t_new
    gw = GQA_GROUP * HEAD_DIM

    @pl.when(pl.program_id(0) == 0)
    def _fill_penalty_table():
        ti = lax.broadcasted_iota(jnp.int32, (t_new, s_all), 0)
        si = lax.broadcasted_iota(jnp.int32, (t_new, s_all), 1)
        dist = WINDOW + ti - si
        key_pos = PAST_LEN - WINDOW + si
        valid = (dist >= 0) & (dist < WINDOW) & (key_pos >= 0)
        for h in range(N_HEADS):
            pen_ref[h] = _penalty(slopes_ref[h], dist, valid)

    kvn = kvn_ref[...]
    k_all = jnp.concatenate([ck_ref[...], kvn[:, :, 0:KV_WIDTH]], axis=1)
    v_all = jnp.concatenate([cv_ref[...], kvn[:, :, KV_WIDTH:2 * KV_WIDTH]], axis=1)
    nk_ref[...] = k_all[:, t_new:, :]
    nv_ref[...] = v_all[:, t_new:, :]
    kb = k_all.astype(_BF16)
    vb = v_all.astype(_BF16)
    for kh in range(N_KV_HEADS):
        q_kh = (q_ref[:, :, kh * gw:(kh + 1) * gw] * (HEAD_DIM ** -0.5)).astype(_BF16)
        qs = jnp.concatenate([q_kh[:, :, g * HEAD_DIM:(g + 1) * HEAD_DIM] for g in range(GQA_GROUP)], axis=1)
        hd = slice(kh * HEAD_DIM, (kh + 1) * HEAD_DIM)
        logits = jnp.stack(
            [lax.dot_general(qs[b], kb[b, :, hd], (((1,), (1,)), ((), ())), preferred_element_type=_F32)
             for b in range(bs)], axis=0)
        probs = []
        for g in range(GQA_GROUP):
            h = kh * GQA_GROUP + g
            l = logits[:, g * t_new:(g + 1) * t_new, :] - pen_ref[h]
            probs.append(_softmax_with_sink(l, sinks_ref[h]).astype(_BF16))
        pr = jnp.concatenate(probs, axis=1)
        o = jnp.stack([jnp.dot(pr[b], vb[b, :, hd], preferred_element_type=_F32) for b in range(bs)], axis=0)
        o_kh = jnp.concatenate([o[:, g * t_new:(g + 1) * t_new, :] for g in range(GQA_GROUP)], axis=2)
        zb = zb_ref[:, :, kh * gw:(kh + 1) * gw]
        res = (zb * jax.nn.sigmoid(zb) * o_kh).reshape(bs * t_new, gw)
        o_ref[:, kh * gw:(kh + 1) * gw] = res.astype(_BF16)


def _mix_sample(proj3, cache_k, cache_v, state_t, slopes, sinks, w_pool_bf16, pool_scale, b_all, a_all, seq0, bs=16):
    nseq = cache_k.shape[0]
    t_new = proj3.shape[1]
    blk0 = seq0 // bs
    smem = pl.BlockSpec(memory_space=pltpu.SMEM)
    alias = pl.BlockSpec(memory_space=pl.ANY)
    kern = functools.partial(_mix_sample_kernel, bs=bs, t_new=t_new)
    cache_spec = pl.BlockSpec((bs, WINDOW, KV_WIDTH), lambda i: (i, 0, 0))
    in_specs = [
        smem, smem,
        pl.BlockSpec((pl.Element(bs), pl.Element(t_new), pl.Element(COL_GA - COL_Q)),
                     lambda i: ((blk0 + i) * bs, 0, COL_Q)),
        cache_spec, cache_spec,
        pl.BlockSpec((bs, t_new, COL_Q - COL_U), lambda i: (blk0 + i, 0, 0)),
        pl.BlockSpec((POOL_STATE, bs, POOL_WIDTH), lambda i: (0, i, 0)),
        pl.BlockSpec((4, POOL_GROUP_WIDTH, POOL_GROUP_WIDTH), lambda i: (0, 0, 0)),
        pl.BlockSpec((1, POOL_WIDTH), lambda i: (0, 0)),
        alias, alias,
    ]
    operands = (slopes, sinks, proj3, cache_k, cache_v, proj3, state_t, w_pool_bf16, pool_scale, b_all, a_all)
    return pl.pallas_call(
        kern,
        grid=(nseq // bs,),
        in_specs=in_specs,
        out_specs=[
            pl.BlockSpec((bs * t_new, ATTN_WIDTH), lambda i: (blk0 + i, 0)),
            cache_spec, cache_spec,
            pl.BlockSpec((bs * t_new, POOL_WIDTH), lambda i: (blk0 + i, 0)),
            pl.BlockSpec((bs, POOL_STATE, POOL_WIDTH), lambda i: (i, 0, 0)),
        ],
        out_shape=[
            jax.ShapeDtypeStruct(b_all.shape, _BF16),
            jax.ShapeDtypeStruct((nseq, WINDOW, KV_WIDTH), _F32),
            jax.ShapeDtypeStruct((nseq, WINDOW, KV_WIDTH), _F32),
            jax.ShapeDtypeStruct(a_all.shape, _BF16),
            jax.ShapeDtypeStruct((nseq, POOL_STATE, POOL_WIDTH), _F32),
        ],
        input_output_aliases={len(operands) - 2: 0, len(operands) - 1: 3},
        scratch_shapes=[pltpu.VMEM((N_HEADS, t_new, WINDOW + t_new), _F32),
                        pltpu.VMEM((bs, HALO_ROWS, POOL_WIDTH), _F32)],
        compiler_params=_params(("arbitrary",), 48),
        name="mix_sample",
    )(*operands)


def _merge_kernel(a_ref, b_ref, wpc_ref, wac_ref, wpf_ref, waf_ref, ga_ref, gb_ref, h_ref,
                  wp0_ref, wp1_ref, wa0_ref, wa1_ref, *, row_chunk):
    j = pl.program_id(0)
    chunk = wpc_ref.shape[0]
    wrows = pl.ds(pl.multiple_of(pl.program_id(1) * chunk, chunk), chunk)

    def step(wp_next_ref, wa_next_ref, wp_cur_ref, wa_cur_ref):
        wp_next_ref[wrows, :] = wpc_ref[...].astype(_BF16)
        wa_next_ref[wrows, :] = wac_ref[...].astype(_BF16)
        for r in range(0, a_ref.shape[0], row_chunk):
            rows = pl.ds(r, row_chunk)
            ya = jnp.dot(a_ref[rows, :], wp_cur_ref[...], preferred_element_type=_F32)
            yb = jnp.dot(b_ref[rows, :], wa_cur_ref[...], preferred_element_type=_F32)
            h = jax.nn.sigmoid(ga_ref[rows, :]) * ya + jax.nn.sigmoid(gb_ref[rows, :]) * yb
            h_ref[rows, :] = h.astype(_BF16)

    @pl.when(j == 0)
    def _first():
        step(wp1_ref, wa1_ref, wpf_ref, waf_ref)

    @pl.when(j % 2 == 1)
    def _odd():
        step(wp0_ref, wa0_ref, wp1_ref, wa1_ref)

    @pl.when((j > 0) & (j % 2 == 0))
    def _even():
        step(wp1_ref, wa1_ref, wp0_ref, wa0_ref)


def _merge(a, b, proj, w_out_pool, w_out_attn, wp_first, wa_first, steps=STREAM_STEPS, tn=MERGE_TN):
    rows = a.shape[0]
    n_tiles = D_MODEL // tn
    last = n_tiles - 1
    assert rows % steps == 0 and POOL_WIDTH % steps == 0 and ATTN_WIDTH == POOL_WIDTH
    tm = rows // steps
    chunk = POOL_WIDTH // steps

    def stage_spec():
        return pl.BlockSpec((chunk, tn), lambda j, i: (jnp.where(j < last, i, steps - 1), jnp.minimum(j + 1, last)))

    def gate_spec(col0):
        assert col0 % MXU_WIDTH == 0 and tn % MXU_WIDTH == 0 and tm % SUBLANES == 0
        return pl.BlockSpec((pl.Element(tm), pl.Element(tn)),
                            lambda j, i: (pl.multiple_of(i * tm, SUBLANES), pl.multiple_of(col0 + j * tn, MXU_WIDTH)))

    first_spec = pl.BlockSpec((POOL_WIDTH, tn), lambda j, i: (0, 0), pipeline_mode=pl.Buffered(1))
    return pl.pallas_call(
        functools.partial(_merge_kernel, row_chunk=tm // 2),
        grid=(n_tiles, steps),
        in_specs=[
            pl.BlockSpec((tm, POOL_WIDTH), lambda j, i: (i, 0)),
            pl.BlockSpec((tm, ATTN_WIDTH), lambda j, i: (i, 0)),
            stage_spec(), stage_spec(), first_spec, first_spec,
            gate_spec(COL_GA), gate_spec(COL_GB),
        ],
        out_specs=pl.BlockSpec((tm, tn), lambda j, i: (i, j)),
        out_shape=jax.ShapeDtypeStruct((rows, D_MODEL), _BF16),
        scratch_shapes=[pltpu.VMEM((POOL_WIDTH, tn), _BF16)] * 2 + [pltpu.VMEM((ATTN_WIDTH, tn), _BF16)] * 2,
        compiler_params=_params(("arbitrary", "arbitrary"), 58),
        name="merge",
    )(a, b, w_out_pool, w_out_attn, wp_first, wa_first, proj, proj)


def _out_kernel(h_ref, w_ref, x_ref, gain_ref, bias_ref, y_ref, *, sub):
    gain = gain_ref[...]
    bias = bias_ref[...]
    for r0 in range(0, h_ref.shape[0], sub):
        y_ref[pl.ds(r0, sub), :] = jnp.dot(h_ref[pl.ds(r0, sub), :], w_ref[...], preferred_element_type=_F32)
        for g in range(sub // SUBLANES):
            rows = pl.ds(r0 + g * SUBLANES, SUBLANES)
            y = DEEPNORM_ALPHA * x_ref[rows, :] + y_ref[rows, :]
            mu = jnp.mean(y, axis=-1, keepdims=True)
            d = y - mu
            var = jnp.mean(d * d, axis=-1, keepdims=True)
            y_ref[rows, :] = d * lax.rsqrt(var + LN_EPS) * gain + bias


def _out(h_all, row0, w_out_bf16, x, gain, bias, tm=256, sub=128):
    rows = x.shape[0]
    blk0 = row0 // tm
    return pl.pallas_call(
        functools.partial(_out_kernel, sub=sub),
        grid=(rows // tm,),
        in_specs=[
            pl.BlockSpec((tm, D_MODEL), lambda i: (blk0 + i, 0)),
            pl.BlockSpec((D_MODEL, D_MODEL), lambda i: (0, 0), pipeline_mode=pl.Buffered(1)),
            pl.BlockSpec((tm, D_MODEL), lambda i: (i, 0)),
            pl.BlockSpec((1, D_MODEL), lambda i: (0, 0)),
            pl.BlockSpec((1, D_MODEL), lambda i: (0, 0)),
        ],
        out_specs=pl.BlockSpec((tm, D_MODEL), lambda i: (i, 0)),
        out_shape=jax.ShapeDtypeStruct((rows, D_MODEL), _F32),
        compiler_params=_params(("arbitrary",), 60),
        name="out",
    )(h_all, w_out_bf16, x, gain, bias)


def _alibi_slopes():
    h = jnp.arange(N_HEADS, dtype=_F32)
    return 2.0 ** (-8.0 * (h + 1.0) / N_HEADS)


def kernel(x_prompt, x_sample, cache_k, cache_v, state_pool, w_in, w_pool, pool_scale, sinks,
           w_out_pool, w_out_attn, w_out, ln_gain, ln_bias):
    depth = w_in.shape[0]
    assert depth == 1
    bp, tp, _ = x_prompt.shape
    bs, ts, _ = x_sample.shape
    slopes = _alibi_slopes()

    l = 0
    w_pool_b = w_pool[l].astype(_BF16)
    scale = pool_scale[l].reshape(1, POOL_WIDTH)
    gain = ln_gain[l].reshape(1, D_MODEL)
    bias = ln_bias[l].reshape(1, D_MODEL)
    sink = sinks[l]

    xp2 = x_prompt.reshape(bp * tp, D_MODEL)
    xs2 = x_sample.reshape(bs * ts, D_MODEL)
    rows_p = bp * tp
    rows_all = rows_p + bs * ts
    x_all, proj = _proj_cast(xp2, w_in[l], 0, rows_all)
    x_all, proj = _proj_cast(xs2, w_in[l], rows_p, rows_all, x_all, proj)
    ckt = jnp.transpose(cache_k[l], (0, 2, 3, 1)).reshape(bs, KV_WIDTH, WINDOW)
    cvt = jnp.transpose(cache_v[l], (0, 2, 3, 1)).reshape(bs, KV_WIDTH, WINDOW)
    proj, w_o_b, ck, cv, wp_first, wa_first = _proj_main(
        x_all, w_in[l], proj, w_out[l], ckt, cvt, w_out_pool[l], w_out_attn[l])
    proj3 = proj.reshape(proj.shape[0] // ts, ts, N_IN)

    b_all, a_all = _mix_prompt(proj, slopes, sink, w_pool_b, scale, bp, tp)
    state_t = jnp.transpose(state_pool[l], (1, 0, 2))
    b_all, new_k_s, new_v_s, a_all, new_pool_s = _mix_sample(
        proj3, ck, cv, state_t, slopes, sink, w_pool_b, scale, b_all, a_all, rows_p // ts)

    h_all = _merge(a_all, b_all, proj, w_out_pool[l], w_out_attn[l], wp_first, wa_first)
    y_p = _out(h_all, 0, w_o_b, xp2, gain, bias).reshape(bp, tp, D_MODEL)
    y_s = _out(h_all, rows_p, w_o_b, xs2, gain, bias).reshape(bs, ts, D_MODEL)

    def tail(n_rows, col0, width):
        return jnp.stack([lax.slice(proj, ((b + 1) * tp - n_rows, col0), ((b + 1) * tp, col0 + width))
                          for b in range(bp)])

    new_k_p = tail(WINDOW, COL_K, KV_WIDTH).reshape(1, bp, WINDOW, N_KV_HEADS, HEAD_DIM)
    new_v_p = tail(WINDOW, COL_V, KV_WIDTH).reshape(1, bp, WINDOW, N_KV_HEADS, HEAD_DIM)
    new_pool_p = tail(POOL_STATE, COL_U, POOL_WIDTH)[None]
    new_k_s = new_k_s.reshape(1, bs, WINDOW, N_KV_HEADS, HEAD_DIM)
    new_v_s = new_v_s.reshape(1, bs, WINDOW, N_KV_HEADS, HEAD_DIM)

    return (y_p, y_s, new_k_p, new_v_p, new_pool_p, new_k_s, new_v_s, new_pool_s[None])
```

```python
import functools

import jax
import jax.numpy as jnp
from jax import lax
from jax.experimental import pallas as pl
from jax.experimental.pallas import tpu as pltpu

D_MODEL = 4096
POOL_WIDTH = 2048
POOL_WINDOWS = (2, 4, 8, 16)
POOL_GROUP_WIDTH = 512
POOL_STATE = 15
HEAD_DIM = 64
N_HEADS = 32
N_KV_HEADS = 4
GQA_GROUP = 8
ATTN_WIDTH = 2048
KV_WIDTH = 256
WINDOW = 128
PAST_LEN = 8192
N_IN = 16896
COL_U, COL_ZA, COL_Q, COL_K, COL_V, COL_ZB, COL_GA, COL_GB = 0, 2048, 4096, 6144, 6400, 6656, 8704, 12800
DEEPNORM_ALPHA = 2.0 ** 0.25
LN_EPS = 1e-5

V7X_VMEM_BYTES = 64 * 1024 * 1024
SUBLANES = 8
HALO_ROWS = 16
MXU_WIDTH = 256
PROJ_TN = 3 * MXU_WIDTH
PROJ_MAIN_TN = 7 * MXU_WIDTH
MERGE_TN = 4 * MXU_WIDTH
WO_SLICE_ROWS = 32
MIX_BLOCKS = 4
STREAM_STEPS = 16

_F32 = jnp.float32
_BF16 = jnp.bfloat16


def _params(sem, vmem_mib):
    return pltpu.CompilerParams(dimension_semantics=sem, vmem_limit_bytes=vmem_mib * 1024 * 1024)


def _stream_cast_steps(n_tiles, steps_per_pass):
    last = n_tiles - 1

    def row_blk(j, i):
        return jnp.where(j > 0, i, 0)

    def col_blk(j, i):
        return jnp.maximum(j - 1, 0)

    def chunk_blk(j, i):
        return jnp.where(j <= last, i, steps_per_pass - 1)

    def stage_blk(j, i):
        return jnp.minimum(j, last)

    return row_blk, col_blk, chunk_blk, stage_blk


def _proj_main_kernel(x_ref, wc_ref, wo_ref, ckt_ref, cvt_ref, proj_in_ref,
                      o_ref, wo_bf16_ref, ck_ref, cv_ref, wb0_ref, wb1_ref):
    del proj_in_ref
    j = pl.program_id(0)
    chunk = wc_ref.shape[0]
    rows = pl.ds(pl.multiple_of(pl.program_id(1) * chunk, chunk), chunk)

    def step(w_next_ref, w_cur_ref):
        wo_bf16_ref[...] = wo_ref[...].astype(_BF16)
        ck_ref[0] = ckt_ref[0].T
        cv_ref[0] = cvt_ref[0].T
        w_next_ref[rows, :] = wc_ref[...].astype(_BF16)
        if w_cur_ref is not None:
            o_ref[...] = jnp.dot(x_ref[...], w_cur_ref[...], preferred_element_type=_F32)

    @pl.when(j == 0)
    def _stage_only():
        step(wb0_ref, None)

    @pl.when((j > 0) & (j % 2 == 0))
    def _even():
        step(wb0_ref, wb1_ref)

    @pl.when(j % 2 == 1)
    def _odd():
        step(wb1_ref, wb0_ref)


def _proj_main(x_bf16, w_f32, proj, w_out_f32, cache_kt, cache_vt, steps=STREAM_STEPS, tn=PROJ_MAIN_TN):
    rows, k = x_bf16.shape
    n = w_f32.shape[1]
    n_tiles = (n - PROJ_TN) // tn
    assert n_tiles * tn + PROJ_TN == n and rows % steps == 0 and k % steps == 0
    tm = rows // steps
    chunk = k // steps
    total_steps = (n_tiles + 1) * steps
    row_blk, col_blk, chunk_blk, stage_blk = _stream_cast_steps(n_tiles, steps)
    n_slices = w_out_f32.shape[0] // WO_SLICE_ROWS
    nseq, feat, win = cache_kt.shape
    assert n_slices <= total_steps and nseq <= total_steps
    wo_spec = pl.BlockSpec((WO_SLICE_ROWS, w_out_f32.shape[1]),
                           lambda j, i: (jnp.minimum(j * steps + i, n_slices - 1), 0))

    def seq_spec(d1, d2):
        return pl.BlockSpec((1, d1, d2), lambda j, i: (jnp.minimum(j * steps + i, nseq - 1), 0, 0))

    return pl.pallas_call(
        _proj_main_kernel,
        grid=(n_tiles + 1, steps),
        in_specs=[pl.BlockSpec((tm, k), lambda j, i: (row_blk(j, i), 0)),
                  pl.BlockSpec((chunk, tn), lambda j, i: (chunk_blk(j, i), stage_blk(j, i))),
                  wo_spec, seq_spec(feat, win), seq_spec(feat, win),
                  pl.BlockSpec(memory_space=pl.ANY)],
        out_specs=[pl.BlockSpec((tm, tn), lambda j, i: (row_blk(j, i), col_blk(j, i))), wo_spec,
                   seq_spec(win, feat), seq_spec(win, feat)],
        out_shape=[jax.ShapeDtypeStruct((rows, n), _F32), jax.ShapeDtypeStruct(w_out_f32.shape, _BF16),
                   jax.ShapeDtypeStruct((nseq, win, feat), _F32), jax.ShapeDtypeStruct((nseq, win, feat), _F32)],
        input_output_aliases={5: 0},
        scratch_shapes=[pltpu.VMEM((k, tn), _BF16), pltpu.VMEM((k, tn), _BF16)],
        compiler_params=_params(("arbitrary", "arbitrary"), 58),
        name="proj",
    )(x_bf16, w_f32, w_out_f32, cache_kt, cache_vt, proj)


def _proj_cast_kernel(x_ref, w_ref, *refs):
    xb_ref, o_ref, wb_ref = refs[-3:]

    @pl.when(pl.program_id(0) == 0)
    def _cast_weight_tile():
        wb_ref[...] = w_ref[...].astype(_BF16)

    xb = x_ref[...].astype(_BF16)
    xb_ref[...] = xb
    o_ref[...] = jnp.dot(xb, wb_ref[...], preferred_element_type=_F32)


def _proj_cast(x_f32, w_f32, row0, total_rows, x_all=None, proj=None, tm=512, tn=PROJ_TN):
    rows, k = x_f32.shape
    n = w_f32.shape[1]
    blk0 = row0 // tm
    col_blk = n // tn - 1
    aliased = [] if x_all is None else [x_all, proj]
    return pl.pallas_call(
        _proj_cast_kernel,
        grid=(rows // tm,),
        in_specs=[pl.BlockSpec((tm, k), lambda i: (i, 0)),
                  pl.BlockSpec((k, tn), lambda i: (0, col_blk), pipeline_mode=pl.Buffered(1))]
                 + [pl.BlockSpec(memory_space=pl.ANY)] * len(aliased),
        out_specs=[pl.BlockSpec((tm, k), lambda i: (blk0 + i, 0)),
                   pl.BlockSpec((tm, tn), lambda i: (blk0 + i, col_blk))],
        out_shape=[jax.ShapeDtypeStruct((total_rows, k), _BF16),
                   jax.ShapeDtypeStruct((total_rows, n), _F32)],
        input_output_aliases={2: 0, 3: 1} if aliased else {},
        scratch_shapes=[pltpu.VMEM((k, tn), _BF16)],
        compiler_params=_params(("arbitrary",), 56),
        name="proj_cast",
    )(x_f32, w_f32, *aliased)


def _window_sums(ext, n_doublings):
    s = ext
    for d in range(n_doublings):
        s = s + pltpu.roll(s, 1 << d, axis=0)
    return s


def _pool_groups(ext, u, z_a, pos, wp_ref, scale_ref, halo):
    outs = []
    for g, w in enumerate(POOL_WINDOWS):
        sl = slice(g * POOL_GROUP_WIDTH, (g + 1) * POOL_GROUP_WIDTH)
        s = _window_sums(ext[:, sl], g + 1)[halo:]
        count = jnp.minimum(w, pos + 1).astype(_F32)
        pooled = s / count - u[:, sl]
        mixed = jnp.dot(pooled.astype(_BF16), wp_ref[g], preferred_element_type=_F32)
        pool_out = mixed * scale_ref[:, sl]
        z = z_a[:, sl]
        outs.append(z * jax.nn.sigmoid(z) * pool_out)
    return outs


def _pool_prompt_tile(t, u_ref, halo_ref, za_ref, wp_ref, scale_ref, a_ref):
    tm = u_ref.shape[0]
    u = u_ref[...]
    halo = jnp.where(t == 0, 0.0, halo_ref[...])
    ext = jnp.concatenate([halo, u], axis=0)
    pos = t * tm + lax.broadcasted_iota(jnp.int32, (tm, 1), 0)
    outs = _pool_groups(ext, u, za_ref[...], pos, wp_ref, scale_ref, HALO_ROWS)
    for g, o in enumerate(outs):
        a_ref[:, g * POOL_GROUP_WIDTH:(g + 1) * POOL_GROUP_WIDTH] = o.astype(_BF16)


def _pool_sample_tile(u_ref, za_ref, stt_ref, wp_ref, scale_ref, a_ref, np_ref, st_ref, bs, t_new):
    st_ref[:, 0, :] = jnp.zeros((bs, POOL_WIDTH), _F32)
    for t in range(POOL_STATE):
        st_ref[:, t + 1, :] = stt_ref[t]
    u3 = u_ref[...]
    st = st_ref[...]
    rows_per_seq = HALO_ROWS + t_new
    ext = jnp.concatenate([st, u3], axis=1).reshape(bs * rows_per_seq, POOL_WIDTH)
    pos3 = PAST_LEN + lax.broadcasted_iota(jnp.int32, (bs, t_new, 1), 1)
    pos = pos3.reshape(bs * t_new, 1)
    u = u3.reshape(bs * t_new, POOL_WIDTH)
    za = za_ref[...].reshape(bs * t_new, POOL_WIDTH)
    for g, w in enumerate(POOL_WINDOWS):
        sl = slice(g * POOL_GROUP_WIDTH, (g + 1) * POOL_GROUP_WIDTH)
        s = _window_sums(ext[:, sl], g + 1)
        s = s.reshape(bs, rows_per_seq, POOL_GROUP_WIDTH)[:, HALO_ROWS:, :].reshape(bs * t_new, POOL_GROUP_WIDTH)
        count = jnp.minimum(w, pos + 1).astype(_F32)
        pooled = s / count - u[:, sl]
        mixed = jnp.dot(pooled.astype(_BF16), wp_ref[g], preferred_element_type=_F32)
        pool_out = mixed * scale_ref[:, sl]
        z = za[:, sl]
        a_ref[:, sl] = (z * jax.nn.sigmoid(z) * pool_out).astype(_BF16)
    keep = POOL_STATE - t_new
    np_ref[:, 0:keep, :] = st[:, HALO_ROWS - keep:, :]
    np_ref[:, keep:, :] = u3


def _softmax_with_sink(l, sink):
    m = jnp.maximum(jnp.max(l, axis=-1, keepdims=True), sink)
    p = jnp.exp(l - m)
    den = jnp.sum(p, axis=-1, keepdims=True) + jnp.exp(sink - m)
    return p * (1.0 / den)


def _penalty(slope, dist, valid):
    return jnp.where(valid, slope * dist.astype(_F32), jnp.inf)


def _attn_prompt_block(q_ref, kvc_ref, kvp_ref, zb_ref, o_ref, pen_ref, sink_ref, tbl):
    blk = WINDOW
    gw = GQA_GROUP * HEAD_DIM
    kv = jnp.concatenate([kvp_ref[...], kvc_ref[...]], axis=0)
    k_all = kv[:, 0:KV_WIDTH].astype(_BF16)
    v_t = kv[:, KV_WIDTH:2 * KV_WIDTH].T.astype(_BF16)
    o_t = []
    for kh in range(N_KV_HEADS):
        hd = slice(kh * HEAD_DIM, (kh + 1) * HEAD_DIM)
        q_kh = (q_ref[:, kh * gw:(kh + 1) * gw] * (HEAD_DIM ** -0.5)).astype(_BF16)
        qs = jnp.concatenate([q_kh[:, g * HEAD_DIM:(g + 1) * HEAD_DIM] for g in range(GQA_GROUP)], axis=0)
        logits_t = lax.dot_general(k_all[:, hd], qs, (((1,), (1,)), ((), ())),
                                   preferred_element_type=_F32)
        l = logits_t - pen_ref[tbl, kh]
        sink = sink_ref[kh]
        m = jnp.maximum(jnp.max(l, axis=0, keepdims=True), sink)
        p = jnp.exp(l - m)
        den = jnp.sum(p, axis=0, keepdims=True) + jnp.exp(sink - m)
        o_t.append(jnp.dot(v_t[hd, :], p.astype(_BF16), preferred_element_type=_F32) * (1.0 / den))
    for pair in range(N_KV_HEADS // 2):
        o_pair = jnp.concatenate([o_t[2 * pair], o_t[2 * pair + 1]], axis=0)
        tiles = [o_pair[:, g * blk:(g + 1) * blk].T for g in range(GQA_GROUP)]
        for j in range(2):
            kh = 2 * pair + j
            o_kh = jnp.concatenate([t[:, j * HEAD_DIM:(j + 1) * HEAD_DIM] for t in tiles], axis=1)
            zb = zb_ref[:, kh * gw:(kh + 1) * gw]
            o_ref[:, kh * gw:(kh + 1) * gw] = (zb * jax.nn.sigmoid(zb) * o_kh).astype(_BF16)


def _mix_prompt_kernel(slopes_ref, sinks_ref, qkvz_ref, kvp_ref, uz_ref, halo_ref, wp_ref, scale_ref,
                       o_ref, a_ref, pen_ref, sink_ref):
    blk = WINDOW
    n = pl.program_id(1)
    u_ref = uz_ref.at[:, COL_U:COL_U + POOL_WIDTH]
    za_ref = uz_ref.at[:, COL_ZA:COL_ZA + POOL_WIDTH]
    q_ref = qkvz_ref.at[:, 0:ATTN_WIDTH]
    kv_ref = qkvz_ref.at[:, COL_K - COL_Q:COL_K - COL_Q + 2 * KV_WIDTH]
    zb_ref = qkvz_ref.at[:, COL_ZB - COL_Q:COL_ZB - COL_Q + ATTN_WIDTH]
    _pool_prompt_tile(n, u_ref, halo_ref, za_ref, wp_ref, scale_ref, a_ref)

    @pl.when((pl.program_id(0) == 0) & (n == 0))
    def _fill_tables():
        si = lax.broadcasted_iota(jnp.int32, (2 * blk, blk), 0)
        ti = lax.broadcasted_iota(jnp.int32, (2 * blk, blk), 1)
        dist = blk + ti - si
        for n_tbl in (0, 1):
            key_pos = (n_tbl - 1) * blk + si
            valid = (dist >= 0) & (dist < WINDOW) & (key_pos >= 0)
            for h in range(N_HEADS):
                kh, g = divmod(h, GQA_GROUP)
                pen_ref[n_tbl, kh, :, g * blk:(g + 1) * blk] = _penalty(slopes_ref[h], dist, valid)
        for h in range(N_HEADS):
            kh, g = divmod(h, GQA_GROUP)
            sink_ref[kh, :, g * blk:(g + 1) * blk] = jnp.full((1, blk), sinks_ref[h], _F32)

    for s in range(MIX_BLOCKS):
        rows = slice(s * blk, (s + 1) * blk)
        prev = kvp_ref if s == 0 else kv_ref.at[(s - 1) * blk:s * blk, :]
        tbl = jnp.minimum(n, 1) if s == 0 else 1
        _attn_prompt_block(q_ref.at[rows, :], kv_ref.at[rows, :], prev, zb_ref.at[rows, :], o_ref.at[rows, :],
                           pen_ref, sink_ref, tbl)


def _mix_prompt(proj, slopes, sinks, w_pool_bf16, pool_scale, batch, seq):
    blk = WINDOW
    tm = MIX_BLOCKS * blk
    nt = seq // tm
    halo_blocks = tm // HALO_ROWS
    smem = pl.BlockSpec(memory_space=pltpu.SMEM)
    rows = proj.shape[0]
    return pl.pallas_call(
        _mix_prompt_kernel,
        grid=(batch, nt),
        in_specs=[
            smem, smem,
            pl.BlockSpec((pl.Element(tm), pl.Element(COL_GA - COL_Q)), lambda b, n: ((b * nt + n) * tm, COL_Q)),
            pl.BlockSpec((blk, 2 * KV_WIDTH),
                         lambda b, n: (jnp.maximum((b * nt + n) * MIX_BLOCKS - 1, 0), COL_K // (2 * KV_WIDTH))),
            pl.BlockSpec((tm, COL_Q - COL_U), lambda b, n: (b * nt + n, 0)),
            pl.BlockSpec((HALO_ROWS, POOL_WIDTH),
                         lambda b, n: (jnp.maximum((b * nt + n) * halo_blocks - 1, 0), COL_U // POOL_WIDTH)),
            pl.BlockSpec((4, POOL_GROUP_WIDTH, POOL_GROUP_WIDTH), lambda b, n: (0, 0, 0)),
            pl.BlockSpec((1, POOL_WIDTH), lambda b, n: (0, 0)),
        ],
        out_specs=[pl.BlockSpec((tm, ATTN_WIDTH), lambda b, n: (b * nt + n, 0)),
                   pl.BlockSpec((tm, POOL_WIDTH), lambda b, n: (b * nt + n, 0))],
        out_shape=[jax.ShapeDtypeStruct((rows, ATTN_WIDTH), _BF16),
                   jax.ShapeDtypeStruct((rows, POOL_WIDTH), _BF16)],
        scratch_shapes=[pltpu.VMEM((2, N_KV_HEADS, 2 * blk, GQA_GROUP * blk), _F32),
                        pltpu.VMEM((N_KV_HEADS, 1, GQA_GROUP * blk), _F32)],
        compiler_params=_params(("arbitrary", "arbitrary"), 62),
        name="mix_prompt",
    )(slopes, sinks, proj, proj, proj, proj, w_pool_bf16, pool_scale)


def _mix_sample_kernel(slopes_ref, sinks_ref, qkvz_ref, ck_ref, cv_ref, uz_ref, stt_ref, wp_ref, scale_ref,
                       b_in_ref, a_in_ref, o_ref, nk_ref, nv_ref, a_ref, np_ref, pen_ref, st_ref, *, bs, t_new):
    del b_in_ref, a_in_ref
    u_ref = uz_ref.at[:, :, COL_U:COL_U + POOL_WIDTH]
    za_ref = uz_ref.at[:, :, COL_ZA:COL_ZA + POOL_WIDTH]
    q_ref = qkvz_ref.at[:, :, 0:ATTN_WIDTH]
    kvn_ref = qkvz_ref.at[:, :, COL_K - COL_Q:COL_K - COL_Q + 2 * KV_WIDTH]
    zb_ref = qkvz_ref.at[:, :, COL_ZB - COL_Q:COL_ZB - COL_Q + ATTN_WIDTH]
    _pool_sample_tile(u_ref, za_ref, stt_ref, wp_ref, scale_ref, a_ref, np_ref, st_ref, bs, t_new)
    s_all = WINDOW + t_new
    gw = GQA_GROUP * HEAD_DIM

    @pl.when(pl.program_id(0) == 0)
    def _fill_penalty_table():
        ti = lax.broadcasted_iota(jnp.int32, (t_new, s_all), 0)
        si = lax.broadcasted_iota(jnp.int32, (t_new, s_all), 1)
        dist = WINDOW + ti - si
        key_pos = PAST_LEN - WINDOW + si
        valid = (dist >= 0) & (dist < WINDOW) & (key_pos >= 0)
        for h in range(N_HEADS):
            pen_ref[h] = _penalty(slopes_ref[h], dist, valid)

    kvn = kvn_ref[...]
    k_all = jnp.concatenate([ck_ref[...], kvn[:, :, 0:KV_WIDTH]], axis=1)
    v_all = jnp.concatenate([cv_ref[...], kvn[:, :, KV_WIDTH:2 * KV_WIDTH]], axis=1)
    nk_ref[...] = k_all[:, t_new:, :]
    nv_ref[...] = v_all[:, t_new:, :]
    kb = k_all.astype(_BF16)
    vb = v_all.astype(_BF16)
    for kh in range(N_KV_HEADS):
        q_kh = (q_ref[:, :, kh * gw:(kh + 1) * gw] * (HEAD_DIM ** -0.5)).astype(_BF16)
        qs = jnp.concatenate([q_kh[:, :, g * HEAD_DIM:(g + 1) * HEAD_DIM] for g in range(GQA_GROUP)], axis=1)
        hd = slice(kh * HEAD_DIM, (kh + 1) * HEAD_DIM)
        logits = jnp.stack(
            [lax.dot_general(qs[b], kb[b, :, hd], (((1,), (1,)), ((), ())), preferred_element_type=_F32)
             for b in range(bs)], axis=0)
        probs = []
        for g in range(GQA_GROUP):
            h = kh * GQA_GROUP + g
            l = logits[:, g * t_new:(g + 1) * t_new, :] - pen_ref[h]
            probs.append(_softmax_with_sink(l, sinks_ref[h]).astype(_BF16))
        pr = jnp.concatenate(probs, axis=1)
        o = jnp.stack([jnp.dot(pr[b], vb[b, :, hd], preferred_element_type=_F32) for b in range(bs)], axis=0)
        o_kh = jnp.concatenate([o[:, g * t_new:(g + 1) * t_new, :] for g in range(GQA_GROUP)], axis=2)
        zb = zb_ref[:, :, kh * gw:(kh + 1) * gw]
        res = (zb * jax.nn.sigmoid(zb) * o_kh).reshape(bs * t_new, gw)
        o_ref[:, kh * gw:(kh + 1) * gw] = res.astype(_BF16)


def _mix_sample(proj3, cache_k, cache_v, state_t, slopes, sinks, w_pool_bf16, pool_scale, b_all, a_all, seq0, bs=16):
    nseq = cache_k.shape[0]
    t_new = proj3.shape[1]
    blk0 = seq0 // bs
    smem = pl.BlockSpec(memory_space=pltpu.SMEM)
    alias = pl.BlockSpec(memory_space=pl.ANY)
    kern = functools.partial(_mix_sample_kernel, bs=bs, t_new=t_new)
    cache_spec = pl.BlockSpec((bs, WINDOW, KV_WIDTH), lambda i: (i, 0, 0))
    in_specs = [
        smem, smem,
        pl.BlockSpec((pl.Element(bs), pl.Element(t_new), pl.Element(COL_GA - COL_Q)),
                     lambda i: ((blk0 + i) * bs, 0, COL_Q)),
        cache_spec, cache_spec,
        pl.BlockSpec((bs, t_new, COL_Q - COL_U), lambda i: (blk0 + i, 0, 0)),
        pl.BlockSpec((POOL_STATE, bs, POOL_WIDTH), lambda i: (0, i, 0)),
        pl.BlockSpec((4, POOL_GROUP_WIDTH, POOL_GROUP_WIDTH), lambda i: (0, 0, 0)),
        pl.BlockSpec((1, POOL_WIDTH), lambda i: (0, 0)),
        alias, alias,
    ]
    operands = (slopes, sinks, proj3, cache_k, cache_v, proj3, state_t, w_pool_bf16, pool_scale, b_all, a_all)
    return pl.pallas_call(
        kern,
        grid=(nseq // bs,),
        in_specs=in_specs,
        out_specs=[
            pl.BlockSpec((bs * t_new, ATTN_WIDTH), lambda i: (blk0 + i, 0)),
            cache_spec, cache_spec,
            pl.BlockSpec((bs * t_new, POOL_WIDTH), lambda i: (blk0 + i, 0)),
            pl.BlockSpec((bs, POOL_STATE, POOL_WIDTH), lambda i: (i, 0, 0)),
        ],
        out_shape=[
            jax.ShapeDtypeStruct(b_all.shape, _BF16),
            jax.ShapeDtypeStruct((nseq, WINDOW, KV_WIDTH), _F32),
            jax.ShapeDtypeStruct((nseq, WINDOW, KV_WIDTH), _F32),
            jax.ShapeDtypeStruct(a_all.shape, _BF16),
            jax.ShapeDtypeStruct((nseq, POOL_STATE, POOL_WIDTH), _F32),
        ],
        input_output_aliases={len(operands) - 2: 0, len(operands) - 1: 3},
        scratch_shapes=[pltpu.VMEM((N_HEADS, t_new, WINDOW + t_new), _F32),
                        pltpu.VMEM((bs, HALO_ROWS, POOL_WIDTH), _F32)],
        compiler_params=_params(("arbitrary",), 48),
        name="mix_sample",
    )(*operands)


def _merge_kernel(a_ref, b_ref, wpc_ref, wac_ref, ga_ref, gb_ref, h_ref,
                  wp0_ref, wp1_ref, wa0_ref, wa1_ref, *, row_chunk):
    j = pl.program_id(0)
    chunk = wpc_ref.shape[0]
    wrows = pl.ds(pl.multiple_of(pl.program_id(1) * chunk, chunk), chunk)

    def step(wp_next_ref, wa_next_ref, wp_cur_ref, wa_cur_ref):
        wp_next_ref[wrows, :] = wpc_ref[...].astype(_BF16)
        wa_next_ref[wrows, :] = wac_ref[...].astype(_BF16)
        if wp_cur_ref is None:
            return
        for r in range(0, a_ref.shape[0], row_chunk):
            rows = pl.ds(r, row_chunk)
            ya = jnp.dot(a_ref[rows, :], wp_cur_ref[...], preferred_element_type=_F32)
            yb = jnp.dot(b_ref[rows, :], wa_cur_ref[...], preferred_element_type=_F32)
            h = jax.nn.sigmoid(ga_ref[rows, :]) * ya + jax.nn.sigmoid(gb_ref[rows, :]) * yb
            h_ref[rows, :] = h.astype(_BF16)

    @pl.when(j == 0)
    def _stage_only():
        step(wp0_ref, wa0_ref, None, None)

    @pl.when((j > 0) & (j % 2 == 0))
    def _even():
        step(wp0_ref, wa0_ref, wp1_ref, wa1_ref)

    @pl.when(j % 2 == 1)
    def _odd():
        step(wp1_ref, wa1_ref, wp0_ref, wa0_ref)


def _merge(a, b, proj, w_out_pool, w_out_attn, steps=STREAM_STEPS, tn=MERGE_TN):
    rows = a.shape[0]
    n_tiles = D_MODEL // tn
    assert rows % steps == 0 and POOL_WIDTH % steps == 0 and ATTN_WIDTH == POOL_WIDTH
    tm = rows // steps
    chunk = POOL_WIDTH // steps
    row_blk, col_blk, chunk_blk, stage_blk = _stream_cast_steps(n_tiles, steps)

    def gate_spec(col0):
        assert col0 % MXU_WIDTH == 0 and tn % MXU_WIDTH == 0 and tm % SUBLANES == 0
        return pl.BlockSpec((pl.Element(tm), pl.Element(tn)),
                            lambda j, i: (pl.multiple_of(row_blk(j, i) * tm, SUBLANES),
                                          pl.multiple_of(col0 + col_blk(j, i) * tn, MXU_WIDTH)))

    return pl.pallas_call(
        functools.partial(_merge_kernel, row_chunk=tm // 2),
        grid=(n_tiles + 1, steps),
        in_specs=[
            pl.BlockSpec((tm, POOL_WIDTH), lambda j, i: (row_blk(j, i), 0)),
            pl.BlockSpec((tm, ATTN_WIDTH), lambda j, i: (row_blk(j, i), 0)),
            pl.BlockSpec((chunk, tn), lambda j, i: (chunk_blk(j, i), stage_blk(j, i))),
            pl.BlockSpec((chunk, tn), lambda j, i: (chunk_blk(j, i), stage_blk(j, i))),
            gate_spec(COL_GA), gate_spec(COL_GB),
        ],
        out_specs=pl.BlockSpec((tm, tn), lambda j, i: (row_blk(j, i), col_blk(j, i))),
        out_shape=jax.ShapeDtypeStruct((rows, D_MODEL), _BF16),
        scratch_shapes=[pltpu.VMEM((POOL_WIDTH, tn), _BF16)] * 2 + [pltpu.VMEM((ATTN_WIDTH, tn), _BF16)] * 2,
        compiler_params=_params(("arbitrary", "arbitrary"), 56),
        name="merge",
    )(a, b, w_out_pool, w_out_attn, proj, proj)


def _out_kernel(h_ref, w_ref, x_ref, gain_ref, bias_ref, y_ref, *, sub):
    gain = gain_ref[...]
    bias = bias_ref[...]
    for r0 in range(0, h_ref.shape[0], sub):
        y_ref[pl.ds(r0, sub), :] = jnp.dot(h_ref[pl.ds(r0, sub), :], w_ref[...], preferred_element_type=_F32)
        for g in range(sub // SUBLANES):
            rows = pl.ds(r0 + g * SUBLANES, SUBLANES)
            y = DEEPNORM_ALPHA * x_ref[rows, :] + y_ref[rows, :]
            mu = jnp.mean(y, axis=-1, keepdims=True)
            d = y - mu
            var = jnp.mean(d * d, axis=-1, keepdims=True)
            y_ref[rows, :] = d * lax.rsqrt(var + LN_EPS) * gain + bias


def _out(h_all, row0, w_out_bf16, x, gain, bias, tm=256, sub=128):
    rows = x.shape[0]
    blk0 = row0 // tm
    return pl.pallas_call(
        functools.partial(_out_kernel, sub=sub),
        grid=(rows // tm,),
        in_specs=[
            pl.BlockSpec((tm, D_MODEL), lambda i: (blk0 + i, 0)),
            pl.BlockSpec((D_MODEL, D_MODEL), lambda i: (0, 0), pipeline_mode=pl.Buffered(1)),
            pl.BlockSpec((tm, D_MODEL), lambda i: (i, 0)),
            pl.BlockSpec((1, D_MODEL), lambda i: (0, 0)),
            pl.BlockSpec((1, D_MODEL), lambda i: (0, 0)),
        ],
        out_specs=pl.BlockSpec((tm, D_MODEL), lambda i: (i, 0)),
        out_shape=jax.ShapeDtypeStruct((rows, D_MODEL), _F32),
        compiler_params=_params(("arbitrary",), 60),
        name="out",
    )(h_all, w_out_bf16, x, gain, bias)


def _alibi_slopes():
    h = jnp.arange(N_HEADS, dtype=_F32)
    return 2.0 ** (-8.0 * (h + 1.0) / N_HEADS)


def kernel(x_prompt, x_sample, cache_k, cache_v, state_pool, w_in, w_pool, pool_scale, sinks,
           w_out_pool, w_out_attn, w_out, ln_gain, ln_bias):
    depth = w_in.shape[0]
    assert depth == 1
    bp, tp, _ = x_prompt.shape
    bs, ts, _ = x_sample.shape
    slopes = _alibi_slopes()

    l = 0
    w_pool_b = w_pool[l].astype(_BF16)
    scale = pool_scale[l].reshape(1, POOL_WIDTH)
    gain = ln_gain[l].reshape(1, D_MODEL)
    bias = ln_bias[l].reshape(1, D_MODEL)
    sink = sinks[l]

    xp2 = x_prompt.reshape(bp * tp, D_MODEL)
    xs2 = x_sample.reshape(bs * ts, D_MODEL)
    rows_p = bp * tp
    rows_all = rows_p + bs * ts
    x_all, proj = _proj_cast(xp2, w_in[l], 0, rows_all)
    x_all, proj = _proj_cast(xs2, w_in[l], rows_p, rows_all, x_all, proj)
    ckt = jnp.transpose(cache_k[l], (0, 2, 3, 1)).reshape(bs, KV_WIDTH, WINDOW)
    cvt = jnp.transpose(cache_v[l], (0, 2, 3, 1)).reshape(bs, KV_WIDTH, WINDOW)
    proj, w_o_b, ck, cv = _proj_main(x_all, w_in[l], proj, w_out[l], ckt, cvt)
    proj3 = proj.reshape(proj.shape[0] // ts, ts, N_IN)

    b_all, a_all = _mix_prompt(proj, slopes, sink, w_pool_b, scale, bp, tp)
    state_t = jnp.transpose(state_pool[l], (1, 0, 2))
    b_all, new_k_s, new_v_s, a_all, new_pool_s = _mix_sample(
        proj3, ck, cv, state_t, slopes, sink, w_pool_b, scale, b_all, a_all, rows_p // ts)

    h_all = _merge(a_all, b_all, proj, w_out_pool[l], w_out_attn[l])
    y_p = _out(h_all, 0, w_o_b, xp2, gain, bias).reshape(bp, tp, D_MODEL)
    y_s = _out(h_all, rows_p, w_o_b, xs2, gain, bias).reshape(bs, ts, D_MODEL)

    def tail(n_rows, col0, width):
        return jnp.stack([lax.slice(proj, ((b + 1) * tp - n_rows, col0), ((b + 1) * tp, col0 + width))
                          for b in range(bp)])

    new_k_p = tail(WINDOW, COL_K, KV_WIDTH).reshape(1, bp, WINDOW, N_KV_HEADS, HEAD_DIM)
    new_v_p = tail(WINDOW, COL_V, KV_WIDTH).reshape(1, bp, WINDOW, N_KV_HEADS, HEAD_DIM)
    new_pool_p = tail(POOL_STATE, COL_U, POOL_WIDTH)[None]
    new_k_s = new_k_s.reshape(1, bs, WINDOW, N_KV_HEADS, HEAD_DIM)
    new_v_s = new_v_s.reshape(1, bs, WINDOW, N_KV_HEADS, HEAD_DIM)

    return (y_p, y_s, new_k_p, new_v_p, new_pool_p, new_k_s, new_v_s, new_pool_s[None])
```

```python
import functools

import jax
import jax.numpy as jnp
from jax import lax
from jax.experimental import pallas as pl
from jax.experimental.pallas import tpu as pltpu

D_MODEL = 4096
POOL_WIDTH = 2048
POOL_WINDOWS = (2, 4, 8, 16)
POOL_GROUP_WIDTH = 512
POOL_STATE = 15
HEAD_DIM = 64
N_HEADS = 32
N_KV_HEADS = 4
GQA_GROUP = 8
ATTN_WIDTH = 2048
KV_WIDTH = 256
WINDOW = 128
PAST_LEN = 8192
N_IN = 16896
COL_U, COL_ZA, COL_Q, COL_K, COL_V, COL_ZB, COL_GA, COL_GB = 0, 2048, 4096, 6144, 6400, 6656, 8704, 12800
DEEPNORM_ALPHA = 2.0 ** 0.25
LN_EPS = 1e-5

V7X_VMEM_BYTES = 64 * 1024 * 1024
SUBLANES = 8
HALO_ROWS = 16
MXU_WIDTH = 256
PROJ_TN = 3 * MXU_WIDTH
PROJ_MAIN_TN = 7 * MXU_WIDTH
MERGE_TN = 4 * MXU_WIDTH
WO_SLICE_ROWS = 32
MIX_BLOCKS = 4
STREAM_STEPS = 16

_F32 = jnp.float32
_BF16 = jnp.bfloat16


def _params(sem, vmem_mib):
    return pltpu.CompilerParams(dimension_semantics=sem, vmem_limit_bytes=vmem_mib * 1024 * 1024)


def _stream_cast_steps(n_tiles, steps_per_pass):
    last = n_tiles - 1

    def row_blk(j, i):
        return jnp.where(j > 0, i, 0)

    def col_blk(j, i):
        return jnp.maximum(j - 1, 0)

    def chunk_blk(j, i):
        return jnp.where(j <= last, i, steps_per_pass - 1)

    def stage_blk(j, i):
        return jnp.minimum(j, last)

    return row_blk, col_blk, chunk_blk, stage_blk


def _proj_main_kernel(x_ref, wc_ref, wo_ref, ckt_ref, cvt_ref, proj_in_ref,
                      o_ref, wo_bf16_ref, ck_ref, cv_ref, wb0_ref, wb1_ref):
    del proj_in_ref
    j = pl.program_id(0)
    chunk = wc_ref.shape[0]
    rows = pl.ds(pl.multiple_of(pl.program_id(1) * chunk, chunk), chunk)

    def step(w_next_ref, w_cur_ref):
        wo_bf16_ref[...] = wo_ref[...].astype(_BF16)
        ck_ref[0] = ckt_ref[0].T
        cv_ref[0] = cvt_ref[0].T
        w_next_ref[rows, :] = wc_ref[...].astype(_BF16)
        if w_cur_ref is not None:
            o_ref[...] = jnp.dot(x_ref[...], w_cur_ref[...], preferred_element_type=_F32)

    @pl.when(j == 0)
    def _stage_only():
        step(wb0_ref, None)

    @pl.when((j > 0) & (j % 2 == 0))
    def _even():
        step(wb0_ref, wb1_ref)

    @pl.when(j % 2 == 1)
    def _odd():
        step(wb1_ref, wb0_ref)


def _proj_main(x_bf16, w_f32, proj, w_out_f32, cache_kt, cache_vt, steps=STREAM_STEPS, tn=PROJ_MAIN_TN):
    rows, k = x_bf16.shape
    n = w_f32.shape[1]
    n_tiles = (n - PROJ_TN) // tn
    assert n_tiles * tn + PROJ_TN == n and rows % steps == 0 and k % steps == 0
    tm = rows // steps
    chunk = k // steps
    total_steps = (n_tiles + 1) * steps
    row_blk, col_blk, chunk_blk, stage_blk = _stream_cast_steps(n_tiles, steps)
    n_slices = w_out_f32.shape[0] // WO_SLICE_ROWS
    nseq, feat, win = cache_kt.shape
    assert n_slices <= total_steps and nseq <= total_steps
    wo_spec = pl.BlockSpec((WO_SLICE_ROWS, w_out_f32.shape[1]),
                           lambda j, i: (jnp.minimum(j * steps + i, n_slices - 1), 0))

    def seq_spec(d1, d2):
        return pl.BlockSpec((1, d1, d2), lambda j, i: (jnp.minimum(j * steps + i, nseq - 1), 0, 0))

    return pl.pallas_call(
        _proj_main_kernel,
        grid=(n_tiles + 1, steps),
        in_specs=[pl.BlockSpec((tm, k), lambda j, i: (row_blk(j, i), 0)),
                  pl.BlockSpec((chunk, tn), lambda j, i: (chunk_blk(j, i), stage_blk(j, i))),
                  wo_spec, seq_spec(feat, win), seq_spec(feat, win),
                  pl.BlockSpec(memory_space=pl.ANY)],
        out_specs=[pl.BlockSpec((tm, tn), lambda j, i: (row_blk(j, i), col_blk(j, i))), wo_spec,
                   seq_spec(win, feat), seq_spec(win, feat)],
        out_shape=[jax.ShapeDtypeStruct((rows, n), _F32), jax.ShapeDtypeStruct(w_out_f32.shape, _BF16),
                   jax.ShapeDtypeStruct((nseq, win, feat), _F32), jax.ShapeDtypeStruct((nseq, win, feat), _F32)],
        input_output_aliases={5: 0},
        scratch_shapes=[pltpu.VMEM((k, tn), _BF16), pltpu.VMEM((k, tn), _BF16)],
        compiler_params=_params(("arbitrary", "arbitrary"), 58),
        name="proj",
    )(x_bf16, w_f32, w_out_f32, cache_kt, cache_vt, proj)


def _proj_cast_kernel(x_ref, w_ref, *refs):
    xb_ref, o_ref, wb_ref = refs[-3:]

    @pl.when(pl.program_id(0) == 0)
    def _cast_weight_tile():
        wb_ref[...] = w_ref[...].astype(_BF16)

    xb = x_ref[...].astype(_BF16)
    xb_ref[...] = xb
    o_ref[...] = jnp.dot(xb, wb_ref[...], preferred_element_type=_F32)


def _proj_cast(x_f32, w_f32, row0, total_rows, x_all=None, proj=None, tm=512, tn=PROJ_TN):
    rows, k = x_f32.shape
    n = w_f32.shape[1]
    blk0 = row0 // tm
    col_blk = n // tn - 1
    aliased = [] if x_all is None else [x_all, proj]
    return pl.pallas_call(
        _proj_cast_kernel,
        grid=(rows // tm,),
        in_specs=[pl.BlockSpec((tm, k), lambda i: (i, 0)),
                  pl.BlockSpec((k, tn), lambda i: (0, col_blk), pipeline_mode=pl.Buffered(1))]
                 + [pl.BlockSpec(memory_space=pl.ANY)] * len(aliased),
        out_specs=[pl.BlockSpec((tm, k), lambda i: (blk0 + i, 0)),
                   pl.BlockSpec((tm, tn), lambda i: (blk0 + i, col_blk))],
        out_shape=[jax.ShapeDtypeStruct((total_rows, k), _BF16),
                   jax.ShapeDtypeStruct((total_rows, n), _F32)],
        input_output_aliases={2: 0, 3: 1} if aliased else {},
        scratch_shapes=[pltpu.VMEM((k, tn), _BF16)],
        compiler_params=_params(("arbitrary",), 56),
        name="proj_cast",
    )(x_f32, w_f32, *aliased)


def _window_sums(ext, n_doublings):
    s = ext
    for d in range(n_doublings):
        s = s + pltpu.roll(s, 1 << d, axis=0)
    return s


def _pool_groups(ext, u, z_a, pos, wp_ref, scale_ref, halo):
    outs = []
    for g, w in enumerate(POOL_WINDOWS):
        sl = slice(g * POOL_GROUP_WIDTH, (g + 1) * POOL_GROUP_WIDTH)
        s = _window_sums(ext[:, sl], g + 1)[halo:]
        count = jnp.minimum(w, pos + 1).astype(_F32)
        pooled = s / count - u[:, sl]
        mixed = jnp.dot(pooled.astype(_BF16), wp_ref[g], preferred_element_type=_F32)
        pool_out = mixed * scale_ref[:, sl]
        z = z_a[:, sl]
        outs.append(z * jax.nn.sigmoid(z) * pool_out)
    return outs


def _pool_prompt_tile(t, u_ref, halo_ref, za_ref, wp_ref, scale_ref, a_ref):
    tm = u_ref.shape[0]
    u = u_ref[...]
    halo = jnp.where(t == 0, 0.0, halo_ref[...])
    ext = jnp.concatenate([halo, u], axis=0)
    pos = t * tm + lax.broadcasted_iota(jnp.int32, (tm, 1), 0)
    outs = _pool_groups(ext, u, za_ref[...], pos, wp_ref, scale_ref, HALO_ROWS)
    for g, o in enumerate(outs):
        a_ref[:, g * POOL_GROUP_WIDTH:(g + 1) * POOL_GROUP_WIDTH] = o.astype(_BF16)


def _pool_sample_tile(u_ref, za_ref, stt_ref, wp_ref, scale_ref, a_ref, np_ref, st_ref, bs, t_new):
    st_ref[:, 0, :] = jnp.zeros((bs, POOL_WIDTH), _F32)
    for t in range(POOL_STATE):
        st_ref[:, t + 1, :] = stt_ref[t]
    u3 = u_ref[...]
    st = st_ref[...]
    rows_per_seq = HALO_ROWS + t_new
    ext = jnp.concatenate([st, u3], axis=1).reshape(bs * rows_per_seq, POOL_WIDTH)
    pos3 = PAST_LEN + lax.broadcasted_iota(jnp.int32, (bs, t_new, 1), 1)
    pos = pos3.reshape(bs * t_new, 1)
    u = u3.reshape(bs * t_new, POOL_WIDTH)
    za = za_ref[...].reshape(bs * t_new, POOL_WIDTH)
    for g, w in enumerate(POOL_WINDOWS):
        sl = slice(g * POOL_GROUP_WIDTH, (g + 1) * POOL_GROUP_WIDTH)
        s = _window_sums(ext[:, sl], g + 1)
        s = s.reshape(bs, rows_per_seq, POOL_GROUP_WIDTH)[:, HALO_ROWS:, :].reshape(bs * t_new, POOL_GROUP_WIDTH)
        count = jnp.minimum(w, pos + 1).astype(_F32)
        pooled = s / count - u[:, sl]
        mixed = jnp.dot(pooled.astype(_BF16), wp_ref[g], preferred_element_type=_F32)
        pool_out = mixed * scale_ref[:, sl]
        z = za[:, sl]
        a_ref[:, sl] = (z * jax.nn.sigmoid(z) * pool_out).astype(_BF16)
    keep = POOL_STATE - t_new
    np_ref[:, 0:keep, :] = st[:, HALO_ROWS - keep:, :]
    np_ref[:, keep:, :] = u3


def _softmax_with_sink(l, sink):
    m = jnp.maximum(jnp.max(l, axis=-1, keepdims=True), sink)
    p = jnp.exp(l - m)
    den = jnp.sum(p, axis=-1, keepdims=True) + jnp.exp(sink - m)
    return p, 1.0 / den


def _penalty(slope, dist, valid):
    return jnp.where(valid, slope * dist.astype(_F32), jnp.inf)


def _attn_prompt_block(q_ref, kvc_ref, kvp_ref, zb_ref, o_ref, pen_ref, sink_ref, tbl):
    blk = WINDOW
    gw = GQA_GROUP * HEAD_DIM
    kv = jnp.concatenate([kvp_ref[...], kvc_ref[...]], axis=0)
    k_all = kv[:, 0:KV_WIDTH].astype(_BF16)
    v_t = kv[:, KV_WIDTH:2 * KV_WIDTH].T.astype(_BF16)
    o_t = []
    for kh in range(N_KV_HEADS):
        hd = slice(kh * HEAD_DIM, (kh + 1) * HEAD_DIM)
        q_kh = (q_ref[:, kh * gw:(kh + 1) * gw] * (HEAD_DIM ** -0.5)).astype(_BF16)
        qs = jnp.concatenate([q_kh[:, g * HEAD_DIM:(g + 1) * HEAD_DIM] for g in range(GQA_GROUP)], axis=0)
        logits_t = lax.dot_general(k_all[:, hd], qs, (((1,), (1,)), ((), ())),
                                   preferred_element_type=_F32)
        l = logits_t - pen_ref[tbl, kh]
        sink = sink_ref[kh]
        m = jnp.maximum(jnp.max(l, axis=0, keepdims=True), sink)
        p = jnp.exp(l - m)
        den = jnp.sum(p, axis=0, keepdims=True) + jnp.exp(sink - m)
        o_t.append(jnp.dot(v_t[hd, :], p.astype(_BF16), preferred_element_type=_F32) * (1.0 / den))
    for pair in range(N_KV_HEADS // 2):
        o_pair = jnp.concatenate([o_t[2 * pair], o_t[2 * pair + 1]], axis=0)
        tiles = [o_pair[:, g * blk:(g + 1) * blk].T for g in range(GQA_GROUP)]
        for j in range(2):
            kh = 2 * pair + j
            o_kh = jnp.concatenate([t[:, j * HEAD_DIM:(j + 1) * HEAD_DIM] for t in tiles], axis=1)
            zb = zb_ref[:, kh * gw:(kh + 1) * gw]
            o_ref[:, kh * gw:(kh + 1) * gw] = (zb * jax.nn.sigmoid(zb) * o_kh).astype(_BF16)


def _mix_prompt_kernel(slopes_ref, sinks_ref, qkvz_ref, kvp_ref, uz_ref, halo_ref, wp_ref, scale_ref,
                       o_ref, a_ref, pen_ref, sink_ref):
    blk = WINDOW
    n = pl.program_id(1)
    u_ref = uz_ref.at[:, COL_U:COL_U + POOL_WIDTH]
    za_ref = uz_ref.at[:, COL_ZA:COL_ZA + POOL_WIDTH]
    q_ref = qkvz_ref.at[:, 0:ATTN_WIDTH]
    kv_ref = qkvz_ref.at[:, COL_K - COL_Q:COL_K - COL_Q + 2 * KV_WIDTH]
    zb_ref = qkvz_ref.at[:, COL_ZB - COL_Q:COL_ZB - COL_Q + ATTN_WIDTH]
    _pool_prompt_tile(n, u_ref, halo_ref, za_ref, wp_ref, scale_ref, a_ref)

    @pl.when((pl.program_id(0) == 0) & (n == 0))
    def _fill_tables():
        si = lax.broadcasted_iota(jnp.int32, (2 * blk, blk), 0)
        ti = lax.broadcasted_iota(jnp.int32, (2 * blk, blk), 1)
        dist = blk + ti - si
        for n_tbl in (0, 1):
            key_pos = (n_tbl - 1) * blk + si
            valid = (dist >= 0) & (dist < WINDOW) & (key_pos >= 0)
            for h in range(N_HEADS):
                kh, g = divmod(h, GQA_GROUP)
                pen_ref[n_tbl, kh, :, g * blk:(g + 1) * blk] = _penalty(slopes_ref[h], dist, valid)
        for h in range(N_HEADS):
            kh, g = divmod(h, GQA_GROUP)
            sink_ref[kh, :, g * blk:(g + 1) * blk] = jnp.full((1, blk), sinks_ref[h], _F32)

    for s in range(MIX_BLOCKS):
        rows = slice(s * blk, (s + 1) * blk)
        prev = kvp_ref if s == 0 else kv_ref.at[(s - 1) * blk:s * blk, :]
        tbl = jnp.minimum(n, 1) if s == 0 else 1
        _attn_prompt_block(q_ref.at[rows, :], kv_ref.at[rows, :], prev, zb_ref.at[rows, :], o_ref.at[rows, :],
                           pen_ref, sink_ref, tbl)


def _mix_prompt(proj, slopes, sinks, w_pool_bf16, pool_scale, batch, seq):
    blk = WINDOW
    tm = MIX_BLOCKS * blk
    nt = seq // tm
    halo_blocks = tm // HALO_ROWS
    smem = pl.BlockSpec(memory_space=pltpu.SMEM)
    rows = proj.shape[0]
    return pl.pallas_call(
        _mix_prompt_kernel,
        grid=(batch, nt),
        in_specs=[
            smem, smem,
            pl.BlockSpec((pl.Element(tm), pl.Element(COL_GA - COL_Q)), lambda b, n: ((b * nt + n) * tm, COL_Q)),
            pl.BlockSpec((blk, 2 * KV_WIDTH),
                         lambda b, n: (jnp.maximum((b * nt + n) * MIX_BLOCKS - 1, 0), COL_K // (2 * KV_WIDTH))),
            pl.BlockSpec((tm, COL_Q - COL_U), lambda b, n: (b * nt + n, 0)),
            pl.BlockSpec((HALO_ROWS, POOL_WIDTH),
                         lambda b, n: (jnp.maximum((b * nt + n) * halo_blocks - 1, 0), COL_U // POOL_WIDTH)),
            pl.BlockSpec((4, POOL_GROUP_WIDTH, POOL_GROUP_WIDTH), lambda b, n: (0, 0, 0)),
            pl.BlockSpec((1, POOL_WIDTH), lambda b, n: (0, 0)),
        ],
        out_specs=[pl.BlockSpec((tm, ATTN_WIDTH), lambda b, n: (b * nt + n, 0)),
                   pl.BlockSpec((tm, POOL_WIDTH), lambda b, n: (b * nt + n, 0))],
        out_shape=[jax.ShapeDtypeStruct((rows, ATTN_WIDTH), _BF16),
                   jax.ShapeDtypeStruct((rows, POOL_WIDTH), _BF16)],
        scratch_shapes=[pltpu.VMEM((2, N_KV_HEADS, 2 * blk, GQA_GROUP * blk), _F32),
                        pltpu.VMEM((N_KV_HEADS, 1, GQA_GROUP * blk), _F32)],
        compiler_params=_params(("arbitrary", "arbitrary"), 62),
        name="mix_prompt",
    )(slopes, sinks, proj, proj, proj, proj, w_pool_bf16, pool_scale)


def _mix_sample_kernel(slopes_ref, sinks_ref, qkvz_ref, ck_ref, cv_ref, uz_ref, stt_ref, wp_ref, scale_ref,
                       b_in_ref, a_in_ref, o_ref, nk_ref, nv_ref, a_ref, np_ref, pen_ref, st_ref, *, bs, t_new):
    del b_in_ref, a_in_ref
    u_ref = uz_ref.at[:, :, COL_U:COL_U + POOL_WIDTH]
    za_ref = uz_ref.at[:, :, COL_ZA:COL_ZA + POOL_WIDTH]
    q_ref = qkvz_ref.at[:, :, 0:ATTN_WIDTH]
    kvn_ref = qkvz_ref.at[:, :, COL_K - COL_Q:COL_K - COL_Q + 2 * KV_WIDTH]
    zb_ref = qkvz_ref.at[:, :, COL_ZB - COL_Q:COL_ZB - COL_Q + ATTN_WIDTH]
    _pool_sample_tile(u_ref, za_ref, stt_ref, wp_ref, scale_ref, a_ref, np_ref, st_ref, bs, t_new)
    s_all = WINDOW + t_new
    gw = GQA_GROUP * HEAD_DIM

    @pl.when(pl.program_id(0) == 0)
    def _fill_penalty_table():
        ti = lax.broadcasted_iota(jnp.int32, (t_new, s_all), 0)
        si = lax.broadcasted_iota(jnp.int32, (t_new, s_all), 1)
        dist = WINDOW + ti - si
        key_pos = PAST_LEN - WINDOW + si
        valid = (dist >= 0) & (dist < WINDOW) & (key_pos >= 0)
        for h in range(N_HEADS):
            pen_ref[h] = _penalty(slopes_ref[h], dist, valid)

    kvn = kvn_ref[...]
    k_all = jnp.concatenate([ck_ref[...], kvn[:, :, 0:KV_WIDTH]], axis=1)
    v_all = jnp.concatenate([cv_ref[...], kvn[:, :, KV_WIDTH:2 * KV_WIDTH]], axis=1)
    nk_ref[...] = k_all[:, t_new:, :]
    nv_ref[...] = v_all[:, t_new:, :]
    kb = k_all.astype(_BF16)
    vb = v_all.astype(_BF16)
    for kh in range(N_KV_HEADS):
        q_kh = (q_ref[:, :, kh * gw:(kh + 1) * gw] * (HEAD_DIM ** -0.5)).astype(_BF16)
        qs = jnp.concatenate([q_kh[:, :, g * HEAD_DIM:(g + 1) * HEAD_DIM] for g in range(GQA_GROUP)], axis=1)
        hd = slice(kh * HEAD_DIM, (kh + 1) * HEAD_DIM)
        logits = jnp.stack(
            [lax.dot_general(qs[b], kb[b, :, hd], (((1,), (1,)), ((), ())), preferred_element_type=_F32)
             for b in range(bs)], axis=0)
        probs, scales = [], []
        for g in range(GQA_GROUP):
            h = kh * GQA_GROUP + g
            l = logits[:, g * t_new:(g + 1) * t_new, :] - pen_ref[h]
            p, r = _softmax_with_sink(l, sinks_ref[h])
            probs.append(p.astype(_BF16))
            scales.append(r)
        pr = jnp.concatenate(probs, axis=1)
        o = jnp.stack([jnp.dot(pr[b], vb[b, :, hd], preferred_element_type=_F32) for b in range(bs)], axis=0)
        o_kh = jnp.concatenate([o[:, g * t_new:(g + 1) * t_new, :] * scales[g] for g in range(GQA_GROUP)], axis=2)
        zb = zb_ref[:, :, kh * gw:(kh + 1) * gw]
        res = (zb * jax.nn.sigmoid(zb) * o_kh).reshape(bs * t_new, gw)
        o_ref[:, kh * gw:(kh + 1) * gw] = res.astype(_BF16)


def _mix_sample(proj3, cache_k, cache_v, state_t, slopes, sinks, w_pool_bf16, pool_scale, b_all, a_all, seq0, bs=16):
    nseq = cache_k.shape[0]
    t_new = proj3.shape[1]
    blk0 = seq0 // bs
    smem = pl.BlockSpec(memory_space=pltpu.SMEM)
    alias = pl.BlockSpec(memory_space=pl.ANY)
    kern = functools.partial(_mix_sample_kernel, bs=bs, t_new=t_new)
    cache_spec = pl.BlockSpec((bs, WINDOW, KV_WIDTH), lambda i: (i, 0, 0))
    in_specs = [
        smem, smem,
        pl.BlockSpec((pl.Element(bs), pl.Element(t_new), pl.Element(COL_GA - COL_Q)),
                     lambda i: ((blk0 + i) * bs, 0, COL_Q)),
        cache_spec, cache_spec,
        pl.BlockSpec((bs, t_new, COL_Q - COL_U), lambda i: (blk0 + i, 0, 0)),
        pl.BlockSpec((POOL_STATE, bs, POOL_WIDTH), lambda i: (0, i, 0)),
        pl.BlockSpec((4, POOL_GROUP_WIDTH, POOL_GROUP_WIDTH), lambda i: (0, 0, 0)),
        pl.BlockSpec((1, POOL_WIDTH), lambda i: (0, 0)),
        alias, alias,
    ]
    operands = (slopes, sinks, proj3, cache_k, cache_v, proj3, state_t, w_pool_bf16, pool_scale, b_all, a_all)
    return pl.pallas_call(
        kern,
        grid=(nseq // bs,),
        in_specs=in_specs,
        out_specs=[
            pl.BlockSpec((bs * t_new, ATTN_WIDTH), lambda i: (blk0 + i, 0)),
            cache_spec, cache_spec,
            pl.BlockSpec((bs * t_new, POOL_WIDTH), lambda i: (blk0 + i, 0)),
            pl.BlockSpec((bs, POOL_STATE, POOL_WIDTH), lambda i: (i, 0, 0)),
        ],
        out_shape=[
            jax.ShapeDtypeStruct(b_all.shape, _BF16),
            jax.ShapeDtypeStruct((nseq, WINDOW, KV_WIDTH), _F32),
            jax.ShapeDtypeStruct((nseq, WINDOW, KV_WIDTH), _F32),
            jax.ShapeDtypeStruct(a_all.shape, _BF16),
            jax.ShapeDtypeStruct((nseq, POOL_STATE, POOL_WIDTH), _F32),
        ],
        input_output_aliases={len(operands) - 2: 0, len(operands) - 1: 3},
        scratch_shapes=[pltpu.VMEM((N_HEADS, t_new, WINDOW + t_new), _F32),
                        pltpu.VMEM((bs, HALO_ROWS, POOL_WIDTH), _F32)],
        compiler_params=_params(("arbitrary",), 48),
        name="mix_sample",
    )(*operands)


def _merge_kernel(a_ref, b_ref, wpc_ref, wac_ref, ga_ref, gb_ref, h_ref,
                  wp0_ref, wp1_ref, wa0_ref, wa1_ref, *, row_chunk):
    j = pl.program_id(0)
    chunk = wpc_ref.shape[0]
    wrows = pl.ds(pl.multiple_of(pl.program_id(1) * chunk, chunk), chunk)

    def step(wp_next_ref, wa_next_ref, wp_cur_ref, wa_cur_ref):
        wp_next_ref[wrows, :] = wpc_ref[...].astype(_BF16)
        wa_next_ref[wrows, :] = wac_ref[...].astype(_BF16)
        if wp_cur_ref is None:
            return
        for r in range(0, a_ref.shape[0], row_chunk):
            rows = pl.ds(r, row_chunk)
            ya = jnp.dot(a_ref[rows, :], wp_cur_ref[...], preferred_element_type=_F32)
            yb = jnp.dot(b_ref[rows, :], wa_cur_ref[...], preferred_element_type=_F32)
            h = jax.nn.sigmoid(ga_ref[rows, :]) * ya + jax.nn.sigmoid(gb_ref[rows, :]) * yb
            h_ref[rows, :] = h.astype(_BF16)

    @pl.when(j == 0)
    def _stage_only():
        step(wp0_ref, wa0_ref, None, None)

    @pl.when((j > 0) & (j % 2 == 0))
    def _even():
        step(wp0_ref, wa0_ref, wp1_ref, wa1_ref)

    @pl.when(j % 2 == 1)
    def _odd():
        step(wp1_ref, wa1_ref, wp0_ref, wa0_ref)


def _merge(a, b, proj, w_out_pool, w_out_attn, steps=STREAM_STEPS, tn=MERGE_TN):
    rows = a.shape[0]
    n_tiles = D_MODEL // tn
    assert rows % steps == 0 and POOL_WIDTH % steps == 0 and ATTN_WIDTH == POOL_WIDTH
    tm = rows // steps
    chunk = POOL_WIDTH // steps
    row_blk, col_blk, chunk_blk, stage_blk = _stream_cast_steps(n_tiles, steps)

    def gate_spec(col0):
        assert col0 % MXU_WIDTH == 0 and tn % MXU_WIDTH == 0 and tm % SUBLANES == 0
        return pl.BlockSpec((pl.Element(tm), pl.Element(tn)),
                            lambda j, i: (pl.multiple_of(row_blk(j, i) * tm, SUBLANES),
                                          pl.multiple_of(col0 + col_blk(j, i) * tn, MXU_WIDTH)))

    return pl.pallas_call(
        functools.partial(_merge_kernel, row_chunk=tm // 2),
        grid=(n_tiles + 1, steps),
        in_specs=[
            pl.BlockSpec((tm, POOL_WIDTH), lambda j, i: (row_blk(j, i), 0)),
            pl.BlockSpec((tm, ATTN_WIDTH), lambda j, i: (row_blk(j, i), 0)),
            pl.BlockSpec((chunk, tn), lambda j, i: (chunk_blk(j, i), stage_blk(j, i))),
            pl.BlockSpec((chunk, tn), lambda j, i: (chunk_blk(j, i), stage_blk(j, i))),
            gate_spec(COL_GA), gate_spec(COL_GB),
        ],
        out_specs=pl.BlockSpec((tm, tn), lambda j, i: (row_blk(j, i), col_blk(j, i))),
        out_shape=jax.ShapeDtypeStruct((rows, D_MODEL), _BF16),
        scratch_shapes=[pltpu.VMEM((POOL_WIDTH, tn), _BF16)] * 2 + [pltpu.VMEM((ATTN_WIDTH, tn), _BF16)] * 2,
        compiler_params=_params(("arbitrary", "arbitrary"), 56),
        name="merge",
    )(a, b, w_out_pool, w_out_attn, proj, proj)


def _out_kernel(h_ref, w_ref, x_ref, gain_ref, bias_ref, y_ref, *, sub):
    gain = gain_ref[...]
    bias = bias_ref[...]
    for r0 in range(0, h_ref.shape[0], sub):
        y_ref[pl.ds(r0, sub), :] = jnp.dot(h_ref[pl.ds(r0, sub), :], w_ref[...], preferred_element_type=_F32)
        for g in range(sub // SUBLANES):
            rows = pl.ds(r0 + g * SUBLANES, SUBLANES)
            y = DEEPNORM_ALPHA * x_ref[rows, :] + y_ref[rows, :]
            mu = jnp.mean(y, axis=-1, keepdims=True)
            d = y - mu
            var = jnp.mean(d * d, axis=-1, keepdims=True)
            y_ref[rows, :] = d * lax.rsqrt(var + LN_EPS) * gain + bias


def _out(h_all, row0, w_out_bf16, x, gain, bias, tm=256, sub=128):
    rows = x.shape[0]
    blk0 = row0 // tm
    return pl.pallas_call(
        functools.partial(_out_kernel, sub=sub),
        grid=(rows // tm,),
        in_specs=[
            pl.BlockSpec((tm, D_MODEL), lambda i: (blk0 + i, 0)),
            pl.BlockSpec((D_MODEL, D_MODEL), lambda i: (0, 0), pipeline_mode=pl.Buffered(1)),
            pl.BlockSpec((tm, D_MODEL), lambda i: (i, 0)),
            pl.BlockSpec((1, D_MODEL), lambda i: (0, 0)),
            pl.BlockSpec((1, D_MODEL), lambda i: (0, 0)),
        ],
        out_specs=pl.BlockSpec((tm, D_MODEL), lambda i: (i, 0)),
        out_shape=jax.ShapeDtypeStruct((rows, D_MODEL), _F32),
        compiler_params=_params(("arbitrary",), 60),
        name="out",
    )(h_all, w_out_bf16, x, gain, bias)


def _alibi_slopes():
    h = jnp.arange(N_HEADS, dtype=_F32)
    return 2.0 ** (-8.0 * (h + 1.0) / N_HEADS)


def kernel(x_prompt, x_sample, cache_k, cache_v, state_pool, w_in, w_pool, pool_scale, sinks,
           w_out_pool, w_out_attn, w_out, ln_gain, ln_bias):
    depth = w_in.shape[0]
    assert depth == 1
    bp, tp, _ = x_prompt.shape
    bs, ts, _ = x_sample.shape
    slopes = _alibi_slopes()

    l = 0
    w_pool_b = w_pool[l].astype(_BF16)
    scale = pool_scale[l].reshape(1, POOL_WIDTH)
    gain = ln_gain[l].reshape(1, D_MODEL)
    bias = ln_bias[l].reshape(1, D_MODEL)
    sink = sinks[l]

    xp2 = x_prompt.reshape(bp * tp, D_MODEL)
    xs2 = x_sample.reshape(bs * ts, D_MODEL)
    rows_p = bp * tp
    rows_all = rows_p + bs * ts
    x_all, proj = _proj_cast(xp2, w_in[l], 0, rows_all)
    x_all, proj = _proj_cast(xs2, w_in[l], rows_p, rows_all, x_all, proj)
    ckt = jnp.transpose(cache_k[l], (0, 2, 3, 1)).reshape(bs, KV_WIDTH, WINDOW)
    cvt = jnp.transpose(cache_v[l], (0, 2, 3, 1)).reshape(bs, KV_WIDTH, WINDOW)
    proj, w_o_b, ck, cv = _proj_main(x_all, w_in[l], proj, w_out[l], ckt, cvt)
    proj3 = proj.reshape(proj.shape[0] // ts, ts, N_IN)

    b_all, a_all = _mix_prompt(proj, slopes, sink, w_pool_b, scale, bp, tp)
    state_t = jnp.transpose(state_pool[l], (1, 0, 2))
    b_all, new_k_s, new_v_s, a_all, new_pool_s = _mix_sample(
        proj3, ck, cv, state_t, slopes, sink, w_pool_b, scale, b_all, a_all, rows_p // ts)

    h_all = _merge(a_all, b_all, proj, w_out_pool[l], w_out_attn[l])
    y_p = _out(h_all, 0, w_o_b, xp2, gain, bias).reshape(bp, tp, D_MODEL)
    y_s = _out(h_all, rows_p, w_o_b, xs2, gain, bias).reshape(bs, ts, D_MODEL)

    def tail(n_rows, col0, width):
        return jnp.stack([lax.slice(proj, ((b + 1) * tp - n_rows, col0), ((b + 1) * tp, col0 + width))
                          for b in range(bp)])

    new_k_p = tail(WINDOW, COL_K, KV_WIDTH).reshape(1, bp, WINDOW, N_KV_HEADS, HEAD_DIM)
    new_v_p = tail(WINDOW, COL_V, KV_WIDTH).reshape(1, bp, WINDOW, N_KV_HEADS, HEAD_DIM)
    new_pool_p = tail(POOL_STATE, COL_U, POOL_WIDTH)[None]
    new_k_s = new_k_s.reshape(1, bs, WINDOW, N_KV_HEADS, HEAD_DIM)
    new_v_s = new_v_s.reshape(1, bs, WINDOW, N_KV_HEADS, HEAD_DIM)

    return (y_p, y_s, new_k_p, new_v_p, new_pool_p, new_k_s, new_v_s, new_pool_s[None])
```
